```python
import jax, jax.numpy as jnp
from jax import lax
import numpy as np

D_MODEL = 1024
BATCH = 8
SEQ = 4096
DEPTH = 1

GRID_W = 64
CTX_LEN = 256
D_MIX = D_MODEL
RET_DIM = 64
RET_WIDTH = D_MIX // 2
RET_HEADS = RET_WIDTH // RET_DIM
RWKV_DIM = 64
RWKV_WIDTH = D_MIX - RET_WIDTH
RWKV_HEADS = RWKV_WIDTH // RWKV_DIM
DECAY_LORA = 64
ICLR_LORA = 64
GATE_LORA = 128
CHUNK = 128
CONV_W = 3
N_EXPERTS = 16
EC_CAPACITY = 2
D_EXPERT = D_MODEL
ROPE_BASE = 10000.0
NORM_EPS = 1e-6
RWKV_GN_EPS = 64e-5
RET_COLS = 5 * RET_WIDTH
RWKV_SPLITS = (RWKV_WIDTH, RWKV_WIDTH, RWKV_WIDTH, DECAY_LORA, DECAY_LORA, ICLR_LORA, GATE_LORA, GATE_LORA)
RWKV_COLS = sum(RWKV_SPLITS)
D_IN = RET_COLS + RWKV_COLS

kernel_name = 'hybrid_retention_rwkv7_ecmoe_dit_layer'


def split_cols(u, sizes):
    outs, start = [], 0
    for s in sizes:
        outs.append(u[..., start:start + s])
        start += s
    return outs


def flip(t):
    return jnp.flip(t, axis=1)


def rms_norm(x, g):
    xf = x.astype(jnp.float32)
    y = xf * lax.rsqrt(jnp.mean(xf * xf, axis=-1, keepdims=True) + NORM_EPS)
    return (y * g.astype(jnp.float32)).astype(x.dtype)


def head_rms(y):
    y = y * lax.rsqrt(jnp.mean(y * y, axis=-1, keepdims=True) + NORM_EPS)
    return y.reshape(y.shape[0], y.shape[1], -1)


def depthwise_conv(x, w):
    return lax.conv_general_dilated(x, w[:, None, :].astype(x.dtype), window_strides=(1,), padding='SAME',
                                    dimension_numbers=('NWC', 'WIO', 'NWC'), feature_group_count=x.shape[-1])


def axial_rope(x):
    n_tok, d = x.shape[1], x.shape[-1]
    rows = n_tok // GRID_W
    row = jnp.repeat(jnp.arange(rows, dtype=jnp.float32), GRID_W)
    col = jnp.tile(jnp.arange(GRID_W, dtype=jnp.float32), rows)
    n_freq = d // 4
    freq = ROPE_BASE ** (-jnp.arange(n_freq, dtype=jnp.float32) / n_freq)
    ang = jnp.concatenate([row[:, None] * freq, col[:, None] * freq], axis=-1)[None, :, None, :]
    cos, sin = jnp.cos(ang), jnp.sin(ang)
    x1, x2 = x[..., : d // 2], x[..., d // 2:]
    return jnp.concatenate([x1 * cos - x2 * sin, x2 * cos + x1 * sin], axis=-1)


def retention_scan(q, k, v, log_gamma, s0):
    B, T, H, d = q.shape
    n = T // CHUNK
    to_chunks = lambda t: t.reshape(B, n, CHUNK, H, d).transpose(1, 0, 3, 2, 4)
    idx = jnp.arange(CHUNK, dtype=jnp.float32)
    rel = idx[:, None] - idx[None, :]
    decay_in = jnp.where(rel >= 0, jnp.exp(jnp.maximum(rel, 0.0) * log_gamma[:, None, None]), 0.0)
    q_dec = jnp.exp((idx + 1.0) * log_gamma[:, None])[..., None]
    k_dec = jnp.exp((CHUNK - 1.0 - idx) * log_gamma[:, None])[..., None]
    chunk_dec = jnp.exp(CHUNK * log_gamma)[:, None, None]

    def step(S, inp):
        qc, kc, vc = inp
        scores = jnp.einsum('bhid,bhjd->bhij', qc, kc) * decay_in
        out = jnp.einsum('bhij,bhjd->bhid', scores, vc) + jnp.einsum('bhid,bhde->bhie', qc * q_dec, S)
        S = S * chunk_dec + jnp.einsum('bhjd,bhje->bhde', kc * k_dec, vc)
        return S, out

    S_fin, ys = lax.scan(step, s0, (to_chunks(q), to_chunks(k), to_chunks(v)))
    return ys.transpose(1, 0, 3, 2, 4).reshape(B, T, H, d), S_fin


def retention_group(u, uc, need_ctx):
    def split_heads(t):
        q, k, v, gf, gb = split_cols(t.astype(jnp.float32), (RET_WIDTH,) * 5)
        hd = lambda a: a.reshape(a.shape[0], a.shape[1], RET_HEADS, RET_DIM)
        return hd(q), hd(k) * RET_DIM ** -0.5, hd(v), gf, gb

    q, k, v, gf, gb = split_heads(u)
    qc, kc, vc, gfc, gbc = split_heads(uc)
    q, k = axial_rope(q), axial_rope(k)
    log_gamma = jnp.log1p(-jnp.exp2(-5.0 - jnp.arange(RET_HEADS, dtype=jnp.float32)))
    s0 = jnp.zeros((u.shape[0], RET_HEADS, RET_DIM, RET_DIM), jnp.float32)
    yc_f, sc_f = retention_scan(qc, kc, vc, log_gamma, s0)
    yc_b, sc_b = retention_scan(flip(qc), flip(kc), flip(vc), log_gamma, s0)
    y_f, _ = retention_scan(q, k, v, log_gamma, sc_f)
    y_b, _ = retention_scan(flip(q), flip(k), flip(v), log_gamma, sc_b)
    lat = jax.nn.silu(gf) * head_rms(y_f) + jax.nn.silu(gb) * head_rms(flip(y_b))
    if not need_ctx:
        return lat, None
    ctx_out = jax.nn.silu(gfc) * head_rms(yc_f) + jax.nn.silu(gbc) * head_rms(flip(yc_b))
    return lat, ctx_out


def rwkv_prepare(u, w0, w2, a0, a2, g2, k_k, k_a, r_k):
    u = u.astype(jnp.float32)
    B, T, _ = u.shape
    r, k, v, wl_f, wl_b, al, gl_f, gl_b = split_cols(u, RWKV_SPLITS)
    heads = lambda t: t.reshape(B, T, RWKV_HEADS, RWKV_DIM)

    def decay(wl, w0_d, w2_d):
        w_log = -jax.nn.softplus(-(w0_d + jnp.tanh(wl) @ w2_d)) - 0.5
        return heads(jnp.exp(-jnp.exp(w_log)))

    iclr = jax.nn.sigmoid(a0 + al @ a2)
    kk = heads(k * k_k)
    kk = kk / jnp.maximum(jnp.linalg.norm(kk, axis=-1, keepdims=True), 1e-12)
    k_mod = heads(k * (1.0 + (iclr - 1.0) * k_a))
    r, v = heads(r), heads(v)
    bonus = (jnp.sum(r * k_mod * r_k.reshape(RWKV_HEADS, RWKV_DIM), axis=-1, keepdims=True) * v).reshape(B, T, -1)
    g_f = jax.nn.sigmoid(gl_f) @ g2[0]
    g_b = jax.nn.sigmoid(gl_b) @ g2[1]
    return (r, decay(wl_f, w0[0], w2[0]), decay(wl_b, w0[1], w2[1]), k_mod, v, -kk, kk * heads(iclr), g_f, g_b, bonus)


def rwkv7_scan(r, w, k, v, a, b, s0):
    def step(S, inp):
        r_t, w_t, k_t, v_t, a_t, b_t = inp
        sa = jnp.einsum('bhvk,bhk->bhv', S, a_t)
        S = S * w_t[:, :, None, :] + sa[..., None] * b_t[:, :, None, :] + v_t[..., None] * k_t[:, :, None, :]
        return S, jnp.einsum('bhvk,bhk->bhv', S, r_t)

    xs = tuple(jnp.moveaxis(t, 1, 0) for t in (r, w, k, v, a, b))
    S_fin, ys = lax.scan(step, s0, xs)
    return jnp.moveaxis(ys, 0, 1), S_fin


def rwkv_group_norm(y, w, b):
    mu = jnp.mean(y, axis=-1, keepdims=True)
    var = jnp.mean((y - mu) ** 2, axis=-1, keepdims=True)
    yn = ((y - mu) * lax.rsqrt(var + RWKV_GN_EPS)).reshape(y.shape[0], y.shape[1], -1)
    return yn * w + b


def rwkv_group(rw, rwc, w0, w2, a0, a2, g2, k_k, k_a, r_k, lnx_w, lnx_b, need_ctx):
    r, w_f, w_b, k, v, a, b, g_f, g_b, bonus = rwkv_prepare(rw, w0, w2, a0, a2, g2, k_k, k_a, r_k)
    rc, wc_f, wc_b, kc, vc, ac, bc, gc_f, gc_b, bonus_c = rwkv_prepare(rwc, w0, w2, a0, a2, g2, k_k, k_a, r_k)
    s0 = jnp.zeros((rw.shape[0], RWKV_HEADS, RWKV_DIM, RWKV_DIM), jnp.float32)
    yc_f, sc_f = rwkv7_scan(rc, wc_f, kc, vc, ac, bc, s0)
    yc_b, sc_b = rwkv7_scan(*[flip(t) for t in (rc, wc_b, kc, vc, ac, bc)], s0)
    y_f, _ = rwkv7_scan(r, w_f, k, v, a, b, sc_f)
    y_b, _ = rwkv7_scan(*[flip(t) for t in (r, w_b, k, v, a, b)], sc_b)
    lat = g_f * (rwkv_group_norm(y_f, lnx_w, lnx_b) + bonus) + g_b * (rwkv_group_norm(flip(y_b), lnx_w, lnx_b) + bonus)
    if not need_ctx:
        return lat, None
    ctx_out = (gc_f * (rwkv_group_norm(yc_f, lnx_w, lnx_b) + bonus_c)
               + gc_b * (rwkv_group_norm(flip(yc_b), lnx_w, lnx_b) + bonus_c))
    return lat, ctx_out


def token_mix(h, hc, w_in, conv_w, w0, w2, a0, a2, g2, k_k, k_a, r_k, lnx_w, lnx_b, w_out, need_ctx):
    u, uc = h @ w_in, hc @ w_in
    ret_lat, ret_ctx = retention_group(u[..., :RET_COLS], uc[..., :RET_COLS], need_ctx)
    rw = depthwise_conv(u[..., RET_COLS:], conv_w)
    rwc = depthwise_conv(uc[..., RET_COLS:], conv_w)
    rwkv_lat, rwkv_ctx = rwkv_group(rw, rwc, w0, w2, a0, a2, g2, k_k, k_a, r_k, lnx_w, lnx_b, need_ctx)
    lat = jnp.concatenate([ret_lat, rwkv_lat], axis=-1).astype(h.dtype) @ w_out
    if not need_ctx:
        return lat, None
    ctx_out = jnp.concatenate([ret_ctx, rwkv_ctx], axis=-1).astype(hc.dtype) @ w_out
    return lat, ctx_out


def ec_moe(h, w_router, w_gate, w_up, w_down):
    B, T, D = h.shape
    cap = max(1, EC_CAPACITY * T // N_EXPERTS)
    affinity = jax.nn.softmax((h @ w_router).astype(jnp.float32), axis=-1)
    gate_vals, tok_idx = lax.top_k(jnp.swapaxes(affinity, 1, 2), cap)
    xs = jax.vmap(lambda hb, ib: hb[ib])(h, tok_idx)
    hid = jax.nn.silu(jnp.einsum('becd,edf->becf', xs, w_gate)) * jnp.einsum('becd,edf->becf', xs, w_up)
    ye = jnp.einsum('becf,efd->becd', hid, w_down) * gate_vals[..., None].astype(h.dtype)
    return jax.vmap(lambda yb, ib: jnp.zeros((T, D), yb.dtype).at[ib.reshape(-1)].add(yb.reshape(-1, D)))(ye, tok_idx)


def setup_inputs(seed: int = 0) -> dict:
    key = jax.random.key(seed)
    ks = jax.random.split(key, 24)
    nrm = lambda k, shape, s: s * jax.random.normal(k, shape, jnp.float32)
    L, D, W, E, F = DEPTH, D_MODEL, RWKV_WIDTH, N_EXPERTS, D_EXPERT
    return {
        'x': nrm(ks[0], (BATCH, SEQ, D), 1.0),
        'c': nrm(ks[1], (BATCH, D), 1.0),
        'ctx': nrm(ks[2], (BATCH, CTX_LEN, D), 1.0),
        'c_ctx': nrm(ks[3], (D,), 1.0),
        'w_mod': nrm(ks[4], (L, D, 6 * D), 0.5 * D ** -0.5),
        'b_mod': nrm(ks[5], (L, 6 * D), 0.02),
        'norm_gains': 1.0 + nrm(ks[6], (L, 4, D), 0.1),
        'w_in': nrm(ks[7], (L, D, D_IN), D ** -0.5),
        'rwkv_conv': jnp.array([0.3, 1.0, 0.3], jnp.float32)[None, :, None] + nrm(ks[8], (L, CONV_W, RWKV_COLS), 0.05),
        'rwkv_w0': jnp.linspace(-6.0, -1.0, W, dtype=jnp.float32) + nrm(ks[9], (L, 2, W), 0.1),
        'rwkv_w2': nrm(ks[10], (L, 2, DECAY_LORA, W), 0.5 * DECAY_LORA ** -0.5),
        'rwkv_a0': nrm(ks[11], (L, W), 0.1),
        'rwkv_a2': nrm(ks[12], (L, ICLR_LORA, W), 0.5 * ICLR_LORA ** -0.5),
        'rwkv_g2': nrm(ks[13], (L, 2, GATE_LORA, W), GATE_LORA ** -0.5),
        'rwkv_k_k': 0.85 + nrm(ks[14], (L, W), 0.05),
        'rwkv_k_a': 1.0 + nrm(ks[15], (L, W), 0.05),
        'rwkv_r_k': nrm(ks[16], (L, W), 0.1),
        'rwkv_lnx_w': 1.0 + nrm(ks[17], (L, W), 0.1),
        'rwkv_lnx_b': nrm(ks[18], (L, W), 0.02),
        'w_out': nrm(ks[19], (L, D_MIX, D), D_MIX ** -0.5),
        'w_router': nrm(ks[20], (L, D, E), D ** -0.5),
        'w_gate': nrm(ks[21], (L, E, D, F), D ** -0.5),
        'w_up': nrm(ks[22], (L, E, D, F), D ** -0.5),
        'w_down': nrm(ks[23], (L, E, F, D), F ** -0.5),
    }


def reference(x, c, ctx, c_ctx, w_mod, b_mod, norm_gains, w_in, rwkv_conv, rwkv_w0, rwkv_w2, rwkv_a0, rwkv_a2,
              rwkv_g2, rwkv_k_k, rwkv_k_a, rwkv_r_k, rwkv_lnx_w, rwkv_lnx_b, w_out, w_router, w_gate, w_up, w_down):
    for i in range(DEPTH):
        need_ctx = i < DEPTH - 1
        sh1, sc1, gt1, sh2, sc2, gt2 = jnp.split((jax.nn.silu(c) @ w_mod[i] + b_mod[i])[:, None, :], 6, axis=-1)
        csh1, csc1, cgt1, csh2, csc2, cgt2 = jnp.split(jax.nn.silu(c_ctx) @ w_mod[i] + b_mod[i], 6, axis=-1)
        g_pre_mix, g_post_mix, g_pre_ffn, g_post_ffn = norm_gains[i]
        h = rms_norm(x, g_pre_mix) * (1.0 + sc1) + sh1
        hc = rms_norm(ctx, g_pre_mix) * (1.0 + csc1) + csh1
        mix, mix_c = token_mix(h, hc, w_in[i], rwkv_conv[i], rwkv_w0[i], rwkv_w2[i], rwkv_a0[i], rwkv_a2[i],
                               rwkv_g2[i], rwkv_k_k[i], rwkv_k_a[i], rwkv_r_k[i], rwkv_lnx_w[i], rwkv_lnx_b[i],
                               w_out[i], need_ctx)
        x = x + gt1 * rms_norm(mix, g_post_mix)
        h2 = rms_norm(x, g_pre_ffn) * (1.0 + sc2) + sh2
        x = x + gt2 * rms_norm(ec_moe(h2, w_router[i], w_gate[i], w_up[i], w_down[i]), g_post_ffn)
        if need_ctx:
            ctx = ctx + cgt1 * rms_norm(mix_c, g_post_mix)
            hc2 = rms_norm(ctx, g_pre_ffn) * (1.0 + csc2) + csh2
            ctx = ctx + cgt2 * rms_norm(ec_moe(hc2, w_router[i], w_gate[i], w_up[i], w_down[i]), g_post_ffn)
    return x
```

```python
import functools

import numpy as np
import jax
import jax.numpy as jnp
from jax import lax
from jax.experimental import pallas as pl
from jax.experimental.pallas import tpu as pltpu

F32 = jnp.float32
BF16 = jnp.bfloat16
HIGHEST = lax.Precision.HIGHEST

D_MODEL = 1024
CTX_LEN = 256
GRID_W = 64
HEAD_DIM = 64
GROUP_WIDTH = 512
LANES = 128
N_PAIRS = GROUP_WIDTH // LANES
RET_COLS = 5 * GROUP_WIDTH
RWKV_COLS_PAD = 4 * GROUP_WIDTH
DECAY_LORA = 64
ICLR_LORA = 64
GATE_LORA = 128
N_EXPERTS = 16
EC_CAPACITY = 2
ROPE_BASE = 10000.0
NORM_EPS = 1e-6
RWKV_GN_EPS = 64e-5

TOK_TILE = 256
RET_CHUNK = 256
RWKV_CHUNK = 64
OUT_TILE = 512
VMEM_LIMIT = 56 * 1024 * 1024


def _cparams(sem):
    return pltpu.CompilerParams(dimension_semantics=sem, vmem_limit_bytes=VMEM_LIMIT)


def _dot(a, b):
    return jnp.dot(a.astype(BF16), b.astype(BF16), preferred_element_type=F32)


def _dot_nt(a, b):
    return lax.dot_general(a.astype(BF16), b.astype(BF16), (((1,), (1,)), ((), ())),
                           preferred_element_type=F32)


def _dot_exact(a, b):
    return jnp.dot(a, b, precision=HIGHEST, preferred_element_type=F32)


def _head_block_mask(n):
    r = lax.broadcasted_iota(jnp.int32, (n, n), 0)
    c = lax.broadcasted_iota(jnp.int32, (n, n), 1)
    return (r // HEAD_DIM) == (c // HEAD_DIM)


def _head_sum(x):
    ones = jnp.where(_head_block_mask(LANES), 1.0, 0.0).astype(F32)
    parts = [_dot_exact(x[:, g * LANES:(g + 1) * LANES], ones) for g in range(x.shape[1] // LANES)]
    return parts[0] if len(parts) == 1 else jnp.concatenate(parts, axis=1)


def _silu(x):
    return x * jax.nn.sigmoid(x)


def _mod_kernel(c_ref, w_ref, b_ref, o_ref):
    o_ref[...] = _dot_exact(_silu(c_ref[...]), w_ref[...]) + b_ref[...]


def _modulation(cc, w_mod, b_mod):
    rows, d = cc.shape
    n = w_mod.shape[1]
    tn = 1536
    return pl.pallas_call(
        _mod_kernel,
        name="modulation",
        grid=(n // tn,),
        in_specs=[pl.BlockSpec((rows, d), lambda i: (0, 0)),
                  pl.BlockSpec((d, tn), lambda i: (0, i)),
                  pl.BlockSpec((1, tn), lambda i: (0, i))],
        out_specs=pl.BlockSpec((rows, tn), lambda i: (0, i)),
        out_shape=jax.ShapeDtypeStruct((rows, n), F32),
        compiler_params=_cparams(("arbitrary",)),
    )(cc, w_mod, b_mod)


def _proj_kernel(x_ref, ctx_ref, mod_ref, gain_ref, w_ref, cos_ref, sin_ref, ret_ref, rw_ref):
    is_lat = pl.program_id(1) > 0
    xin = jnp.where(is_lat, x_ref[0], ctx_ref[0])
    ms = jnp.mean(xin * xin, axis=-1, keepdims=True)
    y = xin * lax.rsqrt(ms + NORM_EPS) * gain_ref[...]
    m = mod_ref[0]
    h = y * (1.0 + m[:, D_MODEL:2 * D_MODEL]) + m[:, 0:D_MODEL]
    u = _dot(h, w_ref[...])
    lat = is_lat.astype(F32)
    cos = cos_ref[...] * lat + (1.0 - lat)
    sin = sin_ref[...] * lat
    lane = lax.broadcasted_iota(jnp.int32, (1, LANES), 1)
    first_half = (lane % HEAD_DIM) < (HEAD_DIM // 2)
    for base, scale in ((0, 1.0), (GROUP_WIDTH, HEAD_DIM ** -0.5)):
        for g in range(N_PAIRS):
            lo = base + g * LANES
            t = u[:, lo:lo + LANES] * scale
            sw = jnp.where(first_half, pltpu.roll(t, LANES - HEAD_DIM // 2, 1), pltpu.roll(t, HEAD_DIM // 2, 1))
            ret_ref[0, :, lo:lo + LANES] = t * cos + sw * sin
    ret_ref[0, :, 2 * GROUP_WIDTH:] = u[:, 2 * GROUP_WIDTH:RET_COLS]
    rw_ref[0] = u[:, RET_COLS:]


def _projection(x, ctx, mods, gain, w_pad, cos_t, sin_t):
    B, T, D = x.shape
    nt = (T + CTX_LEN) // TOK_TILE
    S = T + CTX_LEN
    n_cols = w_pad.shape[1]
    return pl.pallas_call(
        _proj_kernel,
        name="projection",
        grid=(B, nt),
        in_specs=[
            pl.BlockSpec((1, TOK_TILE, D), lambda b, j: (b, jnp.maximum(j - 1, 0), 0)),
            pl.BlockSpec((1, TOK_TILE, D), lambda b, j: (b, 0, 0)),
            pl.BlockSpec((1, 1, mods.shape[2]), lambda b, j: (jnp.where(j == 0, B, b), 0, 0)),
            pl.BlockSpec((1, D), lambda b, j: (0, 0)),
            pl.BlockSpec((D, n_cols), lambda b, j: (0, 0)),
            pl.BlockSpec((TOK_TILE, LANES), lambda b, j: (jnp.maximum(j - 1, 0), 0)),
            pl.BlockSpec((TOK_TILE, LANES), lambda b, j: (jnp.maximum(j - 1, 0), 0)),
        ],
        out_specs=[pl.BlockSpec((1, TOK_TILE, RET_COLS), lambda b, j: (b, j, 0)),
                   pl.BlockSpec((1, TOK_TILE, RWKV_COLS_PAD), lambda b, j: (b, j, 0))],
        out_shape=[jax.ShapeDtypeStruct((B, S, RET_COLS), F32),
                   jax.ShapeDtypeStruct((B, S, RWKV_COLS_PAD), F32)],
        compiler_params=_cparams(("arbitrary", "arbitrary")),
    )(x, ctx, mods, gain, w_pad, cos_t, sin_t)


def _ret_direction(d, q_ref, k_ref, v_ref, g_ref, o_ref, s_ref, dmask_ref, qdec_ref, kdec_ref, cdec_ref):
    lane = lax.broadcasted_iota(jnp.int32, (1, LANES), 1)
    head0 = lane < HEAD_DIM
    block_diag = _head_block_mask(LANES)
    mean_mat = jnp.where(block_diag, 1.0 / HEAD_DIM, 0.0).astype(F32)
    for p in range(N_PAIRS):
        sl = slice(p * LANES, (p + 1) * LANES)
        q2, k2, v2 = q_ref[0, :, sl], k_ref[0, :, sl], v_ref[0, :, sl]
        state = s_ref[p]
        outs = []
        for hh in range(2):
            qh = jnp.where(head0 if hh == 0 else jnp.logical_not(head0), q2, 0.0)
            scores = _dot_nt(qh, k2) * dmask_ref[d, 2 * p + hh]
            outs.append(_dot(scores, v2))
        y = jnp.where(head0, outs[0], outs[1]) + _dot(q2 * qdec_ref[d, :, sl], state)
        kd = (k2 * kdec_ref[d, :, sl]).T
        s_ref[p] = state * cdec_ref[:, sl] + jnp.where(block_diag, _dot(kd, v2), 0.0)
        ms = _dot_exact(y * y, mean_mat)
        o_ref[0, :, sl] = _silu(g_ref[0, :, sl]) * (y * lax.rsqrt(ms + NORM_EPS))


def _ret_kernel(qf, kf, vf, gf, qb, kb, vb, gb, dmask, qdec, kdec, cdec, of, ob, sf, sb):
    @pl.when(pl.program_id(1) == 0)
    def _():
        sf[...] = jnp.zeros_like(sf)
        sb[...] = jnp.zeros_like(sb)

    _ret_direction(0, qf, kf, vf, gf, of, sf, dmask, qdec, kdec, cdec)
    _ret_direction(1, qb, kb, vb, gb, ob, sb, dmask, qdec, kdec, cdec)


def _retention_tables():
    n_heads = GROUP_WIDTH // HEAD_DIM
    log_gamma = np.log1p(-np.exp2(-5.0 - np.arange(n_heads, dtype=np.float64)))
    idx = np.arange(RET_CHUNK, dtype=np.float64)
    rel = idx[:, None] - idx[None, :]
    dm = np.where(rel >= 0, np.exp(np.maximum(rel, 0.0) * log_gamma[:, None, None]), 0.0)
    dmask = np.stack([dm, dm.transpose(0, 2, 1)])
    pos = np.stack([idx, RET_CHUNK - 1.0 - idx])
    lg = np.repeat(log_gamma, HEAD_DIM)
    qdec = np.exp((pos[:, :, None] + 1.0) * lg)
    kdec = np.exp((RET_CHUNK - 1.0 - pos[:, :, None]) * lg)
    cdec = np.exp(RET_CHUNK * lg)[None, :]
    return [jnp.asarray(a, F32) for a in (dmask, qdec, kdec, cdec)]


def _retention(u_ret, T):
    B = u_ret.shape[0]
    n_lat = T // RET_CHUNK
    ns = n_lat + 1
    dmask, qdec, kdec, cdec = _retention_tables()

    def fchunk(s):
        return s

    def bchunk(s):
        return jnp.where(s == 0, 0, ns - s)

    def col(c, chunk):
        return pl.BlockSpec((1, RET_CHUNK, GROUP_WIDTH), lambda b, s: (b, chunk(s), c))

    const = lambda shape: pl.BlockSpec(shape, lambda b, s: (0,) * len(shape))
    in_specs = ([col(c, fchunk) for c in (0, 1, 2, 3)] + [col(c, bchunk) for c in (0, 1, 2, 4)]
                + [const(dmask.shape), const(qdec.shape), const(kdec.shape), const(cdec.shape)])
    out_specs = [pl.BlockSpec((1, RET_CHUNK, GROUP_WIDTH), lambda b, s: (b, jnp.maximum(s - 1, 0), 0)),
                 pl.BlockSpec((1, RET_CHUNK, GROUP_WIDTH), lambda b, s: (b, jnp.where(s == 0, n_lat - 1, n_lat - s), 0))]
    out_shape = [jax.ShapeDtypeStruct((B, T, GROUP_WIDTH), F32)] * 2
    return pl.pallas_call(
        _ret_kernel,
        name="retention",
        grid=(B, ns),
        in_specs=in_specs,
        out_specs=out_specs,
        out_shape=out_shape,
        scratch_shapes=[pltpu.VMEM((N_PAIRS, LANES, LANES), F32)] * 2,
        compiler_params=_cparams(("arbitrary", "arbitrary")),
    )(*([u_ret] * 8), dmask, qdec, kdec, cdec)


def _prep_kernel(n_tiles, u_ref, up_ref, un_ref, cw_ref, w0_ref, w2_ref, a0_ref, a2_ref, g2f_ref, g2b_ref,
                 kk_ref, ka_ref, rk_ref,
                 r_o, k_o, v_o, a_o, b_o, lwf_o, lwb_o, gf_o, gb_o, bon_o):
    j = pl.program_id(1)
    x = u_ref[0]
    rows = lax.broadcasted_iota(jnp.int32, (TOK_TILE, 1), 0)
    has_prev = (j >= 2).astype(F32)
    has_next = jnp.logical_and(j >= 1, j <= n_tiles - 2).astype(F32)
    prev_row = up_ref[0, 7:8, :] * has_prev
    next_row = un_ref[0, 0:1, :] * has_next
    xm = jnp.where(rows == 0, prev_row, pltpu.roll(x, 1, 0))
    xp = jnp.where(rows == TOK_TILE - 1, next_row, pltpu.roll(x, TOK_TILE - 1, 0))
    rw = cw_ref[0:1, :] * xm + cw_ref[1:2, :] * x + cw_ref[2:3, :] * xp

    W = GROUP_WIDTH
    r, k, v, lo = rw[:, 0:W], rw[:, W:2 * W], rw[:, 2 * W:3 * W], rw[:, 3 * W:4 * W]
    z = w0_ref[...] + _dot(jnp.tanh(lo[:, 0:LANES]), w2_ref[...])
    softplus = jnp.maximum(-z, 0.0) + jnp.log1p(jnp.exp(-jnp.abs(z)))
    log_decay = -jnp.exp(-softplus - 0.5)
    iclr = jax.nn.sigmoid(a0_ref[...] + _dot(lo[:, LANES:2 * LANES], a2_ref[...]))
    g_f = _dot(jax.nn.sigmoid(lo[:, 2 * LANES:3 * LANES]), g2f_ref[...])
    g_b = _dot(jax.nn.sigmoid(lo[:, 3 * LANES:4 * LANES]), g2b_ref[...])
    kk = k * kk_ref[...]
    kk = kk / jnp.maximum(jnp.sqrt(_head_sum(kk * kk)), 1e-12)
    k_mod = k * (1.0 + (iclr - 1.0) * ka_ref[...])
    bonus = _head_sum(r * k_mod * rk_ref[...]) * v
    r_o[0] = r
    k_o[0] = k_mod
    v_o[0] = v
    a_o[0] = -kk
    b_o[0] = kk * iclr
    lwf_o[0] = log_decay[:, 0:W]
    lwb_o[0] = log_decay[:, W:2 * W]
    gf_o[0] = g_f
    gb_o[0] = g_b
    bon_o[0] = (g_f + g_b) * bonus


def _rwkv_prepare(u_rw, T, cw, w0cat, w2bd, a0, a2pad, g2f, g2b, k_k, k_a, r_k):
    B, S, C4 = u_rw.shape
    nt = S // TOK_TILE
    rb = TOK_TILE // 8
    const = lambda a: pl.BlockSpec(a.shape, lambda b, j: (0,) * a.ndim)
    consts = (cw, w0cat, w2bd, a0, a2pad, g2f, g2b, k_k, k_a, r_k)
    in_specs = [pl.BlockSpec((1, TOK_TILE, C4), lambda b, j: (b, j, 0)),
                pl.BlockSpec((1, 8, C4), lambda b, j: (b, jnp.maximum(j * rb - 1, 0), 0)),
                pl.BlockSpec((1, 8, C4), lambda b, j: (b, jnp.minimum((j + 1) * rb, S // 8 - 1), 0))]
    in_specs += [const(a) for a in consts]
    seq_spec = pl.BlockSpec((1, TOK_TILE, GROUP_WIDTH), lambda b, j: (b, j, 0))
    lat_spec = pl.BlockSpec((1, TOK_TILE, GROUP_WIDTH), lambda b, j: (b, jnp.maximum(j - 1, 0), 0))
    seq_shape = jax.ShapeDtypeStruct((B, S, GROUP_WIDTH), F32)
    lat_shape = jax.ShapeDtypeStruct((B, T, GROUP_WIDTH), F32)
    return pl.pallas_call(
        functools.partial(_prep_kernel, nt),
        name="rwkv_prepare",
        grid=(B, nt),
        in_specs=in_specs,
        out_specs=[seq_spec] * 7 + [lat_spec] * 3,
        out_shape=[seq_shape] * 7 + [lat_shape] * 3,
        compiler_params=_cparams(("arbitrary", "arbitrary")),
    )(u_rw, u_rw, u_rw, *consts)


def _dplr_pair(r, k, v, a, b, lw, st, reverse):
    C = RWKV_CHUNK
    ri = lax.broadcasted_iota(jnp.int32, (C, C), 0)
    ci = lax.broadcasted_iota(jnp.int32, (C, C), 1)
    incl = (ci >= ri) if reverse else (ci <= ri)
    strict = (ci > ri) if reverse else (ci < ri)
    ri2 = lax.broadcasted_iota(jnp.int32, (C, 2 * C), 0)
    ci2 = lax.broadcasted_iota(jnp.int32, (C, 2 * C), 1)
    cm2 = ci2 % C
    incl2 = (cm2 >= ri2) if reverse else (cm2 <= ri2)
    strict_ak = jnp.logical_and(ci2 >= C, (cm2 > ri2) if reverse else (cm2 < ri2))
    lc = _dot_exact(jnp.where(incl, 1.0, 0.0).astype(F32), lw)
    ltot = jnp.sum(lw, axis=0, keepdims=True)
    a_t = a * jnp.exp(lc - lw)
    r_t = r * jnp.exp(lc)
    inv = jnp.exp(-lc)
    b_t, k_t = b * inv, k * inv
    to_end = jnp.exp(ltot - lc)
    b_e, k_e = b * to_end, k * to_end
    rhs = jnp.concatenate([b_t, k_t], axis=0)
    lane = lax.broadcasted_iota(jnp.int32, (1, LANES), 1)
    zeros = jnp.zeros((C, LANES), F32)
    eye = jnp.where(ri == ci, 1.0, 0.0).astype(F32)
    u0 = w = y0 = q_add = None
    for hh in range(2):
        lm = (lane // HEAD_DIM) == hh
        a_h, r_h, v_h = jnp.where(lm, a_t, 0.0), jnp.where(lm, r_t, 0.0), jnp.where(lm, v, 0.0)
        g = _dot_nt(jnp.concatenate([a_h, r_h], axis=0), rhs)
        g_top, g_bot = g[0:C], g[C:2 * C]
        low = jnp.where(strict, g_top[:, 0:C], 0.0)
        akv = _dot(jnp.where(strict_ak, g_top, 0.0), jnp.concatenate([zeros, v_h], axis=0))
        tm = eye + low
        lp = low
        n = 1
        while n < C // 2:
            lp = _dot(lp, lp)
            tm = tm + _dot(tm, lp)
            n *= 2
        uw = _dot(tm, jnp.concatenate([akv, a_h], axis=1))
        u0_h, w_h = uw[:, 0:LANES], uw[:, LANES:]
        r2 = jnp.concatenate([uw, jnp.concatenate([v_h, zeros], axis=1)], axis=0)
        yq = _dot(jnp.where(incl2, g_bot, 0.0), r2)
        if hh == 0:
            u0, w, y0, q_add = u0_h, w_h, yq[:, 0:LANES], yq[:, LANES:]
        else:
            u0, w, y0, q_add = u0 + u0_h, w + w_h, y0 + yq[:, 0:LANES], q_add + yq[:, LANES:]
    y = _dot_nt(r_t + q_add, st) + y0
    stack_t = jnp.concatenate([w, u0, v, zeros], axis=0).T
    rm = jnp.concatenate([jnp.concatenate([b_e, zeros], axis=1),
                          jnp.concatenate([zeros, b_e], axis=1),
                          jnp.concatenate([zeros, k_e], axis=1),
                          jnp.concatenate([zeros, zeros], axis=1)], axis=0)
    mn = _dot(stack_t, rm)
    bd = _head_block_mask(LANES)
    st_new = (st * jnp.exp(ltot) + _dot(st, jnp.where(bd, mn[:, 0:LANES], 0.0))
              + jnp.where(bd, mn[:, LANES:], 0.0))
    return y, st_new


def _scan_direction(reverse, r_ref, k_ref, v_ref, a_ref, b_ref, lw_ref, g_ref, lnw_ref, lnb_ref, o_ref, s_ref):
    mean_mat = jnp.where(_head_block_mask(LANES), 1.0 / HEAD_DIM, 0.0).astype(F32)
    for p in range(N_PAIRS):
        sl = slice(p * LANES, (p + 1) * LANES)
        y, st_new = _dplr_pair(r_ref[0, :, sl], k_ref[0, :, sl], v_ref[0, :, sl], a_ref[0, :, sl],
                               b_ref[0, :, sl], lw_ref[0, :, sl], s_ref[p], reverse)
        s_ref[p] = st_new
        dlt = y - _dot_exact(y, mean_mat)
        var = _dot_exact(dlt * dlt, mean_mat)
        yn = dlt * lax.rsqrt(var + RWKV_GN_EPS) * lnw_ref[:, sl] + lnb_ref[:, sl]
        o_ref[0, :, sl] = g_ref[0, :, sl] * yn


def _scan_kernel(rf, kf, vf, af, bf, lwf, gf, rb, kb, vb, ab, bb, lwb, gb, lnw, lnb, of, ob, sf, sb):
    @pl.when(pl.program_id(1) == 0)
    def _():
        sf[...] = jnp.zeros_like(sf)
        sb[...] = jnp.zeros_like(sb)

    _scan_direction(False, rf, kf, vf, af, bf, lwf, gf, lnw, lnb, of, sf)
    _scan_direction(True, rb, kb, vb, ab, bb, lwb, gb, lnw, lnb, ob, sb)


def _rwkv_scan(r, k, v, a, b, lwf, lwb, gf, gb, lnw, lnb, T):
    B, S, W = r.shape
    C = RWKV_CHUNK
    n_ctx = CTX_LEN // C
    n_lat = T // C
    ns = n_ctx + n_lat

    def fchunk(s):
        return s

    def bchunk(s):
        return jnp.where(s < n_ctx, n_ctx - 1 - s, ns + n_ctx - 1 - s)

    def fout(s):
        return jnp.maximum(s - n_ctx, 0)

    def bout(s):
        return jnp.where(s < n_ctx, n_lat - 1, ns - 1 - s)

    def spec(chunk):
        return pl.BlockSpec((1, C, W), lambda bi, s: (bi, chunk(s), 0))

    const = pl.BlockSpec((1, W), lambda bi, s: (0, 0))
    in_specs = [spec(fchunk)] * 6 + [spec(fout)] + [spec(bchunk)] * 6 + [spec(bout)] + [const, const]
    return pl.pallas_call(
        _scan_kernel,
        name="rwkv_scan",
        grid=(B, ns),
        in_specs=in_specs,
        out_specs=[spec(fout), spec(bout)],
        out_shape=[jax.ShapeDtypeStruct((B, T, W), F32)] * 2,
        scratch_shapes=[pltpu.VMEM((N_PAIRS, LANES, LANES), F32)] * 2,
        compiler_params=_cparams(("arbitrary", "arbitrary")),
    )(r, k, v, a, b, lwf, gf, r, k, v, a, b, lwb, gb, lnw, lnb)


def _out_kernel(x_ref, rf_ref, rb_ref, wf_ref, wb_ref, bon_ref, wo_ref, mod_ref, gains_ref, wr_ref,
                x1_ref, h2_ref, aff_ref):
    W = GROUP_WIDTH
    ret = rf_ref[0] + rb_ref[0]
    rwk = wf_ref[0] + wb_ref[0] + bon_ref[0]
    mix = _dot(ret, wo_ref[0:W, :]) + _dot(rwk, wo_ref[W:2 * W, :])
    m = mod_ref[0]
    D = D_MODEL
    ms = jnp.mean(mix * mix, axis=-1, keepdims=True)
    x1 = x_ref[0] + m[:, 2 * D:3 * D] * (mix * lax.rsqrt(ms + NORM_EPS) * gains_ref[1:2, :])
    x1_ref[0] = x1
    ms2 = jnp.mean(x1 * x1, axis=-1, keepdims=True)
    h2 = (x1 * lax.rsqrt(ms2 + NORM_EPS) * gains_ref[2:3, :]) * (1.0 + m[:, 4 * D:5 * D]) + m[:, 3 * D:4 * D]
    h2_ref[0] = h2.astype(BF16)
    logits = _dot_exact(h2, wr_ref[...]).T[0:N_EXPERTS, :]
    e = jnp.exp(logits - jnp.max(logits, axis=0, keepdims=True))
    aff_ref[0] = e / jnp.sum(e, axis=0, keepdims=True)


def _out_projection(x, ret_f, ret_b, rw_f, rw_b, bonus, w_out, mods, gains, wr_pad):
    B, T, D = x.shape
    nt = T // OUT_TILE
    tok = lambda w: pl.BlockSpec((1, OUT_TILE, w), lambda b, i: (b, i, 0))
    const = lambda a: pl.BlockSpec(a.shape, lambda b, i: (0,) * a.ndim)
    return pl.pallas_call(
        _out_kernel,
        name="out_projection",
        grid=(B, nt),
        in_specs=[tok(D)] + [tok(GROUP_WIDTH)] * 5 + [const(w_out),
                  pl.BlockSpec((1, 1, mods.shape[2]), lambda b, i: (b, 0, 0)), const(gains), const(wr_pad)],
        out_specs=[tok(D), tok(D), pl.BlockSpec((1, N_EXPERTS, OUT_TILE), lambda b, i: (b, 0, i))],
        out_shape=[jax.ShapeDtypeStruct((B, T, D), F32), jax.ShapeDtypeStruct((B, T, D), BF16),
                   jax.ShapeDtypeStruct((B, N_EXPERTS, T), F32)],
        compiler_params=_cparams(("arbitrary", "arbitrary")),
    )(x, ret_f, ret_b, rw_f, rw_b, bonus, w_out, mods, gains, wr_pad)


def _cumsum_lanes(x):
    n = x.shape[1]
    lane = lax.broadcasted_iota(jnp.int32, (1, n), 1)
    sh = 1
    while sh < n:
        x = x + jnp.where(lane >= sh, pltpu.roll(x, sh, 1), 0.0)
        sh *= 2
    return x


def _select_kernel(cap, aff_ref, slot_ref, slot_t_ref):
    aff = aff_ref[0]
    capf = jnp.float32(cap)

    def body(i, thr):
        cand = thr | (jnp.int32(1) << (30 - i))
        cnt = jnp.sum(jnp.where(aff >= pltpu.bitcast(cand, F32), 1.0, 0.0), axis=1, keepdims=True)
        return jnp.where(cnt >= capf, cand, thr)

    thr = lax.fori_loop(0, 31, body, jnp.zeros((aff.shape[0], 1), jnp.int32))
    min_normal_bits = jnp.int32(0x00800000)
    above = pltpu.bitcast(jnp.maximum(thr + 1, min_normal_bits), F32)
    gt = aff >= above
    eq = jnp.where(jnp.logical_and(aff >= pltpu.bitcast(thr, F32), jnp.logical_not(gt)), 1.0, 0.0)
    need = capf - jnp.sum(jnp.where(gt, 1.0, 0.0), axis=1, keepdims=True)
    eq_before = _cumsum_lanes(eq) - eq
    sel = jnp.where(jnp.logical_or(gt, jnp.logical_and(eq > 0.0, eq_before < need)), 1.0, 0.0)
    slot = jnp.where(sel > 0.0, _cumsum_lanes(sel) - sel, -1.0)
    slot_ref[0] = slot
    pad = jnp.full((LANES - slot.shape[0], slot.shape[1]), -1.0, F32)
    slot_t_ref[0] = jnp.concatenate([slot, pad], axis=0).T


def _select(aff, cap):
    B, E, T = aff.shape
    return pl.pallas_call(
        functools.partial(_select_kernel, cap),
        name="expert_select",
        grid=(B,),
        in_specs=[pl.BlockSpec((1, E, T), lambda b: (b, 0, 0))],
        out_specs=[pl.BlockSpec((1, E, T), lambda b: (b, 0, 0)), pl.BlockSpec((1, T, LANES), lambda b: (b, 0, 0))],
        out_shape=[jax.ShapeDtypeStruct((B, E, T), F32), jax.ShapeDtypeStruct((B, T, LANES), F32)],
        compiler_params=_cparams(("arbitrary",)),
    )(aff)


def _expert_kernel(cap, h_ref, slot_ref, aff_ref, wg_ref, wu_ref, wd_ref, ye_ref):
    T = h_ref.shape[1]
    kc = min(T, 1024)
    jj = lax.broadcasted_iota(jnp.int32, (cap, 1), 0).astype(F32)
    xs = jnp.zeros((cap, D_MODEL), F32)
    gate = jnp.zeros((cap, 1), F32)
    for c in range(T // kc):
        hit = slot_ref[0, 0, :, c * kc:(c + 1) * kc] == jj
        xs = xs + jnp.dot(jnp.where(hit, 1.0, 0.0).astype(BF16), h_ref[0, c * kc:(c + 1) * kc, :],
                          preferred_element_type=F32)
        gate = gate + jnp.sum(jnp.where(hit, aff_ref[0, 0, :, c * kc:(c + 1) * kc], 0.0), axis=1, keepdims=True)
    xs = xs.astype(BF16)
    hid = _silu(jnp.dot(xs, wg_ref[0], preferred_element_type=F32)) * jnp.dot(xs, wu_ref[0], preferred_element_type=F32)
    ye = jnp.dot(hid.astype(BF16), wd_ref[0], preferred_element_type=F32) * gate
    ye_ref[0, 0] = ye.astype(BF16)


def _expert_ffn(h2, slot, aff, wg, wu, wd, cap):
    B, T, D = h2.shape
    E = wg.shape[0]
    row = pl.BlockSpec((1, 1, 1, T), lambda b, e: (b, e, 0, 0))
    wspec = pl.BlockSpec((1, D, D), lambda b, e: (e, 0, 0))
    return pl.pallas_call(
        functools.partial(_expert_kernel, cap),
        name="expert_ffn",
        grid=(B, E),
        in_specs=[pl.BlockSpec((1, T, D), lambda b, e: (b, 0, 0)), row, row, wspec, wspec, wspec],
        out_specs=pl.BlockSpec((1, 1, cap, D), lambda b, e: (b, e, 0, 0)),
        out_shape=jax.ShapeDtypeStruct((B, E, cap, D), BF16),
        compiler_params=_cparams(("arbitrary", "arbitrary")),
    )(h2, slot.reshape(B, E, 1, T), aff.reshape(B, E, 1, T), wg, wu, wd)


def _combine_kernel(cap, x1_ref, slot_t_ref, ye_ref, mod_ref, gain_ref, o_ref):
    jj = lax.broadcasted_iota(jnp.int32, (1, cap), 1).astype(F32)
    st = slot_t_ref[0]
    acc = jnp.zeros((OUT_TILE, D_MODEL), F32)
    for e in range(N_EXPERTS):
        hit = st[:, e:e + 1] == jj
        acc = acc + jnp.dot(jnp.where(hit, 1.0, 0.0).astype(BF16), ye_ref[0, e], preferred_element_type=F32)
    ms = jnp.mean(acc * acc, axis=-1, keepdims=True)
    gt2 = mod_ref[0][:, 5 * D_MODEL:6 * D_MODEL]
    o_ref[0] = x1_ref[0] + gt2 * (acc * lax.rsqrt(ms + NORM_EPS) * gain_ref[...])


def _combine(x1, slot_t, ye, mods, gain, cap):
    B, T, D = x1.shape
    E = ye.shape[1]
    return pl.pallas_call(
        functools.partial(_combine_kernel, cap),
        name="combine",
        grid=(B, T // OUT_TILE),
        in_specs=[pl.BlockSpec((1, OUT_TILE, D), lambda b, i: (b, i, 0)),
                  pl.BlockSpec((1, OUT_TILE, LANES), lambda b, i: (b, i, 0)),
                  pl.BlockSpec((1, E, cap, D), lambda b, i: (b, 0, 0, 0)),
                  pl.BlockSpec((1, 1, mods.shape[2]), lambda b, i: (b, 0, 0)),
                  pl.BlockSpec((1, D), lambda b, i: (0, 0))],
        out_specs=pl.BlockSpec((1, OUT_TILE, D), lambda b, i: (b, i, 0)),
        out_shape=jax.ShapeDtypeStruct((B, T, D), F32),
        compiler_params=_cparams(("arbitrary", "arbitrary")),
    )(x1, slot_t, ye, mods, gain)


def _rope_tables(T):
    rows = T // GRID_W
    row = np.repeat(np.arange(rows, dtype=np.float32), GRID_W)
    col = np.tile(np.arange(GRID_W, dtype=np.float32), rows)
    n_freq = HEAD_DIM // 4
    freq = jnp.asarray(ROPE_BASE, F32) ** (-jnp.arange(n_freq, dtype=F32) / n_freq)
    ang = jnp.concatenate([row[:, None] * freq, col[:, None] * freq], axis=-1)
    cos, sin = jnp.cos(ang), jnp.sin(ang)
    reps = LANES // (HEAD_DIM // 2)
    cos_t = jnp.tile(cos, (1, reps))
    sin_t = jnp.tile(jnp.concatenate([-sin, sin], axis=-1), (1, reps // 2))
    return cos_t, sin_t


def kernel(x, c, ctx, c_ctx, w_mod, b_mod, norm_gains, w_in, rwkv_conv, rwkv_w0, rwkv_w2, rwkv_a0, rwkv_a2, rwkv_g2, rwkv_k_k, rwkv_k_a, rwkv_r_k, rwkv_lnx_w, rwkv_lnx_b, w_out, w_router, w_gate, w_up, w_down):
    B, T, D = x.shape
    assert D == D_MODEL and ctx.shape == (B, CTX_LEN, D) and T % OUT_TILE == 0 and w_mod.shape[0] == 1
    W = GROUP_WIDTH
    cap = max(1, EC_CAPACITY * T // N_EXPERTS)

    n_rows = -(-(B + 1) // 8) * 8
    cc = jnp.concatenate([c, c_ctx[None, :], jnp.zeros((n_rows - B - 1, D), F32)], axis=0)
    mods = _modulation(cc, w_mod[0], b_mod).reshape(n_rows, 1, 6 * D)

    split = RET_COLS + 3 * W + 2 * DECAY_LORA + ICLR_LORA
    w_pad = jnp.concatenate([w_in[0][:, :split], jnp.zeros((D, LANES - ICLR_LORA), F32), w_in[0][:, split:]],
                            axis=1).astype(BF16)
    cs = split - RET_COLS
    cw = jnp.concatenate([rwkv_conv[0][:, :cs], jnp.zeros((3, LANES - ICLR_LORA), F32), rwkv_conv[0][:, cs:]], axis=1)
    zl = jnp.zeros((DECAY_LORA, W), F32)
    w2bd = jnp.concatenate([jnp.concatenate([rwkv_w2[0, 0], zl], axis=1),
                            jnp.concatenate([zl, rwkv_w2[0, 1]], axis=1)], axis=0).astype(BF16)
    w0cat = jnp.concatenate([rwkv_w0[0, 0], rwkv_w0[0, 1]])[None, :]
    a2pad = jnp.concatenate([rwkv_a2[0], jnp.zeros((LANES - ICLR_LORA, W), F32)], axis=0).astype(BF16)
    g2f, g2b = rwkv_g2[0, 0].astype(BF16), rwkv_g2[0, 1].astype(BF16)
    row = lambda a: a[0][None, :]
    cos_t, sin_t = _rope_tables(T)

    u_ret, u_rw = _projection(x, ctx, mods, norm_gains[0, 0][None, :], w_pad, cos_t, sin_t)
    ret_f, ret_b = _retention(u_ret, T)
    r, k, v, a, b, lwf, lwb, gf, gb, bonus = _rwkv_prepare(
        u_rw, T, cw, w0cat, w2bd, row(rwkv_a0), a2pad, g2f, g2b, row(rwkv_k_k), row(rwkv_k_a), row(rwkv_r_k))
    rw_f, rw_b = _rwkv_scan(r, k, v, a, b, lwf, lwb, gf, gb, row(rwkv_lnx_w), row(rwkv_lnx_b), T)

    wr_pad = jnp.concatenate([w_router[0], jnp.zeros((D, LANES - N_EXPERTS), F32)], axis=1)
    x1, h2, aff = _out_projection(x, ret_f, ret_b, rw_f, rw_b, bonus, w_out[0].astype(BF16), mods,
                                  norm_gains[0], wr_pad)
    slot, slot_t = _select(aff, cap)
    ye = _expert_ffn(h2, slot, aff, w_gate[0].astype(BF16), w_up[0].astype(BF16), w_down[0].astype(BF16), cap)
    return _combine(x1, slot_t, ye, mods, norm_gains[0, 3][None, :], cap)
```

```python
import functools

import numpy as np
import jax
import jax.numpy as jnp
from jax import lax
from jax.experimental import pallas as pl
from jax.experimental.pallas import tpu as pltpu

F32 = jnp.float32
BF16 = jnp.bfloat16
HIGHEST = lax.Precision.HIGHEST

D_MODEL = 1024
CTX_LEN = 256
GRID_W = 64
HEAD_DIM = 64
GROUP_WIDTH = 512
LANES = 128
N_PAIRS = GROUP_WIDTH // LANES
HEADS_PER_GROUP = 4
GROUP_LANES = HEADS_PER_GROUP * HEAD_DIM
N_GROUPS = GROUP_WIDTH // GROUP_LANES
RET_COLS = 5 * GROUP_WIDTH
RWKV_COLS_PAD = 4 * GROUP_WIDTH
DECAY_LORA = 64
ICLR_LORA = 64
GATE_LORA = 128
N_EXPERTS = 16
EC_CAPACITY = 2
ROPE_BASE = 10000.0
NORM_EPS = 1e-6
RWKV_GN_EPS = 64e-5

TOK_TILE = 256
RET_CHUNK = 256
RWKV_CHUNK = 64
RWKV_CHUNKS_PER_STEP = 2
OUT_TILE = 512
VMEM_LIMIT = 56 * 1024 * 1024


def _cparams(sem):
    return pltpu.CompilerParams(dimension_semantics=sem, vmem_limit_bytes=VMEM_LIMIT)


def _dot(a, b):
    return jnp.dot(a.astype(BF16), b.astype(BF16), preferred_element_type=F32)


def _dot_nt(a, b):
    return lax.dot_general(a.astype(BF16), b.astype(BF16), (((1,), (1,)), ((), ())),
                           preferred_element_type=F32)


def _dot_exact(a, b):
    return jnp.dot(a, b, precision=HIGHEST, preferred_element_type=F32)


def _head_block_mask(n):
    r = lax.broadcasted_iota(jnp.int32, (n, n), 0)
    c = lax.broadcasted_iota(jnp.int32, (n, n), 1)
    return (r // HEAD_DIM) == (c // HEAD_DIM)


def _split_bf16(x, terms):
    out = []
    for _ in range(terms - 1):
        hi = x.astype(BF16)
        out.append(hi)
        x = x - hi.astype(F32)
    out.append(x.astype(BF16))
    return out


def _head_sum(x, scale=1.0):
    cols = []
    for g in range(x.shape[1] // GROUP_LANES):
        m = jnp.where(_head_block_mask(GROUP_LANES), scale, 0.0).astype(BF16)
        parts = _split_bf16(x[:, g * GROUP_LANES:(g + 1) * GROUP_LANES], 2)
        cols.append(jnp.dot(jnp.concatenate(parts, axis=1), jnp.concatenate([m, m], axis=0),
                            preferred_element_type=F32))
    return cols[0] if len(cols) == 1 else jnp.concatenate(cols, axis=1)


def _silu(x):
    return x * jax.nn.sigmoid(x)


def _mod_kernel(c_ref, w_ref, b_ref, o_ref):
    o_ref[...] = _dot_exact(_silu(c_ref[...]), w_ref[...]) + b_ref[...]


def _modulation(cc, w_mod, b_mod):
    rows, d = cc.shape
    n = w_mod.shape[1]
    tn = 1536
    return pl.pallas_call(
        _mod_kernel,
        name="modulation",
        grid=(n // tn,),
        in_specs=[pl.BlockSpec((rows, d), lambda i: (0, 0)),
                  pl.BlockSpec((d, tn), lambda i: (0, i)),
                  pl.BlockSpec((1, tn), lambda i: (0, i))],
        out_specs=pl.BlockSpec((rows, tn), lambda i: (0, i)),
        out_shape=jax.ShapeDtypeStruct((rows, n), F32),
        compiler_params=_cparams(("arbitrary",)),
    )(cc, w_mod, b_mod)


def _proj_kernel(x_ref, ctx_ref, mod_ref, gain_ref, w_ref, cos_ref, sin_ref, ret_ref, rw_ref):
    is_lat = pl.program_id(1) > 0
    xin = jnp.where(is_lat, x_ref[0], ctx_ref[0])
    ms = jnp.mean(xin * xin, axis=-1, keepdims=True)
    y = xin * lax.rsqrt(ms + NORM_EPS) * gain_ref[...]
    m = mod_ref[0]
    h = y * (1.0 + m[:, D_MODEL:2 * D_MODEL]) + m[:, 0:D_MODEL]
    u = _dot(h, w_ref[...])
    lat = is_lat.astype(F32)
    cos = cos_ref[...] * lat + (1.0 - lat)
    sin = sin_ref[...] * lat
    lane = lax.broadcasted_iota(jnp.int32, (1, LANES), 1)
    first_half = (lane % HEAD_DIM) < (HEAD_DIM // 2)
    for base, scale in ((0, 1.0), (GROUP_WIDTH, HEAD_DIM ** -0.5)):
        for g in range(N_PAIRS):
            lo = base + g * LANES
            t = u[:, lo:lo + LANES] * scale
            sw = jnp.where(first_half, pltpu.roll(t, LANES - HEAD_DIM // 2, 1), pltpu.roll(t, HEAD_DIM // 2, 1))
            ret_ref[0, :, lo:lo + LANES] = t * cos + sw * sin
    ret_ref[0, :, 2 * GROUP_WIDTH:] = u[:, 2 * GROUP_WIDTH:RET_COLS]
    rw_ref[0] = u[:, RET_COLS:]


def _projection(x, ctx, mods, gain, w_pad, cos_t, sin_t):
    B, T, D = x.shape
    nt = (T + CTX_LEN) // TOK_TILE
    S = T + CTX_LEN
    n_cols = w_pad.shape[1]
    return pl.pallas_call(
        _proj_kernel,
        name="projection",
        grid=(B, nt),
        in_specs=[
            pl.BlockSpec((1, TOK_TILE, D), lambda b, j: (b, jnp.maximum(j - 1, 0), 0)),
            pl.BlockSpec((1, TOK_TILE, D), lambda b, j: (b, 0, 0)),
            pl.BlockSpec((1, 1, mods.shape[2]), lambda b, j: (jnp.where(j == 0, B, b), 0, 0)),
            pl.BlockSpec((1, D), lambda b, j: (0, 0)),
            pl.BlockSpec((D, n_cols), lambda b, j: (0, 0)),
            pl.BlockSpec((TOK_TILE, LANES), lambda b, j: (jnp.maximum(j - 1, 0), 0)),
            pl.BlockSpec((TOK_TILE, LANES), lambda b, j: (jnp.maximum(j - 1, 0), 0)),
        ],
        out_specs=[pl.BlockSpec((1, TOK_TILE, RET_COLS), lambda b, j: (b, j, 0)),
                   pl.BlockSpec((1, TOK_TILE, RWKV_COLS_PAD), lambda b, j: (b, j, 0))],
        out_shape=[jax.ShapeDtypeStruct((B, S, RET_COLS), F32),
                   jax.ShapeDtypeStruct((B, S, RWKV_COLS_PAD), F32)],
        compiler_params=_cparams(("arbitrary", "arbitrary")),
    )(x, ctx, mods, gain, w_pad, cos_t, sin_t)


def _ret_direction(d, q_ref, k_ref, v_ref, g_ref, o_ref, s_ref, dmask_ref, qdec_ref, kdec_ref, cdec_ref):
    C = RET_CHUNK
    lane_head = lax.broadcasted_iota(jnp.int32, (1, GROUP_LANES), 1) // HEAD_DIM
    block_diag = _head_block_mask(GROUP_LANES)
    for gi in range(N_GROUPS):
        sl = slice(gi * GROUP_LANES, (gi + 1) * GROUP_LANES)
        q4, k4, v4 = q_ref[0, :, sl], k_ref[0, :, sl], v_ref[0, :, sl]
        state = s_ref[gi]
        q_s = jnp.concatenate([jnp.where(lane_head == hh, q4, 0.0) for hh in range(HEADS_PER_GROUP)], axis=0)
        scores = _dot_nt(q_s, k4) * dmask_ref[d, gi]
        o_s = _dot(scores, v4)
        inner = jnp.where(lane_head == 0, o_s[0:C], 0.0)
        for hh in range(1, HEADS_PER_GROUP):
            inner = inner + jnp.where(lane_head == hh, o_s[hh * C:(hh + 1) * C], 0.0)
        y = inner + _dot(q4 * qdec_ref[d, :, sl], state)
        kd = (k4 * kdec_ref[d, :, sl]).T
        s_ref[gi] = state * cdec_ref[:, sl] + jnp.where(block_diag, _dot(kd, v4), 0.0)
        ms = _head_sum(y * y, 1.0 / HEAD_DIM)
        o_ref[0, :, sl] = _silu(g_ref[0, :, sl]) * (y * lax.rsqrt(ms + NORM_EPS))


def _ret_kernel(qf, kf, vf, gf, qb, kb, vb, gb, dmask, qdec, kdec, cdec, of, ob, sf, sb):
    @pl.when(pl.program_id(1) == 0)
    def _():
        sf[...] = jnp.zeros_like(sf)
        sb[...] = jnp.zeros_like(sb)

    _ret_direction(0, qf, kf, vf, gf, of, sf, dmask, qdec, kdec, cdec)
    _ret_direction(1, qb, kb, vb, gb, ob, sb, dmask, qdec, kdec, cdec)


def _retention_tables():
    n_heads = GROUP_WIDTH // HEAD_DIM
    log_gamma = np.log1p(-np.exp2(-5.0 - np.arange(n_heads, dtype=np.float64)))
    idx = np.arange(RET_CHUNK, dtype=np.float64)
    rel = idx[:, None] - idx[None, :]
    dm = np.where(rel >= 0, np.exp(np.maximum(rel, 0.0) * log_gamma[:, None, None]), 0.0)
    dmask = np.stack([dm, dm.transpose(0, 2, 1)])
    dmask = dmask.reshape(2, N_GROUPS, HEADS_PER_GROUP * RET_CHUNK, RET_CHUNK)
    pos = np.stack([idx, RET_CHUNK - 1.0 - idx])
    lg = np.repeat(log_gamma, HEAD_DIM)
    qdec = np.exp((pos[:, :, None] + 1.0) * lg)
    kdec = np.exp((RET_CHUNK - 1.0 - pos[:, :, None]) * lg)
    cdec = np.exp(RET_CHUNK * lg)[None, :]
    return [jnp.asarray(a, F32) for a in (dmask, qdec, kdec, cdec)]


def _retention(u_ret, T):
    B = u_ret.shape[0]
    n_lat = T // RET_CHUNK
    ns = n_lat + 1
    dmask, qdec, kdec, cdec = _retention_tables()

    def fchunk(s):
        return s

    def bchunk(s):
        return jnp.where(s == 0, 0, ns - s)

    def col(c, chunk):
        return pl.BlockSpec((1, RET_CHUNK, GROUP_WIDTH), lambda b, s: (b, chunk(s), c))

    const = lambda shape: pl.BlockSpec(shape, lambda b, s: (0,) * len(shape))
    in_specs = ([col(c, fchunk) for c in (0, 1, 2, 3)] + [col(c, bchunk) for c in (0, 1, 2, 4)]
                + [const(dmask.shape), const(qdec.shape), const(kdec.shape), const(cdec.shape)])
    out_specs = [pl.BlockSpec((1, RET_CHUNK, GROUP_WIDTH), lambda b, s: (b, jnp.maximum(s - 1, 0), 0)),
                 pl.BlockSpec((1, RET_CHUNK, GROUP_WIDTH), lambda b, s: (b, jnp.where(s == 0, n_lat - 1, n_lat - s), 0))]
    out_shape = [jax.ShapeDtypeStruct((B, T, GROUP_WIDTH), F32)] * 2
    return pl.pallas_call(
        _ret_kernel,
        name="retention",
        grid=(B, ns),
        in_specs=in_specs,
        out_specs=out_specs,
        out_shape=out_shape,
        scratch_shapes=[pltpu.VMEM((N_GROUPS, GROUP_LANES, GROUP_LANES), F32)] * 2,
        compiler_params=_cparams(("arbitrary", "arbitrary")),
    )(*([u_ret] * 8), dmask, qdec, kdec, cdec)


def _prep_kernel(n_tiles, u_ref, up_ref, un_ref, cw_ref, w0_ref, w2_ref, a0_ref, a2_ref, g2f_ref, g2b_ref,
                 kk_ref, ka_ref, rk_ref,
                 r_o, k_o, v_o, a_o, b_o, lwf_o, lwb_o, gf_o, gb_o, bon_o):
    j = pl.program_id(1)
    x = u_ref[0]
    rows = lax.broadcasted_iota(jnp.int32, (TOK_TILE, 1), 0)
    has_prev = (j >= 2).astype(F32)
    has_next = jnp.logical_and(j >= 1, j <= n_tiles - 2).astype(F32)
    prev_row = up_ref[0, 7:8, :] * has_prev
    next_row = un_ref[0, 0:1, :] * has_next
    xm = jnp.where(rows == 0, prev_row, pltpu.roll(x, 1, 0))
    xp = jnp.where(rows == TOK_TILE - 1, next_row, pltpu.roll(x, TOK_TILE - 1, 0))
    rw = cw_ref[0:1, :] * xm + cw_ref[1:2, :] * x + cw_ref[2:3, :] * xp

    W = GROUP_WIDTH
    r, k, v, lo = rw[:, 0:W], rw[:, W:2 * W], rw[:, 2 * W:3 * W], rw[:, 3 * W:4 * W]
    z = w0_ref[...] + _dot(jnp.tanh(lo[:, 0:LANES]), w2_ref[...])
    softplus = jnp.maximum(-z, 0.0) + jnp.log1p(jnp.exp(-jnp.abs(z)))
    log_decay = -jnp.exp(-softplus - 0.5)
    iclr = jax.nn.sigmoid(a0_ref[...] + _dot(lo[:, LANES:2 * LANES], a2_ref[...]))
    g_f = _dot(jax.nn.sigmoid(lo[:, 2 * LANES:3 * LANES]), g2f_ref[...])
    g_b = _dot(jax.nn.sigmoid(lo[:, 3 * LANES:4 * LANES]), g2b_ref[...])
    kk = k * kk_ref[...]
    kk = kk / jnp.maximum(jnp.sqrt(_head_sum(kk * kk)), 1e-12)
    k_mod = k * (1.0 + (iclr - 1.0) * ka_ref[...])
    bonus = _head_sum(r * k_mod * rk_ref[...]) * v
    r_o[0] = r
    k_o[0] = k_mod
    v_o[0] = v
    a_o[0] = -kk
    b_o[0] = kk * iclr
    lwf_o[0] = log_decay[:, 0:W]
    lwb_o[0] = log_decay[:, W:2 * W]
    gf_o[0] = g_f
    gb_o[0] = g_b
    bon_o[0] = (g_f + g_b) * bonus


def _rwkv_prepare(u_rw, T, cw, w0cat, w2bd, a0, a2pad, g2f, g2b, k_k, k_a, r_k):
    B, S, C4 = u_rw.shape
    nt = S // TOK_TILE
    rb = TOK_TILE // 8
    const = lambda a: pl.BlockSpec(a.shape, lambda b, j: (0,) * a.ndim)
    consts = (cw, w0cat, w2bd, a0, a2pad, g2f, g2b, k_k, k_a, r_k)
    in_specs = [pl.BlockSpec((1, TOK_TILE, C4), lambda b, j: (b, j, 0)),
                pl.BlockSpec((1, 8, C4), lambda b, j: (b, jnp.maximum(j * rb - 1, 0), 0)),
                pl.BlockSpec((1, 8, C4), lambda b, j: (b, jnp.minimum((j + 1) * rb, S // 8 - 1), 0))]
    in_specs += [const(a) for a in consts]
    seq_spec = pl.BlockSpec((1, TOK_TILE, GROUP_WIDTH), lambda b, j: (b, j, 0))
    lat_spec = pl.BlockSpec((1, TOK_TILE, GROUP_WIDTH), lambda b, j: (b, jnp.maximum(j - 1, 0), 0))
    seq_shape = jax.ShapeDtypeStruct((B, S, GROUP_WIDTH), F32)
    lat_shape = jax.ShapeDtypeStruct((B, T, GROUP_WIDTH), F32)
    return pl.pallas_call(
        functools.partial(_prep_kernel, nt),
        name="rwkv_prepare",
        grid=(B, nt),
        in_specs=in_specs,
        out_specs=[seq_spec] * 7 + [lat_spec] * 3,
        out_shape=[seq_shape] * 7 + [lat_shape] * 3,
        compiler_params=_cparams(("arbitrary", "arbitrary")),
    )(u_rw, u_rw, u_rw, *consts)


def _dplr_local(r, k, v, a, b, lw, reverse):
    C, H, GL = RWKV_CHUNK, HEADS_PER_GROUP, GROUP_LANES
    CS = H * C
    ri = lax.broadcasted_iota(jnp.int32, (C, C), 0)
    ci = lax.broadcasted_iota(jnp.int32, (C, C), 1)
    tri = jnp.where((ci >= ri) if reverse else (ci <= ri), 1.0, 0.0).astype(BF16)
    lc = jnp.dot(jnp.concatenate([tri, tri, tri], axis=1), jnp.concatenate(_split_bf16(lw, 3), axis=0),
                 preferred_element_type=F32)
    yield
    ltot = jnp.sum(lw, axis=0, keepdims=True)
    a_t = a * jnp.exp(lc - lw)
    r_t = r * jnp.exp(lc)
    inv = jnp.exp(-lc)
    b_t, k_t = b * inv, k * inv
    to_end = jnp.exp(ltot - lc)
    b_e, k_e = b * to_end, k * to_end

    rs = lax.broadcasted_iota(jnp.int32, (CS, CS), 0)
    cs = lax.broadcasted_iota(jnp.int32, (CS, CS), 1)
    bd = (rs // C) == (cs // C)
    rt, ct = rs % C, cs % C
    incl_bd = jnp.logical_and(bd, (ct >= rt) if reverse else (ct <= rt))
    strict_bd = jnp.logical_and(bd, (ct > rt) if reverse else (ct < rt))
    stack = lambda x: jnp.where(bd, jnp.concatenate([x] * H, axis=0), 0.0)
    unstack = lambda x: x[0:C] + x[C:2 * C] + x[2 * C:3 * C] + x[3 * C:4 * C]
    a_s, r_s, b_s, k_s, v_s = stack(a_t), stack(r_t), stack(b_t), stack(k_t), stack(v)
    g = _dot_nt(jnp.concatenate([a_s, r_s], axis=0), jnp.concatenate([b_s, k_s], axis=0))
    low = jnp.where(strict_bd, g[0:CS, 0:CS], 0.0)
    ak = jnp.where(strict_bd, g[0:CS, CS:], 0.0)
    rb = jnp.where(incl_bd, g[CS:, 0:CS], 0.0)
    rk = jnp.where(incl_bd, g[CS:, CS:], 0.0)
    yield
    akv = _dot(ak, v_s)
    tm = jnp.where(rs == cs, 1.0, 0.0) + low
    pw = _dot(low, low)
    yield
    n = 2
    while n < C // 2:
        both = _dot(jnp.concatenate([pw, tm], axis=0), pw)
        pw, tm = both[0:CS], tm + both[CS:]
        n *= 2
        yield
    tm = tm + _dot(tm, pw)
    yield
    zeros_s = jnp.zeros((CS, GL), F32)
    uw = _dot(tm, jnp.concatenate([akv, a_s], axis=1))
    yield
    r2 = jnp.concatenate([uw, jnp.concatenate([v_s, zeros_s], axis=1)], axis=0)
    yq = _dot(jnp.concatenate([rb, rk], axis=1), r2)
    yield
    u0, w = unstack(uw[:, 0:GL]), unstack(uw[:, GL:])
    y0, q_add = unstack(yq[:, 0:GL]), unstack(yq[:, GL:])
    zeros = jnp.zeros((C, GL), F32)
    stack_t = jnp.concatenate([w, u0, v, zeros], axis=0).T
    rm = jnp.concatenate([jnp.concatenate([b_e, zeros], axis=1),
                          jnp.concatenate([zeros, b_e], axis=1),
                          jnp.concatenate([zeros, k_e], axis=1),
                          jnp.concatenate([zeros, zeros], axis=1)], axis=0)
    mn = _dot(stack_t, rm)
    return (r_t + q_add, y0, jnp.exp(ltot), jnp.where(bd, mn[:, 0:GL], 0.0), jnp.where(bd, mn[:, GL:], 0.0))


def _round_robin(gens):
    results = [None] * len(gens)
    active = list(range(len(gens)))
    while active:
        for i in list(active):
            try:
                next(gens[i])
            except StopIteration as done:
                results[i] = done.value
                active.remove(i)
    return results


def _scan_kernel(rf, kf, vf, af, bf, lwf, gf, rb, kb, vb, ab, bb, lwb, gb, lnw, lnb, of, ob, sf, sb):
    @pl.when(pl.program_id(1) == 0)
    def _():
        sf[...] = jnp.zeros_like(sf)
        sb[...] = jnp.zeros_like(sb)

    C = RWKV_CHUNK
    fwd = (False, (rf, kf, vf, af, bf, lwf), gf, of, sf)
    bwd = (True, (rb, kb, vb, ab, bb, lwb), gb, ob, sb)
    chains = []
    for reverse, refs, g_ref, o_ref, s_ref in (fwd, bwd):
        subs = range(RWKV_CHUNKS_PER_STEP)
        for gi in range(N_GROUPS):
            sl = slice(gi * GROUP_LANES, (gi + 1) * GROUP_LANES)
            for sub in (reversed(subs) if reverse else subs):
                rows = slice(sub * C, (sub + 1) * C)
                chains.append((gi, sl, rows, g_ref, o_ref, s_ref,
                               _dplr_local(*[ref[0, rows, sl] for ref in refs], reverse)))
    local = _round_robin([c[-1] for c in chains])
    states = {}
    for (gi, sl, rows, g_ref, o_ref, s_ref, _), (q, y0, decay, m_t, n_t) in zip(chains, local):
        key = (id(s_ref), gi)
        st = states.get(key)
        if st is None:
            st = s_ref[gi]
        y = _dot_nt(q, st) + y0
        states[key] = st * decay + _dot(st, m_t) + n_t
        dlt = y - _head_sum(y, 1.0 / HEAD_DIM)
        var = _head_sum(dlt * dlt, 1.0 / HEAD_DIM)
        yn = dlt * lax.rsqrt(var + RWKV_GN_EPS) * lnw[:, sl] + lnb[:, sl]
        o_ref[0, rows, sl] = g_ref[0, rows, sl] * yn
    for s_ref in (sf, sb):
        for gi in range(N_GROUPS):
            s_ref[gi] = states[(id(s_ref), gi)]


def _rwkv_scan(r, k, v, a, b, lwf, lwb, gf, gb, lnw, lnb, T):
    B, S, W = r.shape
    C = RWKV_CHUNK * RWKV_CHUNKS_PER_STEP
    n_ctx = CTX_LEN // C
    n_lat = T // C
    ns = n_ctx + n_lat

    def fchunk(s):
        return s

    def bchunk(s):
        return jnp.where(s < n_ctx, n_ctx - 1 - s, ns + n_ctx - 1 - s)

    def fout(s):
        return jnp.maximum(s - n_ctx, 0)

    def bout(s):
        return jnp.where(s < n_ctx, n_lat - 1, ns - 1 - s)

    def spec(chunk):
        return pl.BlockSpec((1, C, W), lambda bi, s: (bi, chunk(s), 0))

    const = pl.BlockSpec((1, W), lambda bi, s: (0, 0))
    in_specs = [spec(fchunk)] * 6 + [spec(fout)] + [spec(bchunk)] * 6 + [spec(bout)] + [const, const]
    return pl.pallas_call(
        _scan_kernel,
        name="rwkv_scan",
        grid=(B, ns),
        in_specs=in_specs,
        out_specs=[spec(fout), spec(bout)],
        out_shape=[jax.ShapeDtypeStruct((B, T, W), F32)] * 2,
        scratch_shapes=[pltpu.VMEM((N_GROUPS, GROUP_LANES, GROUP_LANES), F32)] * 2,
        compiler_params=_cparams(("arbitrary", "arbitrary")),
    )(r, k, v, a, b, lwf, gf, r, k, v, a, b, lwb, gb, lnw, lnb)


def _out_kernel(x_ref, rf_ref, rb_ref, wf_ref, wb_ref, bon_ref, wo_ref, mod_ref, gains_ref, wr_ref,
                x1_ref, h2_ref, aff_ref):
    W = GROUP_WIDTH
    ret = rf_ref[0] + rb_ref[0]
    rwk = wf_ref[0] + wb_ref[0] + bon_ref[0]
    mix = _dot(ret, wo_ref[0:W, :]) + _dot(rwk, wo_ref[W:2 * W, :])
    m = mod_ref[0]
    D = D_MODEL
    ms = jnp.mean(mix * mix, axis=-1, keepdims=True)
    x1 = x_ref[0] + m[:, 2 * D:3 * D] * (mix * lax.rsqrt(ms + NORM_EPS) * gains_ref[1:2, :])
    x1_ref[0] = x1
    ms2 = jnp.mean(x1 * x1, axis=-1, keepdims=True)
    h2 = (x1 * lax.rsqrt(ms2 + NORM_EPS) * gains_ref[2:3, :]) * (1.0 + m[:, 4 * D:5 * D]) + m[:, 3 * D:4 * D]
    h2_ref[0] = h2.astype(BF16)
    logits = _dot_exact(h2, wr_ref[...]).T[0:N_EXPERTS, :]
    e = jnp.exp(logits - jnp.max(logits, axis=0, keepdims=True))
    aff_ref[0] = e / jnp.sum(e, axis=0, keepdims=True)


def _out_projection(x, ret_f, ret_b, rw_f, rw_b, bonus, w_out, mods, gains, wr_pad):
    B, T, D = x.shape
    nt = T // OUT_TILE
    tok = lambda w: pl.BlockSpec((1, OUT_TILE, w), lambda b, i: (b, i, 0))
    const = lambda a: pl.BlockSpec(a.shape, lambda b, i: (0,) * a.ndim)
    return pl.pallas_call(
        _out_kernel,
        name="out_projection",
        grid=(B, nt),
        in_specs=[tok(D)] + [tok(GROUP_WIDTH)] * 5 + [const(w_out),
                  pl.BlockSpec((1, 1, mods.shape[2]), lambda b, i: (b, 0, 0)), const(gains), const(wr_pad)],
        out_specs=[tok(D), tok(D), pl.BlockSpec((1, N_EXPERTS, OUT_TILE), lambda b, i: (b, 0, i))],
        out_shape=[jax.ShapeDtypeStruct((B, T, D), F32), jax.ShapeDtypeStruct((B, T, D), BF16),
                   jax.ShapeDtypeStruct((B, N_EXPERTS, T), F32)],
        compiler_params=_cparams(("arbitrary", "arbitrary")),
    )(x, ret_f, ret_b, rw_f, rw_b, bonus, w_out, mods, gains, wr_pad)


def _cumsum_lanes(x):
    n = x.shape[1]
    lane = lax.broadcasted_iota(jnp.int32, (1, n), 1)
    sh = 1
    while sh < n:
        x = x + jnp.where(lane >= sh, pltpu.roll(x, sh, 1), 0.0)
        sh *= 2
    return x


def _select_kernel(cap, aff_ref, slot_ref, slot_t_ref):
    aff = aff_ref[0]
    capf = jnp.float32(cap)

    def body(i, thr):
        cand = thr | (jnp.int32(1) << (30 - i))
        cnt = jnp.sum(jnp.where(aff >= pltpu.bitcast(cand, F32), 1.0, 0.0), axis=1, keepdims=True)
        return jnp.where(cnt >= capf, cand, thr)

    thr = lax.fori_loop(0, 31, body, jnp.zeros((aff.shape[0], 1), jnp.int32))
    min_normal_bits = jnp.int32(0x00800000)
    above = pltpu.bitcast(jnp.maximum(thr + 1, min_normal_bits), F32)
    gt = aff >= above
    eq = jnp.where(jnp.logical_and(aff >= pltpu.bitcast(thr, F32), jnp.logical_not(gt)), 1.0, 0.0)
    need = capf - jnp.sum(jnp.where(gt, 1.0, 0.0), axis=1, keepdims=True)
    eq_before = _cumsum_lanes(eq) - eq
    sel = jnp.where(jnp.logical_or(gt, jnp.logical_and(eq > 0.0, eq_before < need)), 1.0, 0.0)
    slot = jnp.where(sel > 0.0, _cumsum_lanes(sel) - sel, -1.0)
    slot_ref[0] = slot
    pad = jnp.full((LANES - slot.shape[0], slot.shape[1]), -1.0, F32)
    slot_t_ref[0] = jnp.concatenate([slot, pad], axis=0).T


def _select(aff, cap):
    B, E, T = aff.shape
    return pl.pallas_call(
        functools.partial(_select_kernel, cap),
        name="expert_select",
        grid=(B,),
        in_specs=[pl.BlockSpec((1, E, T), lambda b: (b, 0, 0))],
        out_specs=[pl.BlockSpec((1, E, T), lambda b: (b, 0, 0)), pl.BlockSpec((1, T, LANES), lambda b: (b, 0, 0))],
        out_shape=[jax.ShapeDtypeStruct((B, E, T), F32), jax.ShapeDtypeStruct((B, T, LANES), F32)],
        compiler_params=_cparams(("arbitrary",)),
    )(aff)


def _expert_kernel(cap, h_ref, slot_ref, aff_ref, wg_ref, wu_ref, wd_ref, ye_ref):
    T = h_ref.shape[1]
    kc = min(T, 1024)
    jj = lax.broadcasted_iota(jnp.int32, (cap, 1), 0).astype(F32)
    xs = jnp.zeros((cap, D_MODEL), F32)
    gate = jnp.zeros((cap, 1), F32)
    for c in range(T // kc):
        hit = slot_ref[0, 0, :, c * kc:(c + 1) * kc] == jj
        xs = xs + jnp.dot(jnp.where(hit, 1.0, 0.0).astype(BF16), h_ref[0, c * kc:(c + 1) * kc, :],
                          preferred_element_type=F32)
        gate = gate + jnp.sum(jnp.where(hit, aff_ref[0, 0, :, c * kc:(c + 1) * kc], 0.0), axis=1, keepdims=True)
    xs = xs.astype(BF16)
    hid = _silu(jnp.dot(xs, wg_ref[0], preferred_element_type=F32)) * jnp.dot(xs, wu_ref[0], preferred_element_type=F32)
    ye = jnp.dot(hid.astype(BF16), wd_ref[0], preferred_element_type=F32) * gate
    ye_ref[0, 0] = ye.astype(BF16)


def _expert_ffn(h2, slot, aff, wg, wu, wd, cap):
    B, T, D = h2.shape
    E = wg.shape[0]
    row = pl.BlockSpec((1, 1, 1, T), lambda b, e: (b, e, 0, 0))
    wspec = pl.BlockSpec((1, D, D), lambda b, e: (e, 0, 0))
    return pl.pallas_call(
        functools.partial(_expert_kernel, cap),
        name="expert_ffn",
        grid=(B, E),
        in_specs=[pl.BlockSpec((1, T, D), lambda b, e: (b, 0, 0)), row, row, wspec, wspec, wspec],
        out_specs=pl.BlockSpec((1, 1, cap, D), lambda b, e: (b, e, 0, 0)),
        out_shape=jax.ShapeDtypeStruct((B, E, cap, D), BF16),
        compiler_params=_cparams(("arbitrary", "arbitrary")),
    )(h2, slot.reshape(B, E, 1, T), aff.reshape(B, E, 1, T), wg, wu, wd)


def _combine_kernel(cap, x1_ref, slot_t_ref, ye_ref, mod_ref, gain_ref, o_ref):
    jj = lax.broadcasted_iota(jnp.int32, (1, cap), 1).astype(F32)
    st = slot_t_ref[0]
    acc = jnp.zeros((OUT_TILE, D_MODEL), F32)
    for e in range(N_EXPERTS):
        hit = st[:, e:e + 1] == jj
        acc = acc + jnp.dot(jnp.where(hit, 1.0, 0.0).astype(BF16), ye_ref[0, e], preferred_element_type=F32)
    ms = jnp.mean(acc * acc, axis=-1, keepdims=True)
    gt2 = mod_ref[0][:, 5 * D_MODEL:6 * D_MODEL]
    o_ref[0] = x1_ref[0] + gt2 * (acc * lax.rsqrt(ms + NORM_EPS) * gain_ref[...])


def _combine(x1, slot_t, ye, mods, gain, cap):
    B, T, D = x1.shape
    E = ye.shape[1]
    return pl.pallas_call(
        functools.partial(_combine_kernel, cap),
        name="combine",
        grid=(B, T // OUT_TILE),
        in_specs=[pl.BlockSpec((1, OUT_TILE, D), lambda b, i: (b, i, 0)),
                  pl.BlockSpec((1, OUT_TILE, LANES), lambda b, i: (b, i, 0)),
                  pl.BlockSpec((1, E, cap, D), lambda b, i: (b, 0, 0, 0)),
                  pl.BlockSpec((1, 1, mods.shape[2]), lambda b, i: (b, 0, 0)),
                  pl.BlockSpec((1, D), lambda b, i: (0, 0))],
        out_specs=pl.BlockSpec((1, OUT_TILE, D), lambda b, i: (b, i, 0)),
        out_shape=jax.ShapeDtypeStruct((B, T, D), F32),
        compiler_params=_cparams(("arbitrary", "arbitrary")),
    )(x1, slot_t, ye, mods, gain)


def _rope_tables(T):
    rows = T // GRID_W
    row = np.repeat(np.arange(rows, dtype=np.float32), GRID_W)
    col = np.tile(np.arange(GRID_W, dtype=np.float32), rows)
    n_freq = HEAD_DIM // 4
    freq = jnp.asarray(ROPE_BASE, F32) ** (-jnp.arange(n_freq, dtype=F32) / n_freq)
    ang = jnp.concatenate([row[:, None] * freq, col[:, None] * freq], axis=-1)
    cos, sin = jnp.cos(ang), jnp.sin(ang)
    reps = LANES // (HEAD_DIM // 2)
    cos_t = jnp.tile(cos, (1, reps))
    sin_t = jnp.tile(jnp.concatenate([-sin, sin], axis=-1), (1, reps // 2))
    return cos_t, sin_t


def kernel(x, c, ctx, c_ctx, w_mod, b_mod, norm_gains, w_in, rwkv_conv, rwkv_w0, rwkv_w2, rwkv_a0, rwkv_a2, rwkv_g2, rwkv_k_k, rwkv_k_a, rwkv_r_k, rwkv_lnx_w, rwkv_lnx_b, w_out, w_router, w_gate, w_up, w_down):
    B, T, D = x.shape
    assert D == D_MODEL and ctx.shape == (B, CTX_LEN, D) and T % OUT_TILE == 0 and w_mod.shape[0] == 1
    W = GROUP_WIDTH
    cap = max(1, EC_CAPACITY * T // N_EXPERTS)

    n_rows = -(-(B + 1) // 8) * 8
    cc = jnp.concatenate([c, c_ctx[None, :], jnp.zeros((n_rows - B - 1, D), F32)], axis=0)
    mods = _modulation(cc, w_mod[0], b_mod).reshape(n_rows, 1, 6 * D)

    split = RET_COLS + 3 * W + 2 * DECAY_LORA + ICLR_LORA
    w_pad = jnp.concatenate([w_in[0][:, :split], jnp.zeros((D, LANES - ICLR_LORA), F32), w_in[0][:, split:]],
                            axis=1).astype(BF16)
    cs = split - RET_COLS
    cw = jnp.concatenate([rwkv_conv[0][:, :cs], jnp.zeros((3, LANES - ICLR_LORA), F32), rwkv_conv[0][:, cs:]], axis=1)
    zl = jnp.zeros((DECAY_LORA, W), F32)
    w2bd = jnp.concatenate([jnp.concatenate([rwkv_w2[0, 0], zl], axis=1),
                            jnp.concatenate([zl, rwkv_w2[0, 1]], axis=1)], axis=0).astype(BF16)
    w0cat = jnp.concatenate([rwkv_w0[0, 0], rwkv_w0[0, 1]])[None, :]
    a2pad = jnp.concatenate([rwkv_a2[0], jnp.zeros((LANES - ICLR_LORA, W), F32)], axis=0).astype(BF16)
    g2f, g2b = rwkv_g2[0, 0].astype(BF16), rwkv_g2[0, 1].astype(BF16)
    row = lambda a: a[0][None, :]
    cos_t, sin_t = _rope_tables(T)

    u_ret, u_rw = _projection(x, ctx, mods, norm_gains[0, 0][None, :], w_pad, cos_t, sin_t)
    ret_f, ret_b = _retention(u_ret, T)
    r, k, v, a, b, lwf, lwb, gf, gb, bonus = _rwkv_prepare(
        u_rw, T, cw, w0cat, w2bd, row(rwkv_a0), a2pad, g2f, g2b, row(rwkv_k_k), row(rwkv_k_a), row(rwkv_r_k))
    rw_f, rw_b = _rwkv_scan(r, k, v, a, b, lwf, lwb, gf, gb, row(rwkv_lnx_w), row(rwkv_lnx_b), T)

    wr_pad = jnp.concatenate([w_router[0], jnp.zeros((D, LANES - N_EXPERTS), F32)], axis=1)
    x1, h2, aff = _out_projection(x, ret_f, ret_b, rw_f, rw_b, bonus, w_out[0].astype(BF16), mods,
                                  norm_gains[0], wr_pad)
    slot, slot_t = _select(aff, cap)
    ye = _expert_ffn(h2, slot, aff, w_gate[0].astype(BF16), w_up[0].astype(BF16), w_down[0].astype(BF16), cap)
    return _combine(x1, slot_t, ye, mods, norm_gains[0, 3][None, :], cap)
```

```python
import functools

import numpy as np
import jax
import jax.numpy as jnp
from jax import lax
from jax.experimental import pallas as pl
from jax.experimental.pallas import tpu as pltpu

F32 = jnp.float32
BF16 = jnp.bfloat16
HIGHEST = lax.Precision.HIGHEST

D_MODEL = 1024
CTX_LEN = 256
GRID_W = 64
HEAD_DIM = 64
GROUP_WIDTH = 512
LANES = 128
N_PAIRS = GROUP_WIDTH // LANES
HEADS_PER_GROUP = 4
GROUP_LANES = HEADS_PER_GROUP * HEAD_DIM
N_GROUPS = GROUP_WIDTH // GROUP_LANES
RET_COLS = 5 * GROUP_WIDTH
RWKV_COLS_PAD = 4 * GROUP_WIDTH
DECAY_LORA = 64
ICLR_LORA = 64
GATE_LORA = 128
N_EXPERTS = 16
EC_CAPACITY = 2
ROPE_BASE = 10000.0
NORM_EPS = 1e-6
RWKV_GN_EPS = 64e-5

TOK_TILE = 256
RET_CHUNK = 256
RWKV_CHUNK = 64
RWKV_CHUNKS_PER_STEP = 4
OUT_TILE = 512
BAND_TILE = 256
GATHER_WINDOW = BAND_TILE + 16
SLOT_WINDOW = 128
BAND_COLS = 32
VMEM_LIMIT = 56 * 1024 * 1024


def _cparams(sem):
    return pltpu.CompilerParams(dimension_semantics=sem, vmem_limit_bytes=VMEM_LIMIT)


def _dot(a, b):
    return jnp.dot(a.astype(BF16), b.astype(BF16), preferred_element_type=F32)


def _dot_nt(a, b):
    return lax.dot_general(a.astype(BF16), b.astype(BF16), (((1,), (1,)), ((), ())),
                           preferred_element_type=F32)


def _dot_exact(a, b):
    return jnp.dot(a, b, precision=HIGHEST, preferred_element_type=F32)


def _head_block_mask(n):
    r = lax.broadcasted_iota(jnp.int32, (n, n), 0)
    c = lax.broadcasted_iota(jnp.int32, (n, n), 1)
    return (r // HEAD_DIM) == (c // HEAD_DIM)


def _split_bf16(x, terms):
    out = []
    for _ in range(terms - 1):
        hi = x.astype(BF16)
        out.append(hi)
        x = x - hi.astype(F32)
    out.append(x.astype(BF16))
    return out


def _head_sum(x, scale=1.0):
    cols = []
    for g in range(x.shape[1] // GROUP_LANES):
        m = jnp.where(_head_block_mask(GROUP_LANES), scale, 0.0).astype(BF16)
        parts = _split_bf16(x[:, g * GROUP_LANES:(g + 1) * GROUP_LANES], 2)
        cols.append(jnp.dot(jnp.concatenate(parts, axis=1), jnp.concatenate([m, m], axis=0),
                            preferred_element_type=F32))
    return cols[0] if len(cols) == 1 else jnp.concatenate(cols, axis=1)


def _silu(x):
    return x * jax.nn.sigmoid(x)


def _round_robin(gens):
    results = [None] * len(gens)
    active = list(range(len(gens)))
    while active:
        for i in list(active):
            try:
                next(gens[i])
            except StopIteration as done:
                results[i] = done.value
                active.remove(i)
    return results


def _mod_kernel(c_ref, w_ref, b_ref, o_ref):
    o_ref[...] = _dot_exact(_silu(c_ref[...]), w_ref[...]) + b_ref[...]


def _modulation(cc, w_mod, b_mod):
    rows, d = cc.shape
    n = w_mod.shape[1]
    tn = 1536
    return pl.pallas_call(
        _mod_kernel,
        name="modulation",
        grid=(n // tn,),
        in_specs=[pl.BlockSpec((rows, d), lambda i: (0, 0)),
                  pl.BlockSpec((d, tn), lambda i: (0, i)),
                  pl.BlockSpec((1, tn), lambda i: (0, i))],
        out_specs=pl.BlockSpec((rows, tn), lambda i: (0, i)),
        out_shape=jax.ShapeDtypeStruct((rows, n), F32),
        compiler_params=_cparams(("arbitrary",)),
    )(cc, w_mod, b_mod)


def _proj_kernel(x_ref, ctx_ref, mod_ref, gain_ref, w_ref, cos_ref, sin_ref, ret_ref, rw_ref):
    is_lat = pl.program_id(1) > 0
    xin = jnp.where(is_lat, x_ref[0], ctx_ref[0])
    ms = jnp.mean(xin * xin, axis=-1, keepdims=True)
    y = xin * lax.rsqrt(ms + NORM_EPS) * gain_ref[...]
    m = mod_ref[0]
    h = y * (1.0 + m[:, D_MODEL:2 * D_MODEL]) + m[:, 0:D_MODEL]
    u = _dot(h, w_ref[...])
    lat = is_lat.astype(F32)
    cos = cos_ref[...] * lat + (1.0 - lat)
    sin = sin_ref[...] * lat
    lane = lax.broadcasted_iota(jnp.int32, (1, LANES), 1)
    first_half = (lane % HEAD_DIM) < (HEAD_DIM // 2)
    for base, scale in ((0, 1.0), (GROUP_WIDTH, HEAD_DIM ** -0.5)):
        for g in range(N_PAIRS):
            lo = base + g * LANES
            t = u[:, lo:lo + LANES] * scale
            sw = jnp.where(first_half, pltpu.roll(t, LANES - HEAD_DIM // 2, 1), pltpu.roll(t, HEAD_DIM // 2, 1))
            ret_ref[0, :, lo:lo + LANES] = t * cos + sw * sin
    ret_ref[0, :, 2 * GROUP_WIDTH:] = u[:, 2 * GROUP_WIDTH:RET_COLS]
    rw_ref[0] = u[:, RET_COLS:]


def _projection(x, ctx, mods, gain, w_pad, cos_t, sin_t):
    B, T, D = x.shape
    nt = (T + CTX_LEN) // TOK_TILE
    S = T + CTX_LEN
    n_cols = w_pad.shape[1]
    return pl.pallas_call(
        _proj_kernel,
        name="projection",
        grid=(B, nt),
        in_specs=[
            pl.BlockSpec((1, TOK_TILE, D), lambda b, j: (b, jnp.maximum(j - 1, 0), 0)),
            pl.BlockSpec((1, TOK_TILE, D), lambda b, j: (b, 0, 0)),
            pl.BlockSpec((1, 1, mods.shape[2]), lambda b, j: (jnp.where(j == 0, B, b), 0, 0)),
            pl.BlockSpec((1, D), lambda b, j: (0, 0)),
            pl.BlockSpec((D, n_cols), lambda b, j: (0, 0)),
            pl.BlockSpec((TOK_TILE, LANES), lambda b, j: (jnp.maximum(j - 1, 0), 0)),
            pl.BlockSpec((TOK_TILE, LANES), lambda b, j: (jnp.maximum(j - 1, 0), 0)),
        ],
        out_specs=[pl.BlockSpec((1, TOK_TILE, RET_COLS), lambda b, j: (b, j, 0)),
                   pl.BlockSpec((1, TOK_TILE, RWKV_COLS_PAD), lambda b, j: (b, j, 0))],
        out_shape=[jax.ShapeDtypeStruct((B, S, RET_COLS), F32),
                   jax.ShapeDtypeStruct((B, S, RWKV_COLS_PAD), F32)],
        compiler_params=_cparams(("arbitrary", "arbitrary")),
    )(x, ctx, mods, gain, w_pad, cos_t, sin_t)


def _ret_group(d, gi, q_ref, k_ref, v_ref, g_ref, o_ref, s_ref, dmask_ref, qdec_ref, kdec_ref, cdec_ref):
    C = RET_CHUNK
    lane_head = lax.broadcasted_iota(jnp.int32, (1, GROUP_LANES), 1) // HEAD_DIM
    sl = slice(gi * GROUP_LANES, (gi + 1) * GROUP_LANES)
    q4, k4, v4 = q_ref[0, :, sl], k_ref[0, :, sl], v_ref[0, :, sl]
    state = s_ref[gi]
    q_s = jnp.concatenate([jnp.where(lane_head == hh, q4, 0.0) for hh in range(HEADS_PER_GROUP)], axis=0)
    scores = _dot_nt(q_s, k4) * dmask_ref[d, gi]
    cross = _dot(q4 * qdec_ref[d, :, sl], state)
    yield
    o_s = _dot(scores, v4)
    kd = (k4 * kdec_ref[d, :, sl]).T
    s_ref[gi] = state * cdec_ref[:, sl] + jnp.where(_head_block_mask(GROUP_LANES), _dot(kd, v4), 0.0)
    yield
    y = cross + jnp.where(lane_head == 0, o_s[0:C], 0.0)
    for hh in range(1, HEADS_PER_GROUP):
        y = y + jnp.where(lane_head == hh, o_s[hh * C:(hh + 1) * C], 0.0)
    ms = _head_sum(y * y, 1.0 / HEAD_DIM)
    o_ref[0, :, sl] = _silu(g_ref[0, :, sl]) * (y * lax.rsqrt(ms + NORM_EPS))


def _ret_kernel(qf, kf, vf, gf, qb, kb, vb, gb, dmask, qdec, kdec, cdec, of, ob, sf, sb):
    @pl.when(pl.program_id(1) == 0)
    def _():
        sf[...] = jnp.zeros_like(sf)
        sb[...] = jnp.zeros_like(sb)

    tables = (dmask, qdec, kdec, cdec)
    _round_robin([_ret_group(d, gi, *refs, *tables)
                  for d, refs in ((0, (qf, kf, vf, gf, of, sf)), (1, (qb, kb, vb, gb, ob, sb)))
                  for gi in range(N_GROUPS)])


def _retention_tables():
    n_heads = GROUP_WIDTH // HEAD_DIM
    log_gamma = np.log1p(-np.exp2(-5.0 - np.arange(n_heads, dtype=np.float64)))
    idx = np.arange(RET_CHUNK, dtype=np.float64)
    rel = idx[:, None] - idx[None, :]
    dm = np.where(rel >= 0, np.exp(np.maximum(rel, 0.0) * log_gamma[:, None, None]), 0.0)
    dmask = np.stack([dm, dm.transpose(0, 2, 1)])
    dmask = dmask.reshape(2, N_GROUPS, HEADS_PER_GROUP * RET_CHUNK, RET_CHUNK)
    pos = np.stack([idx, RET_CHUNK - 1.0 - idx])
    lg = np.repeat(log_gamma, HEAD_DIM)
    qdec = np.exp((pos[:, :, None] + 1.0) * lg)
    kdec = np.exp((RET_CHUNK - 1.0 - pos[:, :, None]) * lg)
    cdec = np.exp(RET_CHUNK * lg)[None, :]
    return [jnp.asarray(a, F32) for a in (dmask, qdec, kdec, cdec)]


def _retention(u_ret, T):
    B = u_ret.shape[0]
    n_lat = T // RET_CHUNK
    ns = n_lat + 1
    dmask, qdec, kdec, cdec = _retention_tables()

    def fchunk(s):
        return s

    def bchunk(s):
        return jnp.where(s == 0, 0, ns - s)

    def col(c, chunk):
        return pl.BlockSpec((1, RET_CHUNK, GROUP_WIDTH), lambda b, s: (b, chunk(s), c))

    const = lambda shape: pl.BlockSpec(shape, lambda b, s: (0,) * len(shape))
    in_specs = ([col(c, fchunk) for c in (0, 1, 2, 3)] + [col(c, bchunk) for c in (0, 1, 2, 4)]
                + [const(dmask.shape), const(qdec.shape), const(kdec.shape), const(cdec.shape)])
    out_specs = [pl.BlockSpec((1, RET_CHUNK, GROUP_WIDTH), lambda b, s: (b, jnp.maximum(s - 1, 0), 0)),
                 pl.BlockSpec((1, RET_CHUNK, GROUP_WIDTH), lambda b, s: (b, jnp.where(s == 0, n_lat - 1, n_lat - s), 0))]
    out_shape = [jax.ShapeDtypeStruct((B, T, GROUP_WIDTH), F32)] * 2
    return pl.pallas_call(
        _ret_kernel,
        name="retention",
        grid=(B, ns),
        in_specs=in_specs,
        out_specs=out_specs,
        out_shape=out_shape,
        scratch_shapes=[pltpu.VMEM((N_GROUPS, GROUP_LANES, GROUP_LANES), F32)] * 2,
        compiler_params=_cparams(("arbitrary", "arbitrary")),
    )(*([u_ret] * 8), dmask, qdec, kdec, cdec)


def _prep_kernel(n_tiles, u_ref, up_ref, un_ref, cw_ref, w0_ref, w2_ref, a0_ref, a2_ref, g2f_ref, g2b_ref,
                 kk_ref, ka_ref, rk_ref,
                 r_o, k_o, v_o, a_o, b_o, lwf_o, lwb_o, gf_o, gb_o, bon_o):
    j = pl.program_id(1)
    x = u_ref[0]
    rows = lax.broadcasted_iota(jnp.int32, (TOK_TILE, 1), 0)
    has_prev = (j >= 2).astype(F32)
    has_next = jnp.logical_and(j >= 1, j <= n_tiles - 2).astype(F32)
    prev_row = up_ref[0, 7:8, :] * has_prev
    next_row = un_ref[0, 0:1, :] * has_next
    xm = jnp.where(rows == 0, prev_row, pltpu.roll(x, 1, 0))
    xp = jnp.where(rows == TOK_TILE - 1, next_row, pltpu.roll(x, TOK_TILE - 1, 0))
    rw = cw_ref[0:1, :] * xm + cw_ref[1:2, :] * x + cw_ref[2:3, :] * xp

    W = GROUP_WIDTH
    r, k, v, lo = rw[:, 0:W], rw[:, W:2 * W], rw[:, 2 * W:3 * W], rw[:, 3 * W:4 * W]
    z = w0_ref[...] + _dot(jnp.tanh(lo[:, 0:LANES]), w2_ref[...])
    log_decay = -np.exp(-0.5).astype(np.float32) * jax.nn.sigmoid(z)
    iclr = jax.nn.sigmoid(a0_ref[...] + _dot(lo[:, LANES:2 * LANES], a2_ref[...]))
    g_f = _dot(jax.nn.sigmoid(lo[:, 2 * LANES:3 * LANES]), g2f_ref[...])
    g_b = _dot(jax.nn.sigmoid(lo[:, 3 * LANES:4 * LANES]), g2b_ref[...])
    kk = k * kk_ref[...]
    kk = kk / jnp.maximum(jnp.sqrt(_head_sum(kk * kk)), 1e-12)
    k_mod = k * (1.0 + (iclr - 1.0) * ka_ref[...])
    bonus = _head_sum(r * k_mod * rk_ref[...]) * v
    r_o[0] = r
    k_o[0] = k_mod
    v_o[0] = v
    a_o[0] = -kk
    b_o[0] = kk * iclr
    lwf_o[0] = log_decay[:, 0:W]
    lwb_o[0] = log_decay[:, W:2 * W]
    gf_o[0] = g_f
    gb_o[0] = g_b
    bon_o[0] = (g_f + g_b) * bonus


def _rwkv_prepare(u_rw, T, cw, w0cat, w2bd, a0, a2pad, g2f, g2b, k_k, k_a, r_k):
    B, S, C4 = u_rw.shape
    nt = S // TOK_TILE
    rb = TOK_TILE // 8
    const = lambda a: pl.BlockSpec(a.shape, lambda b, j: (0,) * a.ndim)
    consts = (cw, w0cat, w2bd, a0, a2pad, g2f, g2b, k_k, k_a, r_k)
    in_specs = [pl.BlockSpec((1, TOK_TILE, C4), lambda b, j: (b, j, 0)),
                pl.BlockSpec((1, 8, C4), lambda b, j: (b, jnp.maximum(j * rb - 1, 0), 0)),
                pl.BlockSpec((1, 8, C4), lambda b, j: (b, jnp.minimum((j + 1) * rb, S // 8 - 1), 0))]
    in_specs += [const(a) for a in consts]
    seq_spec = pl.BlockSpec((1, TOK_TILE, GROUP_WIDTH), lambda b, j: (b, j, 0))
    lat_spec = pl.BlockSpec((1, TOK_TILE, GROUP_WIDTH), lambda b, j: (b, jnp.maximum(j - 1, 0), 0))
    seq_shape = jax.ShapeDtypeStruct((B, S, GROUP_WIDTH), F32)
    lat_shape = jax.ShapeDtypeStruct((B, T, GROUP_WIDTH), F32)
    return pl.pallas_call(
        functools.partial(_prep_kernel, nt),
        name="rwkv_prepare",
        grid=(B, nt),
        in_specs=in_specs,
        out_specs=[seq_spec] * 7 + [lat_spec] * 3,
        out_shape=[seq_shape] * 7 + [lat_shape] * 3,
        compiler_params=_cparams(("arbitrary", "arbitrary")),
    )(u_rw, u_rw, u_rw, *consts)


def _dplr_local(r, k, v, a, b, lw, reverse):
    C, H, GL = RWKV_CHUNK, HEADS_PER_GROUP, GROUP_LANES
    CS = H * C
    ri = lax.broadcasted_iota(jnp.int32, (C, C), 0)
    ci = lax.broadcasted_iota(jnp.int32, (C, C), 1)
    tri = jnp.where((ci >= ri) if reverse else (ci <= ri), 1.0, 0.0).astype(BF16)
    lc = jnp.dot(jnp.concatenate([tri, tri, tri], axis=1), jnp.concatenate(_split_bf16(lw, 3), axis=0),
                 preferred_element_type=F32)
    yield
    ltot = jnp.sum(lw, axis=0, keepdims=True)
    a_t = a * jnp.exp(lc - lw)
    r_t = r * jnp.exp(lc)
    inv = jnp.exp(-lc)
    b_t, k_t = b * inv, k * inv
    to_end = jnp.exp(ltot - lc)
    b_e, k_e = b * to_end, k * to_end

    rs = lax.broadcasted_iota(jnp.int32, (CS, CS), 0)
    cs = lax.broadcasted_iota(jnp.int32, (CS, CS), 1)
    bd = (rs // C) == (cs // C)
    stack = lambda x: jnp.where(bd, jnp.concatenate([x] * H, axis=0), 0.0)
    unstack = lambda x: x[0:C] + x[C:2 * C] + x[2 * C:3 * C] + x[3 * C:4 * C]
    a_s, r_s, v_s = stack(a_t), stack(r_t), stack(v)
    g = _dot_nt(jnp.concatenate([a_s, r_s], axis=0), jnp.concatenate([b_t, k_t], axis=0))
    g_swapped = pltpu.roll(g, C, 1)
    t128 = lax.broadcasted_iota(jnp.int32, (C, 2 * C), 0)
    l128 = lax.broadcasted_iota(jnp.int32, (C, 2 * C), 1)
    s128, half128 = l128 % C, l128 // C
    incl128 = (s128 >= t128) if reverse else (s128 <= t128)
    strict128 = (s128 > t128) if reverse else (s128 < t128)
    zeros128 = jnp.zeros((C, 2 * C), F32)

    def block_diagonal(row0, from_k, tri):
        blocks = []
        for hh in range(H):
            half = hh % 2
            src = g if half == int(from_k) else g_swapped
            piece = jnp.where(jnp.logical_and(half128 == half, tri), src[row0 + hh * C:row0 + (hh + 1) * C], 0.0)
            blocks.append(jnp.concatenate([piece if c == hh // 2 else zeros128 for c in range(H // 2)], axis=1))
        return jnp.concatenate(blocks, axis=0)

    low = block_diagonal(0, False, strict128)
    ak = block_diagonal(0, True, strict128)
    rb = block_diagonal(CS, False, incl128)
    rk = block_diagonal(CS, True, incl128)
    yield
    akv = _dot(ak, v_s)
    tm = jnp.where(rs == cs, 1.0, 0.0) + low
    pw = _dot(low, low)
    yield
    n = 2
    while n < C // 2:
        both = _dot(jnp.concatenate([pw, tm], axis=0), pw)
        pw, tm = both[0:CS], tm + both[CS:]
        n *= 2
        yield
    tm = tm + _dot(tm, pw)
    yield
    uw = _dot(tm, jnp.concatenate([akv, a_s], axis=1))
    yield
    y0_s = _dot(jnp.concatenate([rb, rk], axis=1), jnp.concatenate([uw[:, 0:GL], v_s], axis=0))
    qa_s = _dot(rb, uw[:, GL:])
    yield
    u0, w = unstack(uw[:, 0:GL]), unstack(uw[:, GL:])
    y0, q_add = unstack(y0_s), unstack(qa_s)
    zeros = jnp.zeros((C, GL), F32)
    stack_t = jnp.concatenate([w, u0, v, zeros], axis=0).T
    rm = jnp.concatenate([jnp.concatenate([b_e, zeros], axis=1),
                          jnp.concatenate([zeros, b_e], axis=1),
                          jnp.concatenate([zeros, k_e], axis=1),
                          jnp.concatenate([zeros, zeros], axis=1)], axis=0)
    mn = _dot(stack_t, rm)
    return (r_t + q_add, y0, jnp.exp(ltot), jnp.where(bd, mn[:, 0:GL], 0.0), jnp.where(bd, mn[:, GL:], 0.0))


def _scan_kernel(rf, kf, vf, af, bf, lwf, gf, rb, kb, vb, ab, bb, lwb, gb, lnw, lnb, of, ob, sf, sb):
    @pl.when(pl.program_id(1) == 0)
    def _():
        sf[...] = jnp.zeros_like(sf)
        sb[...] = jnp.zeros_like(sb)

    C = RWKV_CHUNK
    fwd = (False, (rf, kf, vf, af, bf, lwf), gf, of, sf)
    bwd = (True, (rb, kb, vb, ab, bb, lwb), gb, ob, sb)
    chains = []
    for reverse, refs, g_ref, o_ref, s_ref in (fwd, bwd):
        subs = range(RWKV_CHUNKS_PER_STEP)
        for gi in range(N_GROUPS):
            sl = slice(gi * GROUP_LANES, (gi + 1) * GROUP_LANES)
            for sub in (reversed(subs) if reverse else subs):
                rows = slice(sub * C, (sub + 1) * C)
                chains.append((gi, sl, rows, g_ref, o_ref, s_ref,
                               _dplr_local(*[ref[0, rows, sl] for ref in refs], reverse)))
    local = _round_robin([c[-1] for c in chains])
    states = {}
    for (gi, sl, rows, g_ref, o_ref, s_ref, _), (q, y0, decay, m_t, n_t) in zip(chains, local):
        key = (id(s_ref), gi)
        st = states.get(key)
        if st is None:
            st = s_ref[gi]
        y = _dot_nt(q, st) + y0
        states[key] = st * decay + _dot(st, m_t) + n_t
        dlt = y - _head_sum(y, 1.0 / HEAD_DIM)
        var = _head_sum(dlt * dlt, 1.0 / HEAD_DIM)
        yn = dlt * lax.rsqrt(var + RWKV_GN_EPS) * lnw[:, sl] + lnb[:, sl]
        o_ref[0, rows, sl] = g_ref[0, rows, sl] * yn
    for s_ref in (sf, sb):
        for gi in range(N_GROUPS):
            s_ref[gi] = states[(id(s_ref), gi)]


def _rwkv_scan(r, k, v, a, b, lwf, lwb, gf, gb, lnw, lnb, T):
    B, S, W = r.shape
    C = RWKV_CHUNK * RWKV_CHUNKS_PER_STEP
    n_ctx = CTX_LEN // C
    n_lat = T // C
    ns = n_ctx + n_lat

    def fchunk(s):
        return s

    def bchunk(s):
        return jnp.where(s < n_ctx, n_ctx - 1 - s, ns + n_ctx - 1 - s)

    def fout(s):
        return jnp.maximum(s - n_ctx, 0)

    def bout(s):
        return jnp.where(s < n_ctx, n_lat - 1, ns - 1 - s)

    def spec(chunk):
        return pl.BlockSpec((1, C, W), lambda bi, s: (bi, chunk(s), 0))

    const = pl.BlockSpec((1, W), lambda bi, s: (0, 0))
    in_specs = [spec(fchunk)] * 6 + [spec(fout)] + [spec(bchunk)] * 6 + [spec(bout)] + [const, const]
    return pl.pallas_call(
        _scan_kernel,
        name="rwkv_scan",
        grid=(B, ns),
        in_specs=in_specs,
        out_specs=[spec(fout), spec(bout)],
        out_shape=[jax.ShapeDtypeStruct((B, T, W), F32)] * 2,
        scratch_shapes=[pltpu.VMEM((N_GROUPS, GROUP_LANES, GROUP_LANES), F32)] * 2,
        compiler_params=_cparams(("arbitrary", "arbitrary")),
    )(r, k, v, a, b, lwf, gf, r, k, v, a, b, lwb, gb, lnw, lnb)


def _out_kernel(x_ref, rf_ref, rb_ref, wf_ref, wb_ref, bon_ref, wo_ref, mod_ref, gains_ref, wr_ref,
                x1_ref, h2_ref, aff_ref):
    W = GROUP_WIDTH
    ret = rf_ref[0] + rb_ref[0]
    rwk = wf_ref[0] + wb_ref[0] + bon_ref[0]
    mix = _dot(ret, wo_ref[0:W, :]) + _dot(rwk, wo_ref[W:2 * W, :])
    m = mod_ref[0]
    D = D_MODEL
    ms = jnp.mean(mix * mix, axis=-1, keepdims=True)
    x1 = x_ref[0] + m[:, 2 * D:3 * D] * (mix * lax.rsqrt(ms + NORM_EPS) * gains_ref[1:2, :])
    x1_ref[0] = x1
    ms2 = jnp.mean(x1 * x1, axis=-1, keepdims=True)
    h2 = (x1 * lax.rsqrt(ms2 + NORM_EPS) * gains_ref[2:3, :]) * (1.0 + m[:, 4 * D:5 * D]) + m[:, 3 * D:4 * D]
    h2_ref[0] = h2.astype(BF16)
    h_hi, h_lo = _split_bf16(h2, 2)
    w_hi, w_lo = _split_bf16(wr_ref[...], 2)
    logits = lax.dot_general(jnp.concatenate([w_hi, w_lo, w_hi], axis=1), jnp.concatenate([h_hi, h_hi, h_lo], axis=1),
                             (((1,), (1,)), ((), ())), preferred_element_type=F32)
    e = jnp.exp(logits - jnp.max(logits, axis=0, keepdims=True))
    aff_ref[0] = e / jnp.sum(e, axis=0, keepdims=True)


def _out_projection(x, ret_f, ret_b, rw_f, rw_b, bonus, w_out, mods, gains, wr_pad):
    B, T, D = x.shape
    nt = T // OUT_TILE
    tok = lambda w: pl.BlockSpec((1, OUT_TILE, w), lambda b, i: (b, i, 0))
    const = lambda a: pl.BlockSpec(a.shape, lambda b, i: (0,) * a.ndim)
    return pl.pallas_call(
        _out_kernel,
        name="out_projection",
        grid=(B, nt),
        in_specs=[tok(D)] + [tok(GROUP_WIDTH)] * 5 + [const(w_out),
                  pl.BlockSpec((1, 1, mods.shape[2]), lambda b, i: (b, 0, 0)), const(gains), const(wr_pad)],
        out_specs=[tok(D), tok(D), pl.BlockSpec((1, N_EXPERTS, OUT_TILE), lambda b, i: (b, 0, i))],
        out_shape=[jax.ShapeDtypeStruct((B, T, D), F32), jax.ShapeDtypeStruct((B, T, D), BF16),
                   jax.ShapeDtypeStruct((B, N_EXPERTS, T), F32)],
        compiler_params=_cparams(("arbitrary", "arbitrary")),
    )(x, ret_f, ret_b, rw_f, rw_b, bonus, w_out, mods, gains, wr_pad)


def _cumsum_lanes(x):
    n = x.shape[1]
    lane = lax.broadcasted_iota(jnp.int32, (1, n), 1)
    sh = 1
    while sh < n:
        x = x + jnp.where(lane >= sh, pltpu.roll(x, sh, 1), 0.0)
        sh *= 2
    return x


def _select_kernel(cap, aff_ref, slot_ref, slot_t_ref, bounds_ref):
    aff = aff_ref[0]
    capf = jnp.float32(cap)

    def body(i, thr):
        cand = thr | (jnp.int32(1) << (30 - i))
        cnt = jnp.sum(jnp.where(aff >= pltpu.bitcast(cand, F32), 1.0, 0.0), axis=1, keepdims=True)
        return jnp.where(cnt >= capf, cand, thr)

    thr = lax.fori_loop(0, 31, body, jnp.zeros((aff.shape[0], 1), jnp.int32))
    min_normal_bits = jnp.int32(0x00800000)
    above = pltpu.bitcast(jnp.maximum(thr + 1, min_normal_bits), F32)
    gt = aff >= above
    eq = jnp.where(jnp.logical_and(aff >= pltpu.bitcast(thr, F32), jnp.logical_not(gt)), 1.0, 0.0)
    need = capf - jnp.sum(jnp.where(gt, 1.0, 0.0), axis=1, keepdims=True)
    eq_before = _cumsum_lanes(eq) - eq
    sel = jnp.where(jnp.logical_or(gt, jnp.logical_and(eq > 0.0, eq_before < need)), 1.0, 0.0)
    count = _cumsum_lanes(sel)
    slot = jnp.where(sel > 0.0, count - sel, -1.0)
    slot_ref[0] = slot
    pad = jnp.full((LANES - slot.shape[0], slot.shape[1]), -1.0, F32)
    slot_t_ref[0] = jnp.concatenate([slot, pad], axis=0).T

    n_experts, n_tok = slot.shape
    tok = lax.broadcasted_iota(jnp.int32, (1, n_tok), 1)
    lane = lax.broadcasted_iota(jnp.int32, (1, LANES), 1)
    bounds = jnp.zeros((n_experts, LANES), F32)
    for i in range(1, n_tok // BAND_TILE + 1):
        before = jnp.sum(jnp.where(tok == i * BAND_TILE - 1, count, 0.0), axis=1, keepdims=True)
        bounds = jnp.where(lane == i, before, bounds)
    bounds_ref[0] = bounds


def _select(aff, cap):
    B, E, T = aff.shape
    return pl.pallas_call(
        functools.partial(_select_kernel, cap),
        name="expert_select",
        grid=(B,),
        in_specs=[pl.BlockSpec((1, E, T), lambda b: (b, 0, 0))],
        out_specs=[pl.BlockSpec((1, E, T), lambda b: (b, 0, 0)), pl.BlockSpec((1, T, LANES), lambda b: (b, 0, 0)),
                   pl.BlockSpec((1, E, LANES), lambda b: (b, 0, 0))],
        out_shape=[jax.ShapeDtypeStruct((B, E, T), F32), jax.ShapeDtypeStruct((B, T, LANES), F32),
                   jax.ShapeDtypeStruct((B, E, LANES), F32)],
        compiler_params=_cparams(("arbitrary",)),
    )(aff)


def _expert_kernel(cap, band_ref, h_ref, slot_ref, aff_ref, wg_ref, wu_ref, wd_ref, ye_ref, xs_ref, gate_ref):
    base = (pl.program_id(0) * pl.num_programs(1) + pl.program_id(1)) * BAND_COLS
    xs_ref[...] = jnp.zeros_like(xs_ref)
    gate_ref[...] = jnp.zeros_like(gate_ref)
    jw = lax.broadcasted_iota(jnp.int32, (GATHER_WINDOW, 1), 0)
    for ti in range(h_ref.shape[1] // BAND_TILE):
        first = pl.multiple_of((band_ref[base + ti] >> 3) << 3, 8)
        hit = slot_ref[0, 0, ti:ti + 1, :] == (first + jw).astype(F32)
        xs_ref[pl.ds(first, GATHER_WINDOW), :] += jnp.dot(
            jnp.where(hit, 1.0, 0.0).astype(BF16), h_ref[0, ti * BAND_TILE:(ti + 1) * BAND_TILE, :],
            preferred_element_type=F32)
        gate_ref[pl.ds(first, GATHER_WINDOW), :] += jnp.sum(
            jnp.where(hit, aff_ref[0, 0, ti:ti + 1, :], 0.0), axis=1, keepdims=True)
    xs = xs_ref[0:cap, :].astype(BF16)
    hid = _silu(jnp.dot(xs, wg_ref[0], preferred_element_type=F32)) * jnp.dot(xs, wu_ref[0], preferred_element_type=F32)
    ye = jnp.dot(hid.astype(BF16), wd_ref[0], preferred_element_type=F32) * gate_ref[0:cap, :]
    ye_ref[0, 0] = ye.astype(BF16)


def _expert_ffn(band, h2, slot, aff, wg, wu, wd, cap):
    B, T, D = h2.shape
    E = wg.shape[0]
    nt = T // BAND_TILE
    row = pl.BlockSpec((1, 1, nt, BAND_TILE), lambda b, e, band: (b, e, 0, 0))
    wspec = pl.BlockSpec((1, D, D), lambda b, e, band: (e, 0, 0))
    grid_spec = pltpu.PrefetchScalarGridSpec(
        num_scalar_prefetch=1,
        grid=(B, E),
        in_specs=[pl.BlockSpec((1, T, D), lambda b, e, band: (b, 0, 0)), row, row, wspec, wspec, wspec],
        out_specs=pl.BlockSpec((1, 1, cap, D), lambda b, e, band: (b, e, 0, 0)),
        scratch_shapes=[pltpu.VMEM((cap + GATHER_WINDOW, D), F32), pltpu.VMEM((cap + GATHER_WINDOW, 1), F32)])
    return pl.pallas_call(
        functools.partial(_expert_kernel, cap),
        name="expert_ffn",
        grid_spec=grid_spec,
        out_shape=jax.ShapeDtypeStruct((B, E, cap, D), BF16),
        compiler_params=_cparams(("arbitrary", "arbitrary")),
    )(band, h2, slot.reshape(B, E, nt, BAND_TILE), aff.reshape(B, E, nt, BAND_TILE), wg, wu, wd)


def _combine_kernel(cap, band_ref, x1_ref, slot_t_ref, ye_ref, mod_ref, gain_ref, o_ref, acc_ref):
    win = min(SLOT_WINDOW, cap)
    log_win = win.bit_length() - 1
    pack = 16
    b, i = pl.program_id(0), pl.program_id(1)
    tiles_per = OUT_TILE // BAND_TILE
    st = slot_t_ref[0]
    jw = lax.broadcasted_iota(jnp.int32, (1, win), 1)

    def slots_before(e, tile):
        return band_ref[(b * N_EXPERTS + e) * BAND_COLS + tile * tiles_per]

    def window(e, w):
        first = ((slots_before(e, i) >> 4) << 4) + w * win
        src = pl.multiple_of(jnp.minimum(first, cap - win), pack)
        col = st[:, e:e + 1]
        hit = jnp.logical_and(col == (src + jw).astype(F32), col >= first.astype(F32))
        return jnp.where(hit, 1.0, 0.0).astype(BF16), ye_ref[0, e, pl.ds(src, win), :]

    acc = jnp.zeros((OUT_TILE, D_MODEL), F32)
    for e in range(0, N_EXPERTS, 2):
        (h0, r0), (h1, r1) = window(e, 0), window(e + 1, 0)
        acc = acc + jnp.dot(jnp.concatenate([h0, h1], axis=1), jnp.concatenate([r0, r1], axis=0),
                            preferred_element_type=F32)
    acc_ref[...] = acc
    for e in range(N_EXPERTS):
        first = (slots_before(e, i) >> 4) << 4
        n_win = (slots_before(e, i + 1) - first + win - 1) >> log_win

        def more(w, carry, e=e):
            hit, rows = window(e, w)
            acc_ref[...] += jnp.dot(hit, rows, preferred_element_type=F32)
            return carry

        lax.fori_loop(1, n_win, more, 0)
    acc = acc_ref[...]
    ms = jnp.mean(acc * acc, axis=-1, keepdims=True)
    gt2 = mod_ref[0][:, 5 * D_MODEL:6 * D_MODEL]
    o_ref[0] = x1_ref[0] + gt2 * (acc * lax.rsqrt(ms + NORM_EPS) * gain_ref[...])


def _combine(band, x1, slot_t, ye, mods, gain, cap):
    B, T, D = x1.shape
    E = ye.shape[1]
    grid_spec = pltpu.PrefetchScalarGridSpec(
        num_scalar_prefetch=1,
        grid=(B, T // OUT_TILE),
        in_specs=[pl.BlockSpec((1, OUT_TILE, D), lambda b, i, band: (b, i, 0)),
                  pl.BlockSpec((1, OUT_TILE, LANES), lambda b, i, band: (b, i, 0)),
                  pl.BlockSpec((1, E, cap, D), lambda b, i, band: (b, 0, 0, 0)),
                  pl.BlockSpec((1, 1, mods.shape[2]), lambda b, i, band: (b, 0, 0)),
                  pl.BlockSpec((1, D), lambda b, i, band: (0, 0))],
        out_specs=pl.BlockSpec((1, OUT_TILE, D), lambda b, i, band: (b, i, 0)),
        scratch_shapes=[pltpu.VMEM((OUT_TILE, D), F32)])
    return pl.pallas_call(
        functools.partial(_combine_kernel, cap),
        name="combine",
        grid_spec=grid_spec,
        out_shape=jax.ShapeDtypeStruct((B, T, D), F32),
        compiler_params=_cparams(("arbitrary", "arbitrary")),
    )(band, x1, slot_t, ye, mods, gain)


def _rope_tables(T):
    rows = T // GRID_W
    row = np.repeat(np.arange(rows, dtype=np.float32), GRID_W)
    col = np.tile(np.arange(GRID_W, dtype=np.float32), rows)
    n_freq = HEAD_DIM // 4
    freq = jnp.asarray(ROPE_BASE, F32) ** (-jnp.arange(n_freq, dtype=F32) / n_freq)
    ang = jnp.concatenate([row[:, None] * freq, col[:, None] * freq], axis=-1)
    cos, sin = jnp.cos(ang), jnp.sin(ang)
    reps = LANES // (HEAD_DIM // 2)
    cos_t = jnp.tile(cos, (1, reps))
    sin_t = jnp.tile(jnp.concatenate([-sin, sin], axis=-1), (1, reps // 2))
    return cos_t, sin_t


def kernel(x, c, ctx, c_ctx, w_mod, b_mod, norm_gains, w_in, rwkv_conv, rwkv_w0, rwkv_w2, rwkv_a0, rwkv_a2, rwkv_g2, rwkv_k_k, rwkv_k_a, rwkv_r_k, rwkv_lnx_w, rwkv_lnx_b, w_out, w_router, w_gate, w_up, w_down):
    B, T, D = x.shape
    assert D == D_MODEL and ctx.shape == (B, CTX_LEN, D) and T % OUT_TILE == 0 and w_mod.shape[0] == 1
    W = GROUP_WIDTH
    cap = max(1, EC_CAPACITY * T // N_EXPERTS)
    assert T // BAND_TILE < BAND_COLS

    n_rows = -(-(B + 1) // 8) * 8
    cc = jnp.concatenate([c, c_ctx[None, :], jnp.zeros((n_rows - B - 1, D), F32)], axis=0)
    mods = _modulation(cc, w_mod[0], b_mod).reshape(n_rows, 1, 6 * D)

    split = RET_COLS + 3 * W + 2 * DECAY_LORA + ICLR_LORA
    w_pad = jnp.concatenate([w_in[0][:, :split], jnp.zeros((D, LANES - ICLR_LORA), F32), w_in[0][:, split:]],
                            axis=1).astype(BF16)
    cs = split - RET_COLS
    cw = jnp.concatenate([rwkv_conv[0][:, :cs], jnp.zeros((3, LANES - ICLR_LORA), F32), rwkv_conv[0][:, cs:]], axis=1)
    zl = jnp.zeros((DECAY_LORA, W), F32)
    w2bd = jnp.concatenate([jnp.concatenate([rwkv_w2[0, 0], zl], axis=1),
                            jnp.concatenate([zl, rwkv_w2[0, 1]], axis=1)], axis=0).astype(BF16)
    w0cat = jnp.concatenate([rwkv_w0[0, 0], rwkv_w0[0, 1]])[None, :]
    a2pad = jnp.concatenate([rwkv_a2[0], jnp.zeros((LANES - ICLR_LORA, W), F32)], axis=0).astype(BF16)
    g2f, g2b = rwkv_g2[0, 0].astype(BF16), rwkv_g2[0, 1].astype(BF16)
    row = lambda a: a[0][None, :]
    cos_t, sin_t = _rope_tables(T)

    u_ret, u_rw = _projection(x, ctx, mods, norm_gains[0, 0][None, :], w_pad, cos_t, sin_t)
    ret_f, ret_b = _retention(u_ret, T)
    r, k, v, a, b, lwf, lwb, gf, gb, bonus = _rwkv_prepare(
        u_rw, T, cw, w0cat, w2bd, row(rwkv_a0), a2pad, g2f, g2b, row(rwkv_k_k), row(rwkv_k_a), row(rwkv_r_k))
    rw_f, rw_b = _rwkv_scan(r, k, v, a, b, lwf, lwb, gf, gb, row(rwkv_lnx_w), row(rwkv_lnx_b), T)

    x1, h2, aff = _out_projection(x, ret_f, ret_b, rw_f, rw_b, bonus, w_out[0].astype(BF16), mods,
                                  norm_gains[0], w_router[0].T)
    slot, slot_t, bounds = _select(aff, cap)
    band = bounds[:, :, :BAND_COLS].astype(jnp.int32).reshape(-1)
    ye = _expert_ffn(band, h2, slot, aff, w_gate[0].astype(BF16), w_up[0].astype(BF16), w_down[0].astype(BF16), cap)
    return _combine(band, x1, slot_t, ye, mods, norm_gains[0, 3][None, :], cap)
```

```python
import functools

import numpy as np
import jax
import jax.numpy as jnp
from jax import lax
from jax.experimental import pallas as pl
from jax.experimental.pallas import tpu as pltpu

F32 = jnp.float32
BF16 = jnp.bfloat16
HIGHEST = lax.Precision.HIGHEST

D_MODEL = 1024
CTX_LEN = 256
GRID_W = 64
HEAD_DIM = 64
GROUP_WIDTH = 512
LANES = 128
N_PAIRS = GROUP_WIDTH // LANES
HEADS_PER_GROUP = 4
GROUP_LANES = HEADS_PER_GROUP * HEAD_DIM
N_GROUPS = GROUP_WIDTH // GROUP_LANES
RET_COLS = 5 * GROUP_WIDTH
RWKV_COLS_PAD = 4 * GROUP_WIDTH
DECAY_LORA = 64
ICLR_LORA = 64
GATE_LORA = 128
N_EXPERTS = 16
EC_CAPACITY = 2
ROPE_BASE = 10000.0
NORM_EPS = 1e-6
RWKV_GN_EPS = 64e-5

TOK_TILE = 256
RET_CHUNK = 256
RWKV_CHUNK = 64
RWKV_CHUNKS_PER_STEP = 4
OUT_TILE = 512
BAND_TILE = 256
GATHER_WINDOW = BAND_TILE + 16
SLOT_WINDOW = 128
BAND_COLS = 32
VMEM_LIMIT = 56 * 1024 * 1024


def _cparams(sem):
    return pltpu.CompilerParams(dimension_semantics=sem, vmem_limit_bytes=VMEM_LIMIT)


def _dot(a, b):
    return jnp.dot(a.astype(BF16), b.astype(BF16), preferred_element_type=F32)


def _dot_nt(a, b):
    return lax.dot_general(a.astype(BF16), b.astype(BF16), (((1,), (1,)), ((), ())),
                           preferred_element_type=F32)


def _dot_exact(a, b):
    return jnp.dot(a, b, precision=HIGHEST, preferred_element_type=F32)


def _head_block_mask(n):
    r = lax.broadcasted_iota(jnp.int32, (n, n), 0)
    c = lax.broadcasted_iota(jnp.int32, (n, n), 1)
    return (r // HEAD_DIM) == (c // HEAD_DIM)


def _split_bf16(x, terms):
    out = []
    for _ in range(terms - 1):
        hi = x.astype(BF16)
        out.append(hi)
        x = x - hi.astype(F32)
    out.append(x.astype(BF16))
    return out


def _head_sum(x, scale=1.0):
    cols = []
    for g in range(x.shape[1] // GROUP_LANES):
        m = jnp.where(_head_block_mask(GROUP_LANES), scale, 0.0).astype(BF16)
        parts = _split_bf16(x[:, g * GROUP_LANES:(g + 1) * GROUP_LANES], 2)
        cols.append(jnp.dot(jnp.concatenate(parts, axis=1), jnp.concatenate([m, m], axis=0),
                            preferred_element_type=F32))
    return cols[0] if len(cols) == 1 else jnp.concatenate(cols, axis=1)


def _silu(x):
    return x * jax.nn.sigmoid(x)


def _round_robin(gens):
    results = [None] * len(gens)
    active = list(range(len(gens)))
    while active:
        for i in list(active):
            try:
                next(gens[i])
            except StopIteration as done:
                results[i] = done.value
                active.remove(i)
    return results


def _mod_kernel(c_ref, w_ref, b_ref, o_ref):
    o_ref[...] = _dot_exact(_silu(c_ref[...]), w_ref[...]) + b_ref[...]


def _modulation(cc, w_mod, b_mod):
    rows, d = cc.shape
    n = w_mod.shape[1]
    tn = 1536
    return pl.pallas_call(
        _mod_kernel,
        name="modulation",
        grid=(n // tn,),
        in_specs=[pl.BlockSpec((rows, d), lambda i: (0, 0)),
                  pl.BlockSpec((d, tn), lambda i: (0, i)),
                  pl.BlockSpec((1, tn), lambda i: (0, i))],
        out_specs=pl.BlockSpec((rows, tn), lambda i: (0, i)),
        out_shape=jax.ShapeDtypeStruct((rows, n), F32),
        compiler_params=_cparams(("arbitrary",)),
    )(cc, w_mod, b_mod)


def _proj_kernel(n_tiles, x_ref, ctx_ref, xp_ref, xn_ref, mod_ref, gain_ref, w_ref, cos_ref, sin_ref, *rest):
    feature_consts, ret_ref, feature_outs = rest[:10], rest[10], rest[11:]
    j = pl.program_id(1)
    is_lat = j > 0
    xin = jnp.concatenate([jnp.where(is_lat, x_ref[0], ctx_ref[0]), xp_ref[0], xn_ref[0]], axis=0)
    ms = jnp.mean(xin * xin, axis=-1, keepdims=True)
    y = xin * lax.rsqrt(ms + NORM_EPS) * gain_ref[...]
    m = mod_ref[0]
    h = (y * (1.0 + m[:, D_MODEL:2 * D_MODEL]) + m[:, 0:D_MODEL]).astype(BF16)
    u_rw = jnp.dot(h, w_ref[:, RET_COLS:], preferred_element_type=F32)
    u = jnp.dot(h[0:TOK_TILE], w_ref[:, 0:RET_COLS], preferred_element_type=F32)
    has_prev = (j >= 2).astype(F32)
    has_next = jnp.logical_and(j >= 1, j <= n_tiles - 2).astype(F32)
    _rwkv_features(u_rw[0:TOK_TILE], u_rw[TOK_TILE + 7:TOK_TILE + 8] * has_prev,
                   u_rw[TOK_TILE + 8:TOK_TILE + 9] * has_next, *feature_consts, *feature_outs)
    lat = is_lat.astype(F32)
    cos = cos_ref[...] * lat + (1.0 - lat)
    sin = sin_ref[...] * lat
    lane = lax.broadcasted_iota(jnp.int32, (1, LANES), 1)
    first_half = (lane % HEAD_DIM) < (HEAD_DIM // 2)
    for base, scale in ((0, 1.0), (GROUP_WIDTH, HEAD_DIM ** -0.5)):
        for g in range(N_PAIRS):
            lo = base + g * LANES
            t = u[:, lo:lo + LANES] * scale
            sw = jnp.where(first_half, pltpu.roll(t, LANES - HEAD_DIM // 2, 1), pltpu.roll(t, HEAD_DIM // 2, 1))
            ret_ref[0, :, lo:lo + LANES] = t * cos + sw * sin
    ret_ref[0, :, 2 * GROUP_WIDTH:] = u[:, 2 * GROUP_WIDTH:RET_COLS]


def _projection(x, ctx, mods, gain, w_pad, cos_t, sin_t, feature_consts):
    B, T, D = x.shape
    nt = (T + CTX_LEN) // TOK_TILE
    S = T + CTX_LEN
    rb = TOK_TILE // 8
    const = lambda a: pl.BlockSpec(a.shape, lambda b, j: (0,) * a.ndim)
    seq_spec = pl.BlockSpec((1, TOK_TILE, GROUP_WIDTH), lambda b, j: (b, j, 0))
    lat_spec = pl.BlockSpec((1, TOK_TILE, GROUP_WIDTH), lambda b, j: (b, jnp.maximum(j - 1, 0), 0))
    seq_shape = jax.ShapeDtypeStruct((B, S, GROUP_WIDTH), F32)
    lat_shape = jax.ShapeDtypeStruct((B, T, GROUP_WIDTH), F32)
    return pl.pallas_call(
        functools.partial(_proj_kernel, nt),
        name="projection",
        grid=(B, nt),
        in_specs=[
            pl.BlockSpec((1, TOK_TILE, D), lambda b, j: (b, jnp.maximum(j - 1, 0), 0)),
            pl.BlockSpec((1, TOK_TILE, D), lambda b, j: (b, 0, 0)),
            pl.BlockSpec((1, 8, D), lambda b, j: (b, jnp.maximum((j - 1) * rb - 1, 0), 0)),
            pl.BlockSpec((1, 8, D), lambda b, j: (b, jnp.clip(j * rb, 0, T // 8 - 1), 0)),
            pl.BlockSpec((1, 1, mods.shape[2]), lambda b, j: (jnp.where(j == 0, B, b), 0, 0)),
            pl.BlockSpec((1, D), lambda b, j: (0, 0)),
            const(w_pad),
            pl.BlockSpec((TOK_TILE, LANES), lambda b, j: (jnp.maximum(j - 1, 0), 0)),
            pl.BlockSpec((TOK_TILE, LANES), lambda b, j: (jnp.maximum(j - 1, 0), 0)),
        ] + [const(a) for a in feature_consts],
        out_specs=[pl.BlockSpec((1, TOK_TILE, RET_COLS), lambda b, j: (b, j, 0))] + [seq_spec] * 7 + [lat_spec] * 3,
        out_shape=[jax.ShapeDtypeStruct((B, S, RET_COLS), F32)] + [seq_shape] * 7 + [lat_shape] * 3,
        compiler_params=_cparams(("arbitrary", "arbitrary")),
    )(x, ctx, x, x, mods, gain, w_pad, cos_t, sin_t, *feature_consts)


def _ret_group(d, gi, q_ref, k_ref, v_ref, g_ref, o_ref, s_ref, dmask_ref, qdec_ref, kdec_ref, cdec_ref):
    C = RET_CHUNK
    lane_head = lax.broadcasted_iota(jnp.int32, (1, GROUP_LANES), 1) // HEAD_DIM
    sl = slice(gi * GROUP_LANES, (gi + 1) * GROUP_LANES)
    q4, k4, v4 = q_ref[0, :, sl], k_ref[0, :, sl], v_ref[0, :, sl]
    state = s_ref[gi]
    q_s = jnp.concatenate([jnp.where(lane_head == hh, q4, 0.0) for hh in range(HEADS_PER_GROUP)], axis=0)
    scores = _dot_nt(q_s, k4) * dmask_ref[d, gi]
    cross = _dot(q4 * qdec_ref[d, :, sl], state)
    yield
    o_s = _dot(scores, v4)
    kd = (k4 * kdec_ref[d, :, sl]).T
    s_ref[gi] = state * cdec_ref[:, sl] + jnp.where(_head_block_mask(GROUP_LANES), _dot(kd, v4), 0.0)
    yield
    y = cross + jnp.where(lane_head == 0, o_s[0:C], 0.0)
    for hh in range(1, HEADS_PER_GROUP):
        y = y + jnp.where(lane_head == hh, o_s[hh * C:(hh + 1) * C], 0.0)
    ms = _head_sum(y * y, 1.0 / HEAD_DIM)
    o_ref[0, :, sl] = _silu(g_ref[0, :, sl]) * (y * lax.rsqrt(ms + NORM_EPS))


def _ret_kernel(qf, kf, vf, gf, qb, kb, vb, gb, dmask, qdec, kdec, cdec, of, ob, sf, sb):
    @pl.when(pl.program_id(1) == 0)
    def _():
        sf[...] = jnp.zeros_like(sf)
        sb[...] = jnp.zeros_like(sb)

    tables = (dmask, qdec, kdec, cdec)
    _round_robin([_ret_group(d, gi, *refs, *tables)
                  for d, refs in ((0, (qf, kf, vf, gf, of, sf)), (1, (qb, kb, vb, gb, ob, sb)))
                  for gi in range(N_GROUPS)])


def _retention_tables():
    n_heads = GROUP_WIDTH // HEAD_DIM
    log_gamma = np.log1p(-np.exp2(-5.0 - np.arange(n_heads, dtype=np.float64)))
    idx = np.arange(RET_CHUNK, dtype=np.float64)
    rel = idx[:, None] - idx[None, :]
    dm = np.where(rel >= 0, np.exp(np.maximum(rel, 0.0) * log_gamma[:, None, None]), 0.0)
    dmask = np.stack([dm, dm.transpose(0, 2, 1)])
    dmask = dmask.reshape(2, N_GROUPS, HEADS_PER_GROUP * RET_CHUNK, RET_CHUNK)
    pos = np.stack([idx, RET_CHUNK - 1.0 - idx])
    lg = np.repeat(log_gamma, HEAD_DIM)
    qdec = np.exp((pos[:, :, None] + 1.0) * lg)
    kdec = np.exp((RET_CHUNK - 1.0 - pos[:, :, None]) * lg)
    cdec = np.exp(RET_CHUNK * lg)[None, :]
    return [jnp.asarray(a, F32) for a in (dmask, qdec, kdec, cdec)]


def _retention(u_ret, T):
    B = u_ret.shape[0]
    n_lat = T // RET_CHUNK
    ns = n_lat + 1
    dmask, qdec, kdec, cdec = _retention_tables()

    def fchunk(s):
        return s

    def bchunk(s):
        return jnp.where(s == 0, 0, ns - s)

    def col(c, chunk):
        return pl.BlockSpec((1, RET_CHUNK, GROUP_WIDTH), lambda b, s: (b, chunk(s), c))

    const = lambda shape: pl.BlockSpec(shape, lambda b, s: (0,) * len(shape))
    in_specs = ([col(c, fchunk) for c in (0, 1, 2, 3)] + [col(c, bchunk) for c in (0, 1, 2, 4)]
                + [const(dmask.shape), const(qdec.shape), const(kdec.shape), const(cdec.shape)])
    out_specs = [pl.BlockSpec((1, RET_CHUNK, GROUP_WIDTH), lambda b, s: (b, jnp.maximum(s - 1, 0), 0)),
                 pl.BlockSpec((1, RET_CHUNK, GROUP_WIDTH), lambda b, s: (b, jnp.where(s == 0, n_lat - 1, n_lat - s), 0))]
    out_shape = [jax.ShapeDtypeStruct((B, T, GROUP_WIDTH), F32)] * 2
    return pl.pallas_call(
        _ret_kernel,
        name="retention",
        grid=(B, ns),
        in_specs=in_specs,
        out_specs=out_specs,
        out_shape=out_shape,
        scratch_shapes=[pltpu.VMEM((N_GROUPS, GROUP_LANES, GROUP_LANES), F32)] * 2,
        compiler_params=_cparams(("arbitrary", "arbitrary")),
    )(*([u_ret] * 8), dmask, qdec, kdec, cdec)


def _rwkv_features(x, prev_row, next_row, cw_ref, w0_ref, w2_ref, a0_ref, a2_ref, g2f_ref, g2b_ref,
                   kk_ref, ka_ref, rk_ref,
                   r_o, k_o, v_o, a_o, b_o, lwf_o, lwb_o, gf_o, gb_o, bon_o):
    rows = lax.broadcasted_iota(jnp.int32, (TOK_TILE, 1), 0)
    xm = jnp.where(rows == 0, prev_row, pltpu.roll(x, 1, 0))
    xp = jnp.where(rows == TOK_TILE - 1, next_row, pltpu.roll(x, TOK_TILE - 1, 0))
    rw = cw_ref[0:1, :] * xm + cw_ref[1:2, :] * x + cw_ref[2:3, :] * xp

    W = GROUP_WIDTH
    r, k, v, lo = rw[:, 0:W], rw[:, W:2 * W], rw[:, 2 * W:3 * W], rw[:, 3 * W:4 * W]
    z = w0_ref[...] + _dot(jnp.tanh(lo[:, 0:LANES]), w2_ref[...])
    log_decay = -np.exp(-0.5).astype(np.float32) * jax.nn.sigmoid(z)
    iclr = jax.nn.sigmoid(a0_ref[...] + _dot(lo[:, LANES:2 * LANES], a2_ref[...]))
    g_f = _dot(jax.nn.sigmoid(lo[:, 2 * LANES:3 * LANES]), g2f_ref[...])
    g_b = _dot(jax.nn.sigmoid(lo[:, 3 * LANES:4 * LANES]), g2b_ref[...])
    kk = k * kk_ref[...]
    kk = kk / jnp.maximum(jnp.sqrt(_head_sum(kk * kk)), 1e-12)
    k_mod = k * (1.0 + (iclr - 1.0) * ka_ref[...])
    bonus = _head_sum(r * k_mod * rk_ref[...]) * v
    r_o[0] = r
    k_o[0] = k_mod
    v_o[0] = v
    a_o[0] = -kk
    b_o[0] = kk * iclr
    lwf_o[0] = log_decay[:, 0:W]
    lwb_o[0] = log_decay[:, W:2 * W]
    gf_o[0] = g_f
    gb_o[0] = g_b
    bon_o[0] = (g_f + g_b) * bonus


def _dplr_local(r, k, v, a, b, lw, reverse):
    C, H, GL = RWKV_CHUNK, HEADS_PER_GROUP, GROUP_LANES
    CS = H * C
    ri = lax.broadcasted_iota(jnp.int32, (C, C), 0)
    ci = lax.broadcasted_iota(jnp.int32, (C, C), 1)
    tri = jnp.where((ci >= ri) if reverse else (ci <= ri), 1.0, 0.0).astype(BF16)
    lc = jnp.dot(jnp.concatenate([tri, tri, tri], axis=1), jnp.concatenate(_split_bf16(lw, 3), axis=0),
                 preferred_element_type=F32)
    yield
    ltot = jnp.sum(lw, axis=0, keepdims=True)
    a_t = a * jnp.exp(lc - lw)
    r_t = r * jnp.exp(lc)
    inv = jnp.exp(-lc)
    b_t, k_t = b * inv, k * inv
    to_end = jnp.exp(ltot - lc)
    b_e, k_e = b * to_end, k * to_end

    rs = lax.broadcasted_iota(jnp.int32, (CS, CS), 0)
    cs = lax.broadcasted_iota(jnp.int32, (CS, CS), 1)
    bd = (rs // C) == (cs // C)
    stack = lambda x: jnp.where(bd, jnp.concatenate([x] * H, axis=0), 0.0)
    unstack = lambda x: x[0:C] + x[C:2 * C] + x[2 * C:3 * C] + x[3 * C:4 * C]
    a_s, r_s, v_s = stack(a_t), stack(r_t), stack(v)
    g = _dot_nt(jnp.concatenate([a_s, r_s], axis=0), jnp.concatenate([b_t, k_t], axis=0))
    g_swapped = pltpu.roll(g, C, 1)
    t128 = lax.broadcasted_iota(jnp.int32, (C, 2 * C), 0)
    l128 = lax.broadcasted_iota(jnp.int32, (C, 2 * C), 1)
    s128, half128 = l128 % C, l128 // C
    incl128 = (s128 >= t128) if reverse else (s128 <= t128)
    strict128 = (s128 > t128) if reverse else (s128 < t128)
    zeros128 = jnp.zeros((C, 2 * C), F32)

    def block_diagonal(row0, from_k, tri):
        blocks = []
        for hh in range(H):
            half = hh % 2
            src = g if half == int(from_k) else g_swapped
            piece = jnp.where(jnp.logical_and(half128 == half, tri), src[row0 + hh * C:row0 + (hh + 1) * C], 0.0)
            blocks.append(jnp.concatenate([piece if c == hh // 2 else zeros128 for c in range(H // 2)], axis=1))
        return jnp.concatenate(blocks, axis=0)

    low = block_diagonal(0, False, strict128)
    ak = block_diagonal(0, True, strict128)
    rb = block_diagonal(CS, False, incl128)
    rk = block_diagonal(CS, True, incl128)
    yield
    akv = _dot(ak, v_s)
    tm = jnp.where(rs == cs, 1.0, 0.0) + low
    pw = _dot(low, low)
    yield
    n = 2
    while n < C // 2:
        both = _dot(jnp.concatenate([pw, tm], axis=0), pw)
        pw, tm = both[0:CS], tm + both[CS:]
        n *= 2
        yield
    tm = tm + _dot(tm, pw)
    yield
    uw = _dot(tm, jnp.concatenate([akv, a_s], axis=1))
    yield
    y0_s = _dot(jnp.concatenate([rb, rk], axis=1), jnp.concatenate([uw[:, 0:GL], v_s], axis=0))
    qa_s = _dot(rb, uw[:, GL:])
    yield
    u0, w = unstack(uw[:, 0:GL]), unstack(uw[:, GL:])
    y0, q_add = unstack(y0_s), unstack(qa_s)
    zeros = jnp.zeros((C, GL), F32)
    stack_t = jnp.concatenate([w, u0, v, zeros], axis=0).T
    rm = jnp.concatenate([jnp.concatenate([b_e, zeros], axis=1),
                          jnp.concatenate([zeros, b_e], axis=1),
                          jnp.concatenate([zeros, k_e], axis=1),
                          jnp.concatenate([zeros, zeros], axis=1)], axis=0)
    mn = _dot(stack_t, rm)
    return (r_t + q_add, y0, jnp.exp(ltot), jnp.where(bd, mn[:, 0:GL], 0.0), jnp.where(bd, mn[:, GL:], 0.0))


def _scan_kernel(rf, kf, vf, af, bf, lwf, gf, rb, kb, vb, ab, bb, lwb, gb, lnw, lnb, of, ob, sf, sb):
    @pl.when(pl.program_id(1) == 0)
    def _():
        sf[...] = jnp.zeros_like(sf)
        sb[...] = jnp.zeros_like(sb)

    C = RWKV_CHUNK
    fwd = (False, (rf, kf, vf, af, bf, lwf), gf, of, sf)
    bwd = (True, (rb, kb, vb, ab, bb, lwb), gb, ob, sb)
    chains = []
    for reverse, refs, g_ref, o_ref, s_ref in (fwd, bwd):
        subs = range(RWKV_CHUNKS_PER_STEP)
        for gi in range(N_GROUPS):
            sl = slice(gi * GROUP_LANES, (gi + 1) * GROUP_LANES)
            for sub in (reversed(subs) if reverse else subs):
                rows = slice(sub * C, (sub + 1) * C)
                chains.append((gi, sl, rows, g_ref, o_ref, s_ref,
                               _dplr_local(*[ref[0, rows, sl] for ref in refs], reverse)))
    local = _round_robin([c[-1] for c in chains])
    states = {}
    for (gi, sl, rows, g_ref, o_ref, s_ref, _), (q, y0, decay, m_t, n_t) in zip(chains, local):
        key = (id(s_ref), gi)
        st = states.get(key)
        if st is None:
            st = s_ref[gi]
        y = _dot_nt(q, st) + y0
        states[key] = st * decay + _dot(st, m_t) + n_t
        dlt = y - _head_sum(y, 1.0 / HEAD_DIM)
        var = _head_sum(dlt * dlt, 1.0 / HEAD_DIM)
        yn = dlt * lax.rsqrt(var + RWKV_GN_EPS) * lnw[:, sl] + lnb[:, sl]
        o_ref[0, rows, sl] = g_ref[0, rows, sl] * yn
    for s_ref in (sf, sb):
        for gi in range(N_GROUPS):
            s_ref[gi] = states[(id(s_ref), gi)]


def _rwkv_scan(r, k, v, a, b, lwf, lwb, gf, gb, lnw, lnb, T):
    B, S, W = r.shape
    C = RWKV_CHUNK * RWKV_CHUNKS_PER_STEP
    n_ctx = CTX_LEN // C
    n_lat = T // C
    ns = n_ctx + n_lat

    def fchunk(s):
        return s

    def bchunk(s):
        return jnp.where(s < n_ctx, n_ctx - 1 - s, ns + n_ctx - 1 - s)

    def fout(s):
        return jnp.maximum(s - n_ctx, 0)

    def bout(s):
        return jnp.where(s < n_ctx, n_lat - 1, ns - 1 - s)

    def spec(chunk):
        return pl.BlockSpec((1, C, W), lambda bi, s: (bi, chunk(s), 0))

    const = pl.BlockSpec((1, W), lambda bi, s: (0, 0))
    in_specs = [spec(fchunk)] * 6 + [spec(fout)] + [spec(bchunk)] * 6 + [spec(bout)] + [const, const]
    return pl.pallas_call(
        _scan_kernel,
        name="rwkv_scan",
        grid=(B, ns),
        in_specs=in_specs,
        out_specs=[spec(fout), spec(bout)],
        out_shape=[jax.ShapeDtypeStruct((B, T, W), F32)] * 2,
        scratch_shapes=[pltpu.VMEM((N_GROUPS, GROUP_LANES, GROUP_LANES), F32)] * 2,
        compiler_params=_cparams(("arbitrary", "arbitrary")),
    )(r, k, v, a, b, lwf, gf, r, k, v, a, b, lwb, gb, lnw, lnb)


def _out_kernel(x_ref, rf_ref, rb_ref, wf_ref, wb_ref, bon_ref, wo_ref, mod_ref, gains_ref, wr_ref,
                x1_ref, h2_ref, aff_ref):
    W = GROUP_WIDTH
    ret = rf_ref[0] + rb_ref[0]
    rwk = wf_ref[0] + wb_ref[0] + bon_ref[0]
    mix = _dot(ret, wo_ref[0:W, :]) + _dot(rwk, wo_ref[W:2 * W, :])
    m = mod_ref[0]
    D = D_MODEL
    ms = jnp.mean(mix * mix, axis=-1, keepdims=True)
    x1 = x_ref[0] + m[:, 2 * D:3 * D] * (mix * lax.rsqrt(ms + NORM_EPS) * gains_ref[1:2, :])
    x1_ref[0] = x1
    ms2 = jnp.mean(x1 * x1, axis=-1, keepdims=True)
    h2 = (x1 * lax.rsqrt(ms2 + NORM_EPS) * gains_ref[2:3, :]) * (1.0 + m[:, 4 * D:5 * D]) + m[:, 3 * D:4 * D]
    h2_ref[0] = h2.astype(BF16)
    h_hi, h_lo = _split_bf16(h2, 2)
    w_hi, w_lo = _split_bf16(wr_ref[...], 2)
    logits = lax.dot_general(jnp.concatenate([w_hi, w_lo, w_hi], axis=1), jnp.concatenate([h_hi, h_hi, h_lo], axis=1),
                             (((1,), (1,)), ((), ())), preferred_element_type=F32)
    e = jnp.exp(logits - jnp.max(logits, axis=0, keepdims=True))
    aff_ref[0] = e / jnp.sum(e, axis=0, keepdims=True)


def _out_projection(x, ret_f, ret_b, rw_f, rw_b, bonus, w_out, mods, gains, wr_pad):
    B, T, D = x.shape
    nt = T // OUT_TILE
    tok = lambda w: pl.BlockSpec((1, OUT_TILE, w), lambda b, i: (b, i, 0))
    const = lambda a: pl.BlockSpec(a.shape, lambda b, i: (0,) * a.ndim)
    return pl.pallas_call(
        _out_kernel,
        name="out_projection",
        grid=(B, nt),
        in_specs=[tok(D)] + [tok(GROUP_WIDTH)] * 5 + [const(w_out),
                  pl.BlockSpec((1, 1, mods.shape[2]), lambda b, i: (b, 0, 0)), const(gains), const(wr_pad)],
        out_specs=[tok(D), tok(D), pl.BlockSpec((1, N_EXPERTS, OUT_TILE), lambda b, i: (b, 0, i))],
        out_shape=[jax.ShapeDtypeStruct((B, T, D), F32), jax.ShapeDtypeStruct((B, T, D), BF16),
                   jax.ShapeDtypeStruct((B, N_EXPERTS, T), F32)],
        compiler_params=_cparams(("arbitrary", "arbitrary")),
    )(x, ret_f, ret_b, rw_f, rw_b, bonus, w_out, mods, gains, wr_pad)


def _cumsum_lanes(x):
    n = x.shape[1]
    lane = lax.broadcasted_iota(jnp.int32, (1, n), 1)
    sh = 1
    while sh < n:
        x = x + jnp.where(lane >= sh, pltpu.roll(x, sh, 1), 0.0)
        sh *= 2
    return x


def _select_kernel(cap, aff_ref, slot_ref, slot_t_ref, bounds_ref):
    aff = aff_ref[0]
    capf = jnp.float32(cap)

    def body(i, thr):
        cand = thr | (jnp.int32(1) << (30 - i))
        cnt = jnp.sum(jnp.where(aff >= pltpu.bitcast(cand, F32), 1.0, 0.0), axis=1, keepdims=True)
        return jnp.where(cnt >= capf, cand, thr)

    thr = lax.fori_loop(0, 31, body, jnp.zeros((aff.shape[0], 1), jnp.int32))
    min_normal_bits = jnp.int32(0x00800000)
    above = pltpu.bitcast(jnp.maximum(thr + 1, min_normal_bits), F32)
    gt = aff >= above
    eq = jnp.where(jnp.logical_and(aff >= pltpu.bitcast(thr, F32), jnp.logical_not(gt)), 1.0, 0.0)
    need = capf - jnp.sum(jnp.where(gt, 1.0, 0.0), axis=1, keepdims=True)
    eq_before = _cumsum_lanes(eq) - eq
    sel = jnp.where(jnp.logical_or(gt, jnp.logical_and(eq > 0.0, eq_before < need)), 1.0, 0.0)
    count = _cumsum_lanes(sel)
    slot = jnp.where(sel > 0.0, count - sel, -1.0)
    slot_ref[0] = slot
    pad = jnp.full((LANES - slot.shape[0], slot.shape[1]), -1.0, F32)
    slot_t_ref[0] = jnp.concatenate([slot, pad], axis=0).T

    n_experts, n_tok = slot.shape
    tok = lax.broadcasted_iota(jnp.int32, (1, n_tok), 1)
    lane = lax.broadcasted_iota(jnp.int32, (1, LANES), 1)
    bounds = jnp.zeros((n_experts, LANES), F32)
    for i in range(1, n_tok // BAND_TILE + 1):
        before = jnp.sum(jnp.where(tok == i * BAND_TILE - 1, count, 0.0), axis=1, keepdims=True)
        bounds = jnp.where(lane == i, before, bounds)
    bounds_ref[0] = bounds


def _select(aff, cap):
    B, E, T = aff.shape
    return pl.pallas_call(
        functools.partial(_select_kernel, cap),
        name="expert_select",
        grid=(B,),
        in_specs=[pl.BlockSpec((1, E, T), lambda b: (b, 0, 0))],
        out_specs=[pl.BlockSpec((1, E, T), lambda b: (b, 0, 0)), pl.BlockSpec((1, T, LANES), lambda b: (b, 0, 0)),
                   pl.BlockSpec((1, E, LANES), lambda b: (b, 0, 0))],
        out_shape=[jax.ShapeDtypeStruct((B, E, T), F32), jax.ShapeDtypeStruct((B, T, LANES), F32),
                   jax.ShapeDtypeStruct((B, E, LANES), F32)],
        compiler_params=_cparams(("arbitrary",)),
    )(aff)


def _expert_kernel(cap, band_ref, h_ref, slot_ref, aff_ref, wg_ref, wu_ref, wd_ref, ye_ref, xs_ref, gate_ref):
    base = (pl.program_id(0) * pl.num_programs(1) + pl.program_id(1)) * BAND_COLS
    xs_ref[...] = jnp.zeros_like(xs_ref)
    gate_ref[...] = jnp.zeros_like(gate_ref)
    jw = lax.broadcasted_iota(jnp.int32, (GATHER_WINDOW, 1), 0)
    for ti in range(h_ref.shape[1] // BAND_TILE):
        first = pl.multiple_of((band_ref[base + ti] >> 3) << 3, 8)
        hit = slot_ref[0, 0, ti:ti + 1, :] == (first + jw).astype(F32)
        xs_ref[pl.ds(first, GATHER_WINDOW), :] += jnp.dot(
            jnp.where(hit, 1.0, 0.0).astype(BF16), h_ref[0, ti * BAND_TILE:(ti + 1) * BAND_TILE, :],
            preferred_element_type=F32)
        gate_ref[pl.ds(first, GATHER_WINDOW), :] += jnp.sum(
            jnp.where(hit, aff_ref[0, 0, ti:ti + 1, :], 0.0), axis=1, keepdims=True)
    xs = xs_ref[0:cap, :].astype(BF16)
    hid = _silu(_dot(xs, wg_ref[0])) * _dot(xs, wu_ref[0])
    ye = _dot(hid, wd_ref[0]) * gate_ref[0:cap, :]
    ye_ref[0, 0] = ye.astype(BF16)


def _expert_ffn(band, h2, slot, aff, wg, wu, wd, cap):
    B, T, D = h2.shape
    E = wg.shape[0]
    nt = T // BAND_TILE
    row = pl.BlockSpec((1, 1, nt, BAND_TILE), lambda b, e, band: (b, e, 0, 0))
    wspec = pl.BlockSpec((1, D, D), lambda b, e, band: (e, 0, 0))
    grid_spec = pltpu.PrefetchScalarGridSpec(
        num_scalar_prefetch=1,
        grid=(B, E),
        in_specs=[pl.BlockSpec((1, T, D), lambda b, e, band: (b, 0, 0)), row, row, wspec, wspec, wspec],
        out_specs=pl.BlockSpec((1, 1, cap, D), lambda b, e, band: (b, e, 0, 0)),
        scratch_shapes=[pltpu.VMEM((cap + GATHER_WINDOW, D), F32), pltpu.VMEM((cap + GATHER_WINDOW, 1), F32)])
    return pl.pallas_call(
        functools.partial(_expert_kernel, cap),
        name="expert_ffn",
        grid_spec=grid_spec,
        out_shape=jax.ShapeDtypeStruct((B, E, cap, D), BF16),
        compiler_params=_cparams(("arbitrary", "arbitrary")),
    )(band, h2, slot.reshape(B, E, nt, BAND_TILE), aff.reshape(B, E, nt, BAND_TILE), wg, wu, wd)


def _combine_kernel(cap, band_ref, x1_ref, slot_t_ref, ye_ref, mod_ref, gain_ref, o_ref, acc_ref):
    win = min(SLOT_WINDOW, cap)
    log_win = win.bit_length() - 1
    pack = 16
    b, i = pl.program_id(0), pl.program_id(1)
    tiles_per = OUT_TILE // BAND_TILE
    st = slot_t_ref[0]
    jw = lax.broadcasted_iota(jnp.int32, (1, win), 1)

    def slots_before(e, tile):
        return band_ref[(b * N_EXPERTS + e) * BAND_COLS + tile * tiles_per]

    def window(e, w):
        first = ((slots_before(e, i) >> 4) << 4) + w * win
        src = pl.multiple_of(jnp.minimum(first, cap - win), pack)
        col = st[:, e:e + 1]
        hit = jnp.logical_and(col == (src + jw).astype(F32), col >= first.astype(F32))
        return jnp.where(hit, 1.0, 0.0).astype(BF16), ye_ref[0, e, pl.ds(src, win), :]

    acc = jnp.zeros((OUT_TILE, D_MODEL), F32)
    for e in range(0, N_EXPERTS, 2):
        (h0, r0), (h1, r1) = window(e, 0), window(e + 1, 0)
        acc = acc + jnp.dot(jnp.concatenate([h0, h1], axis=1), jnp.concatenate([r0, r1], axis=0),
                            preferred_element_type=F32)
    acc_ref[...] = acc
    for e in range(N_EXPERTS):
        first = (slots_before(e, i) >> 4) << 4
        n_win = (slots_before(e, i + 1) - first + win - 1) >> log_win

        def more(w, carry, e=e):
            hit, rows = window(e, w)
            acc_ref[...] += jnp.dot(hit, rows, preferred_element_type=F32)
            return carry

        lax.fori_loop(1, n_win, more, 0)
    acc = acc_ref[...]
    ms = jnp.mean(acc * acc, axis=-1, keepdims=True)
    gt2 = mod_ref[0][:, 5 * D_MODEL:6 * D_MODEL]
    o_ref[0] = x1_ref[0] + gt2 * (acc * lax.rsqrt(ms + NORM_EPS) * gain_ref[...])


def _combine(band, x1, slot_t, ye, mods, gain, cap):
    B, T, D = x1.shape
    E = ye.shape[1]
    grid_spec = pltpu.PrefetchScalarGridSpec(
        num_scalar_prefetch=1,
        grid=(B, T // OUT_TILE),
        in_specs=[pl.BlockSpec((1, OUT_TILE, D), lambda b, i, band: (b, i, 0)),
                  pl.BlockSpec((1, OUT_TILE, LANES), lambda b, i, band: (b, i, 0)),
                  pl.BlockSpec((1, E, cap, D), lambda b, i, band: (b, 0, 0, 0)),
                  pl.BlockSpec((1, 1, mods.shape[2]), lambda b, i, band: (b, 0, 0)),
                  pl.BlockSpec((1, D), lambda b, i, band: (0, 0))],
        out_specs=pl.BlockSpec((1, OUT_TILE, D), lambda b, i, band: (b, i, 0)),
        scratch_shapes=[pltpu.VMEM((OUT_TILE, D), F32)])
    return pl.pallas_call(
        functools.partial(_combine_kernel, cap),
        name="combine",
        grid_spec=grid_spec,
        out_shape=jax.ShapeDtypeStruct((B, T, D), F32),
        compiler_params=_cparams(("arbitrary", "arbitrary")),
    )(band, x1, slot_t, ye, mods, gain)


def _rope_tables(T):
    rows = T // GRID_W
    row = np.repeat(np.arange(rows, dtype=np.float32), GRID_W)
    col = np.tile(np.arange(GRID_W, dtype=np.float32), rows)
    n_freq = HEAD_DIM // 4
    freq = jnp.asarray(ROPE_BASE, F32) ** (-jnp.arange(n_freq, dtype=F32) / n_freq)
    ang = jnp.concatenate([row[:, None] * freq, col[:, None] * freq], axis=-1)
    cos, sin = jnp.cos(ang), jnp.sin(ang)
    reps = LANES // (HEAD_DIM // 2)
    cos_t = jnp.tile(cos, (1, reps))
    sin_t = jnp.tile(jnp.concatenate([-sin, sin], axis=-1), (1, reps // 2))
    return cos_t, sin_t


def kernel(x, c, ctx, c_ctx, w_mod, b_mod, norm_gains, w_in, rwkv_conv, rwkv_w0, rwkv_w2, rwkv_a0, rwkv_a2, rwkv_g2, rwkv_k_k, rwkv_k_a, rwkv_r_k, rwkv_lnx_w, rwkv_lnx_b, w_out, w_router, w_gate, w_up, w_down):
    B, T, D = x.shape
    assert D == D_MODEL and ctx.shape == (B, CTX_LEN, D) and T % OUT_TILE == 0 and w_mod.shape[0] == 1
    W = GROUP_WIDTH
    cap = max(1, EC_CAPACITY * T // N_EXPERTS)
    assert T // BAND_TILE < BAND_COLS

    n_rows = -(-(B + 1) // 8) * 8
    cc = jnp.concatenate([c, c_ctx[None, :], jnp.zeros((n_rows - B - 1, D), F32)], axis=0)
    mods = _modulation(cc, w_mod[0], b_mod).reshape(n_rows, 1, 6 * D)

    split = RET_COLS + 3 * W + 2 * DECAY_LORA + ICLR_LORA
    w_pad = jnp.concatenate([w_in[0][:, :split], jnp.zeros((D, LANES - ICLR_LORA), F32), w_in[0][:, split:]],
                            axis=1).astype(BF16)
    cs = split - RET_COLS
    cw = jnp.concatenate([rwkv_conv[0][:, :cs], jnp.zeros((3, LANES - ICLR_LORA), F32), rwkv_conv[0][:, cs:]], axis=1)
    zl = jnp.zeros((DECAY_LORA, W), F32)
    w2bd = jnp.concatenate([jnp.concatenate([rwkv_w2[0, 0], zl], axis=1),
                            jnp.concatenate([zl, rwkv_w2[0, 1]], axis=1)], axis=0).astype(BF16)
    w0cat = jnp.concatenate([rwkv_w0[0, 0], rwkv_w0[0, 1]])[None, :]
    a2pad = jnp.concatenate([rwkv_a2[0], jnp.zeros((LANES - ICLR_LORA, W), F32)], axis=0).astype(BF16)
    g2f, g2b = rwkv_g2[0, 0].astype(BF16), rwkv_g2[0, 1].astype(BF16)
    row = lambda a: a[0][None, :]
    cos_t, sin_t = _rope_tables(T)

    feature_consts = (cw, w0cat, w2bd, row(rwkv_a0), a2pad, g2f, g2b, row(rwkv_k_k), row(rwkv_k_a), row(rwkv_r_k))
    u_ret, r, k, v, a, b, lwf, lwb, gf, gb, bonus = _projection(
        x, ctx, mods, norm_gains[0, 0][None, :], w_pad, cos_t, sin_t, feature_consts)
    ret_f, ret_b = _retention(u_ret, T)
    rw_f, rw_b = _rwkv_scan(r, k, v, a, b, lwf, lwb, gf, gb, row(rwkv_lnx_w), row(rwkv_lnx_b), T)

    x1, h2, aff = _out_projection(x, ret_f, ret_b, rw_f, rw_b, bonus, w_out[0].astype(BF16), mods,
                                  norm_gains[0], w_router[0].T)
    slot, slot_t, bounds = _select(aff, cap)
    band = bounds[:, :, :BAND_COLS].astype(jnp.int32).reshape(-1)
    ye = _expert_ffn(band, h2, slot, aff, w_gate[0], w_up[0], w_down[0], cap)
    return _combine(band, x1, slot_t, ye, mods, norm_gains[0, 3][None, :], cap)
```

```python
import functools

import numpy as np
import jax
import jax.numpy as jnp
from jax import lax
from jax.experimental import pallas as pl
from jax.experimental.pallas import tpu as pltpu

F32 = jnp.float32
BF16 = jnp.bfloat16
HIGHEST = lax.Precision.HIGHEST

D_MODEL = 1024
CTX_LEN = 256
GRID_W = 64
HEAD_DIM = 64
GROUP_WIDTH = 512
LANES = 128
N_PAIRS = GROUP_WIDTH // LANES
HEADS_PER_GROUP = 4
GROUP_LANES = HEADS_PER_GROUP * HEAD_DIM
N_GROUPS = GROUP_WIDTH // GROUP_LANES
RET_COLS = 5 * GROUP_WIDTH
RWKV_COLS_PAD = 4 * GROUP_WIDTH
DECAY_LORA = 64
ICLR_LORA = 64
GATE_LORA = 128
N_EXPERTS = 16
EC_CAPACITY = 2
ROPE_BASE = 10000.0
NORM_EPS = 1e-6
RWKV_GN_EPS = 64e-5

TOK_TILE = 256
RET_CHUNK = 256
RWKV_CHUNK = 64
RWKV_CHUNKS_PER_STEP = 4
OUT_TILE = 512
BAND_TILE = 256
GATHER_WINDOW = BAND_TILE + 16
SLOT_WINDOW = 128
BAND_COLS = 32
VMEM_LIMIT = 56 * 1024 * 1024


def _cparams(sem):
    return pltpu.CompilerParams(dimension_semantics=sem, vmem_limit_bytes=VMEM_LIMIT)


def _dot(a, b):
    return jnp.dot(a.astype(BF16), b.astype(BF16), preferred_element_type=F32)


def _dot_nt(a, b):
    return lax.dot_general(a.astype(BF16), b.astype(BF16), (((1,), (1,)), ((), ())),
                           preferred_element_type=F32)


def _dot_exact(a, b):
    return jnp.dot(a, b, precision=HIGHEST, preferred_element_type=F32)


def _head_block_mask(n):
    r = lax.broadcasted_iota(jnp.int32, (n, n), 0)
    c = lax.broadcasted_iota(jnp.int32, (n, n), 1)
    return (r // HEAD_DIM) == (c // HEAD_DIM)


def _split_bf16(x, terms):
    out = []
    for _ in range(terms - 1):
        hi = x.astype(BF16)
        out.append(hi)
        x = x - hi.astype(F32)
    out.append(x.astype(BF16))
    return out


def _head_sum(x, scale=1.0):
    cols = []
    for g in range(x.shape[1] // GROUP_LANES):
        m = jnp.where(_head_block_mask(GROUP_LANES), scale, 0.0).astype(BF16)
        parts = _split_bf16(x[:, g * GROUP_LANES:(g + 1) * GROUP_LANES], 2)
        cols.append(jnp.dot(jnp.concatenate(parts, axis=1), jnp.concatenate([m, m], axis=0),
                            preferred_element_type=F32))
    return cols[0] if len(cols) == 1 else jnp.concatenate(cols, axis=1)


def _silu(x):
    return x * jax.nn.sigmoid(x)


def _round_robin(gens):
    results = [None] * len(gens)
    active = list(range(len(gens)))
    while active:
        for i in list(active):
            try:
                next(gens[i])
            except StopIteration as done:
                results[i] = done.value
                active.remove(i)
    return results


def _mod_kernel(c_ref, w_ref, b_ref, o_ref):
    o_ref[...] = _dot_exact(_silu(c_ref[...]), w_ref[...]) + b_ref[...]


def _modulation(cc, w_mod, b_mod):
    rows, d = cc.shape
    n = w_mod.shape[1]
    tn = 1536
    return pl.pallas_call(
        _mod_kernel,
        name="modulation",
        grid=(n // tn,),
        in_specs=[pl.BlockSpec((rows, d), lambda i: (0, 0)),
                  pl.BlockSpec((d, tn), lambda i: (0, i)),
                  pl.BlockSpec((1, tn), lambda i: (0, i))],
        out_specs=pl.BlockSpec((rows, tn), lambda i: (0, i)),
        out_shape=jax.ShapeDtypeStruct((rows, n), F32),
        compiler_params=_cparams(("arbitrary",)),
    )(cc, w_mod, b_mod)


def _proj_kernel(n_tiles, x_ref, ctx_ref, xp_ref, xn_ref, mod_ref, gain_ref, w_ref, cos_ref, sin_ref, *rest):
    feature_consts, ret_ref, feature_outs = rest[:10], rest[10], rest[11:]
    j = pl.program_id(1)
    is_lat = j > 0
    xin = jnp.concatenate([jnp.where(is_lat, x_ref[0], ctx_ref[0]), xp_ref[0], xn_ref[0]], axis=0)
    ms = jnp.mean(xin * xin, axis=-1, keepdims=True)
    y = xin * lax.rsqrt(ms + NORM_EPS) * gain_ref[...]
    m = mod_ref[0]
    h = (y * (1.0 + m[:, D_MODEL:2 * D_MODEL]) + m[:, 0:D_MODEL]).astype(BF16)
    u_rw = jnp.dot(h, w_ref[:, RET_COLS:], preferred_element_type=F32)
    has_prev = (j >= 2).astype(F32)
    has_next = jnp.logical_and(j >= 1, j <= n_tiles - 2).astype(F32)
    features = _rwkv_features(u_rw[0:TOK_TILE], u_rw[TOK_TILE + 7:TOK_TILE + 8] * has_prev,
                              u_rw[TOK_TILE + 8:TOK_TILE + 9] * has_next, *feature_consts, *feature_outs)

    def retention_columns():
        lat = is_lat.astype(F32)
        cos = cos_ref[...] * lat + (1.0 - lat)
        sin = sin_ref[...] * lat
        lane = lax.broadcasted_iota(jnp.int32, (1, LANES), 1)
        first_half = (lane % HEAD_DIM) < (HEAD_DIM // 2)
        W = GROUP_WIDTH
        for c, scale in enumerate((1.0, HEAD_DIM ** -0.5, None, None, None)):
            u = jnp.dot(h[0:TOK_TILE], w_ref[:, c * W:(c + 1) * W], preferred_element_type=F32)
            yield
            if scale is None:
                ret_ref[0, :, c * W:(c + 1) * W] = u
                continue
            for g in range(N_PAIRS):
                t = u[:, g * LANES:(g + 1) * LANES] * scale
                sw = jnp.where(first_half, pltpu.roll(t, LANES - HEAD_DIM // 2, 1), pltpu.roll(t, HEAD_DIM // 2, 1))
                ret_ref[0, :, c * W + g * LANES:c * W + (g + 1) * LANES] = t * cos + sw * sin

    _round_robin([features, retention_columns()])


def _projection(x, ctx, mods, gain, w_pad, cos_t, sin_t, feature_consts):
    B, T, D = x.shape
    nt = (T + CTX_LEN) // TOK_TILE
    S = T + CTX_LEN
    rb = TOK_TILE // 8
    const = lambda a: pl.BlockSpec(a.shape, lambda b, j: (0,) * a.ndim)
    seq_spec = pl.BlockSpec((1, TOK_TILE, GROUP_WIDTH), lambda b, j: (b, j, 0))
    lat_spec = pl.BlockSpec((1, TOK_TILE, GROUP_WIDTH), lambda b, j: (b, jnp.maximum(j - 1, 0), 0))
    seq_shape = jax.ShapeDtypeStruct((B, S, GROUP_WIDTH), F32)
    lat_shape = jax.ShapeDtypeStruct((B, T, GROUP_WIDTH), F32)
    return pl.pallas_call(
        functools.partial(_proj_kernel, nt),
        name="projection",
        grid=(B, nt),
        in_specs=[
            pl.BlockSpec((1, TOK_TILE, D), lambda b, j: (b, jnp.maximum(j - 1, 0), 0)),
            pl.BlockSpec((1, TOK_TILE, D), lambda b, j: (b, 0, 0)),
            pl.BlockSpec((1, 8, D), lambda b, j: (b, jnp.maximum((j - 1) * rb - 1, 0), 0)),
            pl.BlockSpec((1, 8, D), lambda b, j: (b, jnp.clip(j * rb, 0, T // 8 - 1), 0)),
            pl.BlockSpec((1, 1, mods.shape[2]), lambda b, j: (jnp.where(j == 0, B, b), 0, 0)),
            pl.BlockSpec((1, D), lambda b, j: (0, 0)),
            const(w_pad),
            pl.BlockSpec((TOK_TILE, LANES), lambda b, j: (jnp.maximum(j - 1, 0), 0)),
            pl.BlockSpec((TOK_TILE, LANES), lambda b, j: (jnp.maximum(j - 1, 0), 0)),
        ] + [const(a) for a in feature_consts],
        out_specs=[pl.BlockSpec((1, TOK_TILE, RET_COLS), lambda b, j: (b, j, 0))] + [seq_spec] * 7 + [lat_spec] * 3,
        out_shape=[jax.ShapeDtypeStruct((B, S, RET_COLS), F32)] + [seq_shape] * 7 + [lat_shape] * 3,
        compiler_params=_cparams(("arbitrary", "arbitrary")),
    )(x, ctx, x, x, mods, gain, w_pad, cos_t, sin_t, *feature_consts)


def _ret_group(d, gi, q_ref, k_ref, v_ref, g_ref, o_ref, s_ref, dmask_ref, qdec_ref, kdec_ref, cdec_ref):
    C = RET_CHUNK
    lane_head = lax.broadcasted_iota(jnp.int32, (1, GROUP_LANES), 1) // HEAD_DIM
    sl = slice(gi * GROUP_LANES, (gi + 1) * GROUP_LANES)
    q4, k4, v4 = q_ref[0, :, sl], k_ref[0, :, sl], v_ref[0, :, sl]
    state = s_ref[gi]
    q_s = jnp.concatenate([jnp.where(lane_head == hh, q4, 0.0) for hh in range(HEADS_PER_GROUP)], axis=0)
    scores = _dot_nt(q_s, k4) * dmask_ref[d, gi]
    cross = _dot(q4 * qdec_ref[d, :, sl], state)
    yield
    o_s = _dot(scores, v4)
    kd = (k4 * kdec_ref[d, :, sl]).T
    s_ref[gi] = state * cdec_ref[:, sl] + jnp.where(_head_block_mask(GROUP_LANES), _dot(kd, v4), 0.0)
    yield
    y = cross + jnp.where(lane_head == 0, o_s[0:C], 0.0)
    for hh in range(1, HEADS_PER_GROUP):
        y = y + jnp.where(lane_head == hh, o_s[hh * C:(hh + 1) * C], 0.0)
    ms = _head_sum(y * y, 1.0 / HEAD_DIM)
    o_ref[0, :, sl] = _silu(g_ref[0, :, sl]) * (y * lax.rsqrt(ms + NORM_EPS))


def _ret_kernel(qf, kf, vf, gf, qb, kb, vb, gb, dmask, qdec, kdec, cdec, of, ob, sf, sb):
    @pl.when(pl.program_id(1) == 0)
    def _():
        sf[...] = jnp.zeros_like(sf)
        sb[...] = jnp.zeros_like(sb)

    tables = (dmask, qdec, kdec, cdec)
    _round_robin([_ret_group(d, gi, *refs, *tables)
                  for d, refs in ((0, (qf, kf, vf, gf, of, sf)), (1, (qb, kb, vb, gb, ob, sb)))
                  for gi in range(N_GROUPS)])


def _retention_tables():
    n_heads = GROUP_WIDTH // HEAD_DIM
    log_gamma = np.log1p(-np.exp2(-5.0 - np.arange(n_heads, dtype=np.float64)))
    idx = np.arange(RET_CHUNK, dtype=np.float64)
    rel = idx[:, None] - idx[None, :]
    dm = np.where(rel >= 0, np.exp(np.maximum(rel, 0.0) * log_gamma[:, None, None]), 0.0)
    dmask = np.stack([dm, dm.transpose(0, 2, 1)])
    dmask = dmask.reshape(2, N_GROUPS, HEADS_PER_GROUP * RET_CHUNK, RET_CHUNK)
    pos = np.stack([idx, RET_CHUNK - 1.0 - idx])
    lg = np.repeat(log_gamma, HEAD_DIM)
    qdec = np.exp((pos[:, :, None] + 1.0) * lg)
    kdec = np.exp((RET_CHUNK - 1.0 - pos[:, :, None]) * lg)
    cdec = np.exp(RET_CHUNK * lg)[None, :]
    return [jnp.asarray(a, F32) for a in (dmask, qdec, kdec, cdec)]


def _retention(u_ret, T):
    B = u_ret.shape[0]
    n_lat = T // RET_CHUNK
    ns = n_lat + 1
    dmask, qdec, kdec, cdec = _retention_tables()

    def fchunk(s):
        return s

    def bchunk(s):
        return jnp.where(s == 0, 0, ns - s)

    def col(c, chunk):
        return pl.BlockSpec((1, RET_CHUNK, GROUP_WIDTH), lambda b, s: (b, chunk(s), c))

    const = lambda shape: pl.BlockSpec(shape, lambda b, s: (0,) * len(shape))
    in_specs = ([col(c, fchunk) for c in (0, 1, 2, 3)] + [col(c, bchunk) for c in (0, 1, 2, 4)]
                + [const(dmask.shape), const(qdec.shape), const(kdec.shape), const(cdec.shape)])
    out_specs = [pl.BlockSpec((1, RET_CHUNK, GROUP_WIDTH), lambda b, s: (b, jnp.maximum(s - 1, 0), 0)),
                 pl.BlockSpec((1, RET_CHUNK, GROUP_WIDTH), lambda b, s: (b, jnp.where(s == 0, n_lat - 1, n_lat - s), 0))]
    out_shape = [jax.ShapeDtypeStruct((B, T, GROUP_WIDTH), F32)] * 2
    return pl.pallas_call(
        _ret_kernel,
        name="retention",
        grid=(B, ns),
        in_specs=in_specs,
        out_specs=out_specs,
        out_shape=out_shape,
        scratch_shapes=[pltpu.VMEM((N_GROUPS, GROUP_LANES, GROUP_LANES), F32)] * 2,
        compiler_params=_cparams(("arbitrary", "arbitrary")),
    )(*([u_ret] * 8), dmask, qdec, kdec, cdec)


def _rwkv_features(x, prev_row, next_row, cw_ref, w0_ref, w2_ref, a0_ref, a2_ref, g2f_ref, g2b_ref,
                   kk_ref, ka_ref, rk_ref,
                   r_o, k_o, v_o, a_o, b_o, lwf_o, lwb_o, gf_o, gb_o, bon_o):
    rows = lax.broadcasted_iota(jnp.int32, (TOK_TILE, 1), 0)
    xm = jnp.where(rows == 0, prev_row, pltpu.roll(x, 1, 0))
    xp = jnp.where(rows == TOK_TILE - 1, next_row, pltpu.roll(x, TOK_TILE - 1, 0))
    rw = cw_ref[0:1, :] * xm + cw_ref[1:2, :] * x + cw_ref[2:3, :] * xp
    yield

    W = GROUP_WIDTH
    r, k, v, lo = rw[:, 0:W], rw[:, W:2 * W], rw[:, 2 * W:3 * W], rw[:, 3 * W:4 * W]
    z = w0_ref[...] + _dot(jnp.tanh(lo[:, 0:LANES]), w2_ref[...])
    log_decay = -np.exp(-0.5).astype(np.float32) * jax.nn.sigmoid(z)
    iclr = jax.nn.sigmoid(a0_ref[...] + _dot(lo[:, LANES:2 * LANES], a2_ref[...]))
    yield
    g_f = _dot(jax.nn.sigmoid(lo[:, 2 * LANES:3 * LANES]), g2f_ref[...])
    g_b = _dot(jax.nn.sigmoid(lo[:, 3 * LANES:4 * LANES]), g2b_ref[...])
    kk = k * kk_ref[...]
    yield
    kk = kk / jnp.maximum(jnp.sqrt(_head_sum(kk * kk)), 1e-12)
    k_mod = k * (1.0 + (iclr - 1.0) * ka_ref[...])
    yield
    bonus = _head_sum(r * k_mod * rk_ref[...]) * v
    r_o[0] = r
    k_o[0] = k_mod
    v_o[0] = v
    a_o[0] = -kk
    b_o[0] = kk * iclr
    lwf_o[0] = log_decay[:, 0:W]
    lwb_o[0] = log_decay[:, W:2 * W]
    gf_o[0] = g_f
    gb_o[0] = g_b
    bon_o[0] = (g_f + g_b) * bonus


def _dplr_local(r, k, v, a, b, lw, reverse):
    C, H, GL = RWKV_CHUNK, HEADS_PER_GROUP, GROUP_LANES
    CS = H * C
    ri = lax.broadcasted_iota(jnp.int32, (C, C), 0)
    ci = lax.broadcasted_iota(jnp.int32, (C, C), 1)
    tri = jnp.where((ci >= ri) if reverse else (ci <= ri), 1.0, 0.0).astype(BF16)
    lc = jnp.dot(jnp.concatenate([tri, tri, tri], axis=1), jnp.concatenate(_split_bf16(lw, 3), axis=0),
                 preferred_element_type=F32)
    yield
    ltot = jnp.sum(lw, axis=0, keepdims=True)
    a_t = a * jnp.exp(lc - lw)
    r_t = r * jnp.exp(lc)
    inv = jnp.exp(-lc)
    b_t, k_t = b * inv, k * inv
    to_end = jnp.exp(ltot - lc)
    b_e, k_e = b * to_end, k * to_end

    rs = lax.broadcasted_iota(jnp.int32, (CS, CS), 0)
    cs = lax.broadcasted_iota(jnp.int32, (CS, CS), 1)
    bd = (rs // C) == (cs // C)
    stack = lambda x: jnp.where(bd, jnp.concatenate([x] * H, axis=0), 0.0)
    unstack = lambda x: x[0:C] + x[C:2 * C] + x[2 * C:3 * C] + x[3 * C:4 * C]
    a_s, r_s, v_s = stack(a_t), stack(r_t), stack(v)
    g = _dot_nt(jnp.concatenate([a_s, r_s], axis=0), jnp.concatenate([b_t, k_t], axis=0))
    g_swapped = pltpu.roll(g, C, 1)
    t128 = lax.broadcasted_iota(jnp.int32, (C, 2 * C), 0)
    l128 = lax.broadcasted_iota(jnp.int32, (C, 2 * C), 1)
    s128, half128 = l128 % C, l128 // C
    incl128 = (s128 >= t128) if reverse else (s128 <= t128)
    strict128 = (s128 > t128) if reverse else (s128 < t128)
    zeros128 = jnp.zeros((C, 2 * C), F32)

    def block_diagonal(row0, from_k, tri):
        blocks = []
        for hh in range(H):
            half = hh % 2
            src = g if half == int(from_k) else g_swapped
            piece = jnp.where(jnp.logical_and(half128 == half, tri), src[row0 + hh * C:row0 + (hh + 1) * C], 0.0)
            blocks.append(jnp.concatenate([piece if c == hh // 2 else zeros128 for c in range(H // 2)], axis=1))
        return jnp.concatenate(blocks, axis=0)

    low = block_diagonal(0, False, strict128)
    ak = block_diagonal(0, True, strict128)
    rb = block_diagonal(CS, False, incl128)
    rk = block_diagonal(CS, True, incl128)
    yield
    akv = _dot(ak, v_s)
    tm = jnp.where(rs == cs, 1.0, 0.0) + low
    pw = _dot(low, low)
    yield
    n = 2
    while n < C // 2:
        both = _dot(jnp.concatenate([pw, tm], axis=0), pw)
        pw, tm = both[0:CS], tm + both[CS:]
        n *= 2
        yield
    tm = tm + _dot(tm, pw)
    yield
    uw = _dot(tm, jnp.concatenate([akv, a_s], axis=1))
    yield
    y0_s = _dot(jnp.concatenate([rb, rk], axis=1), jnp.concatenate([uw[:, 0:GL], v_s], axis=0))
    qa_s = _dot(rb, uw[:, GL:])
    yield
    u0, w = unstack(uw[:, 0:GL]), unstack(uw[:, GL:])
    y0, q_add = unstack(y0_s), unstack(qa_s)
    zeros = jnp.zeros((C, GL), F32)
    stack_t = jnp.concatenate([w, u0, v, zeros], axis=0).T
    rm = jnp.concatenate([jnp.concatenate([b_e, zeros], axis=1),
                          jnp.concatenate([zeros, b_e], axis=1),
                          jnp.concatenate([zeros, k_e], axis=1),
                          jnp.concatenate([zeros, zeros], axis=1)], axis=0)
    mn = _dot(stack_t, rm)
    return (r_t + q_add, y0, jnp.exp(ltot), jnp.where(bd, mn[:, 0:GL], 0.0), jnp.where(bd, mn[:, GL:], 0.0))


def _scan_kernel(rf, kf, vf, af, bf, lwf, gf, rb, kb, vb, ab, bb, lwb, gb, lnw, lnb, of, ob, sf, sb):
    @pl.when(pl.program_id(1) == 0)
    def _():
        sf[...] = jnp.zeros_like(sf)
        sb[...] = jnp.zeros_like(sb)

    C = RWKV_CHUNK
    fwd = (False, (rf, kf, vf, af, bf, lwf), gf, of, sf)
    bwd = (True, (rb, kb, vb, ab, bb, lwb), gb, ob, sb)
    chains = []
    for reverse, refs, g_ref, o_ref, s_ref in (fwd, bwd):
        subs = range(RWKV_CHUNKS_PER_STEP)
        for gi in range(N_GROUPS):
            sl = slice(gi * GROUP_LANES, (gi + 1) * GROUP_LANES)
            for sub in (reversed(subs) if reverse else subs):
                rows = slice(sub * C, (sub + 1) * C)
                chains.append((gi, sl, rows, g_ref, o_ref, s_ref,
                               _dplr_local(*[ref[0, rows, sl] for ref in refs], reverse)))
    local = _round_robin([c[-1] for c in chains])

    def carry_state(s_ref, gi, parts):
        st = s_ref[gi]
        ys = []
        for q, y0, decay, m_t, n_t in parts:
            ys.append(_dot_nt(q, st) + y0)
            st = st * decay + _dot(st, m_t) + n_t
            yield
        s_ref[gi] = st
        return ys

    per = RWKV_CHUNKS_PER_STEP
    ys = _round_robin([carry_state(chains[i][5], chains[i][0], local[i:i + per]) for i in range(0, len(chains), per)])
    y_all = jnp.concatenate([y for group in ys for y in group], axis=0)
    dlt = y_all - _head_sum(y_all, 1.0 / HEAD_DIM)
    yn = dlt * lax.rsqrt(_head_sum(dlt * dlt, 1.0 / HEAD_DIM) + RWKV_GN_EPS)
    for i, (gi, sl, rows, g_ref, o_ref, s_ref, _) in enumerate(chains):
        o_ref[0, rows, sl] = g_ref[0, rows, sl] * (yn[i * C:(i + 1) * C] * lnw[:, sl] + lnb[:, sl])


def _rwkv_scan(r, k, v, a, b, lwf, lwb, gf, gb, lnw, lnb, T):
    B, S, W = r.shape
    C = RWKV_CHUNK * RWKV_CHUNKS_PER_STEP
    n_ctx = CTX_LEN // C
    n_lat = T // C
    ns = n_ctx + n_lat

    def fchunk(s):
        return s

    def bchunk(s):
        return jnp.where(s < n_ctx, n_ctx - 1 - s, ns + n_ctx - 1 - s)

    def fout(s):
        return jnp.maximum(s - n_ctx, 0)

    def bout(s):
        return jnp.where(s < n_ctx, n_lat - 1, ns - 1 - s)

    def spec(chunk):
        return pl.BlockSpec((1, C, W), lambda bi, s: (bi, chunk(s), 0))

    const = pl.BlockSpec((1, W), lambda bi, s: (0, 0))
    in_specs = [spec(fchunk)] * 6 + [spec(fout)] + [spec(bchunk)] * 6 + [spec(bout)] + [const, const]
    return pl.pallas_call(
        _scan_kernel,
        name="rwkv_scan",
        grid=(B, ns),
        in_specs=in_specs,
        out_specs=[spec(fout), spec(bout)],
        out_shape=[jax.ShapeDtypeStruct((B, T, W), F32)] * 2,
        scratch_shapes=[pltpu.VMEM((N_GROUPS, GROUP_LANES, GROUP_LANES), F32)] * 2,
        compiler_params=_cparams(("arbitrary", "arbitrary")),
    )(r, k, v, a, b, lwf, gf, r, k, v, a, b, lwb, gb, lnw, lnb)


def _out_kernel(x_ref, rf_ref, rb_ref, wf_ref, wb_ref, bon_ref, wo_ref, mod_ref, gains_ref, wr_ref,
                x1_ref, h2_ref, aff_ref):
    W = GROUP_WIDTH
    ret = rf_ref[0] + rb_ref[0]
    rwk = wf_ref[0] + wb_ref[0] + bon_ref[0]
    mix = _dot(ret, wo_ref[0:W, :]) + _dot(rwk, wo_ref[W:2 * W, :])
    m = mod_ref[0]
    D = D_MODEL
    ms = jnp.mean(mix * mix, axis=-1, keepdims=True)
    x1 = x_ref[0] + m[:, 2 * D:3 * D] * (mix * lax.rsqrt(ms + NORM_EPS) * gains_ref[1:2, :])
    x1_ref[0] = x1
    ms2 = jnp.mean(x1 * x1, axis=-1, keepdims=True)
    h2 = (x1 * lax.rsqrt(ms2 + NORM_EPS) * gains_ref[2:3, :]) * (1.0 + m[:, 4 * D:5 * D]) + m[:, 3 * D:4 * D]
    h2_ref[0] = h2.astype(BF16)
    h_hi, h_lo = _split_bf16(h2, 2)
    w_hi, w_lo = _split_bf16(wr_ref[...], 2)
    logits = lax.dot_general(jnp.concatenate([w_hi, w_lo, w_hi], axis=1), jnp.concatenate([h_hi, h_hi, h_lo], axis=1),
                             (((1,), (1,)), ((), ())), preferred_element_type=F32)
    e = jnp.exp(logits - jnp.max(logits, axis=0, keepdims=True))
    aff_ref[0] = e / jnp.sum(e, axis=0, keepdims=True)


def _out_projection(x, ret_f, ret_b, rw_f, rw_b, bonus, w_out, mods, gains, wr_pad):
    B, T, D = x.shape
    nt = T // OUT_TILE
    tok = lambda w: pl.BlockSpec((1, OUT_TILE, w), lambda b, i: (b, i, 0))
    const = lambda a: pl.BlockSpec(a.shape, lambda b, i: (0,) * a.ndim)
    return pl.pallas_call(
        _out_kernel,
        name="out_projection",
        grid=(B, nt),
        in_specs=[tok(D)] + [tok(GROUP_WIDTH)] * 5 + [const(w_out),
                  pl.BlockSpec((1, 1, mods.shape[2]), lambda b, i: (b, 0, 0)), const(gains), const(wr_pad)],
        out_specs=[tok(D), tok(D), pl.BlockSpec((1, N_EXPERTS, OUT_TILE), lambda b, i: (b, 0, i))],
        out_shape=[jax.ShapeDtypeStruct((B, T, D), F32), jax.ShapeDtypeStruct((B, T, D), BF16),
                   jax.ShapeDtypeStruct((B, N_EXPERTS, T), F32)],
        compiler_params=_cparams(("arbitrary", "arbitrary")),
    )(x, ret_f, ret_b, rw_f, rw_b, bonus, w_out, mods, gains, wr_pad)


def _cumsum_lanes(x):
    n = x.shape[1]
    lane = lax.broadcasted_iota(jnp.int32, (1, n), 1)
    sh = 1
    while sh < n:
        x = x + jnp.where(lane >= sh, pltpu.roll(x, sh, 1), 0.0)
        sh *= 2
    return x


def _select_kernel(cap, aff_ref, slot_ref, slot_t_ref, bounds_ref):
    aff = aff_ref[0]
    capf = jnp.float32(cap)

    def body(i, thr):
        cand = thr | (jnp.int32(1) << (30 - i))
        cnt = jnp.sum(jnp.where(aff >= pltpu.bitcast(cand, F32), 1.0, 0.0), axis=1, keepdims=True)
        return jnp.where(cnt >= capf, cand, thr)

    thr = lax.fori_loop(0, 31, body, jnp.zeros((aff.shape[0], 1), jnp.int32))
    min_normal_bits = jnp.int32(0x00800000)
    above = pltpu.bitcast(jnp.maximum(thr + 1, min_normal_bits), F32)
    gt = aff >= above
    eq = jnp.where(jnp.logical_and(aff >= pltpu.bitcast(thr, F32), jnp.logical_not(gt)), 1.0, 0.0)
    need = capf - jnp.sum(jnp.where(gt, 1.0, 0.0), axis=1, keepdims=True)
    eq_before = _cumsum_lanes(eq) - eq
    sel = jnp.where(jnp.logical_or(gt, jnp.logical_and(eq > 0.0, eq_before < need)), 1.0, 0.0)
    count = _cumsum_lanes(sel)
    slot = jnp.where(sel > 0.0, count - sel, -1.0)
    slot_ref[0] = slot
    pad = jnp.full((LANES - slot.shape[0], slot.shape[1]), -1.0, F32)
    slot_t_ref[0] = jnp.concatenate([slot, pad], axis=0).T

    n_experts, n_tok = slot.shape
    tok = lax.broadcasted_iota(jnp.int32, (1, n_tok), 1)
    lane = lax.broadcasted_iota(jnp.int32, (1, LANES), 1)
    bounds = jnp.zeros((n_experts, LANES), F32)
    for i in range(1, n_tok // BAND_TILE + 1):
        before = jnp.sum(jnp.where(tok == i * BAND_TILE - 1, count, 0.0), axis=1, keepdims=True)
        bounds = jnp.where(lane == i, before, bounds)
    bounds_ref[0] = bounds


def _select(aff, cap):
    B, E, T = aff.shape
    return pl.pallas_call(
        functools.partial(_select_kernel, cap),
        name="expert_select",
        grid=(B,),
        in_specs=[pl.BlockSpec((1, E, T), lambda b: (b, 0, 0))],
        out_specs=[pl.BlockSpec((1, E, T), lambda b: (b, 0, 0)), pl.BlockSpec((1, T, LANES), lambda b: (b, 0, 0)),
                   pl.BlockSpec((1, E, LANES), lambda b: (b, 0, 0))],
        out_shape=[jax.ShapeDtypeStruct((B, E, T), F32), jax.ShapeDtypeStruct((B, T, LANES), F32),
                   jax.ShapeDtypeStruct((B, E, LANES), F32)],
        compiler_params=_cparams(("arbitrary",)),
    )(aff)


def _expert_kernel(cap, band_ref, h_ref, slot_ref, aff_ref, wg_ref, wu_ref, wd_ref, ye_ref, xs_ref, gate_ref):
    base = (pl.program_id(0) * pl.num_programs(1) + pl.program_id(1)) * BAND_COLS
    xs_ref[...] = jnp.zeros_like(xs_ref)
    gate_ref[...] = jnp.zeros_like(gate_ref)
    jw = lax.broadcasted_iota(jnp.int32, (GATHER_WINDOW, 1), 0)
    for ti in range(h_ref.shape[1] // BAND_TILE):
        first = pl.multiple_of((band_ref[base + ti] >> 3) << 3, 8)
        hit = slot_ref[0, 0, ti:ti + 1, :] == (first + jw).astype(F32)
        xs_ref[pl.ds(first, GATHER_WINDOW), :] += jnp.dot(
            jnp.where(hit, 1.0, 0.0).astype(BF16), h_ref[0, ti * BAND_TILE:(ti + 1) * BAND_TILE, :],
            preferred_element_type=F32)
        gate_ref[pl.ds(first, GATHER_WINDOW), :] += jnp.sum(
            jnp.where(hit, aff_ref[0, 0, ti:ti + 1, :], 0.0), axis=1, keepdims=True)
    xs = xs_ref[0:cap, :].astype(BF16)
    hid = _silu(_dot(xs, wg_ref[0])) * _dot(xs, wu_ref[0])
    ye = _dot(hid, wd_ref[0]) * gate_ref[0:cap, :]
    ye_ref[0, 0] = ye.astype(BF16)


def _expert_ffn(band, h2, slot, aff, wg, wu, wd, cap):
    B, T, D = h2.shape
    E = wg.shape[0]
    nt = T // BAND_TILE
    row = pl.BlockSpec((1, 1, nt, BAND_TILE), lambda b, e, band: (b, e, 0, 0))
    wspec = pl.BlockSpec((1, D, D), lambda b, e, band: (e, 0, 0))
    grid_spec = pltpu.PrefetchScalarGridSpec(
        num_scalar_prefetch=1,
        grid=(B, E),
        in_specs=[pl.BlockSpec((1, T, D), lambda b, e, band: (b, 0, 0)), row, row, wspec, wspec, wspec],
        out_specs=pl.BlockSpec((1, 1, cap, D), lambda b, e, band: (b, e, 0, 0)),
        scratch_shapes=[pltpu.VMEM((cap + GATHER_WINDOW, D), F32), pltpu.VMEM((cap + GATHER_WINDOW, 1), F32)])
    return pl.pallas_call(
        functools.partial(_expert_kernel, cap),
        name="expert_ffn",
        grid_spec=grid_spec,
        out_shape=jax.ShapeDtypeStruct((B, E, cap, D), BF16),
        compiler_params=_cparams(("arbitrary", "arbitrary")),
    )(band, h2, slot.reshape(B, E, nt, BAND_TILE), aff.reshape(B, E, nt, BAND_TILE), wg, wu, wd)


def _combine_kernel(cap, band_ref, x1_ref, slot_t_ref, ye_ref, mod_ref, gain_ref, o_ref, acc_ref):
    win = min(SLOT_WINDOW, cap)
    log_win = win.bit_length() - 1
    pack = 16
    b, i = pl.program_id(0), pl.program_id(1)
    tiles_per = OUT_TILE // BAND_TILE
    st = slot_t_ref[0]
    jw = lax.broadcasted_iota(jnp.int32, (1, win), 1)

    def slots_before(e, tile):
        return band_ref[(b * N_EXPERTS + e) * BAND_COLS + tile * tiles_per]

    def window(e, w):
        first = ((slots_before(e, i) >> 4) << 4) + w * win
        src = pl.multiple_of(jnp.minimum(first, cap - win), pack)
        col = st[:, e:e + 1]
        hit = jnp.logical_and(col == (src + jw).astype(F32), col >= first.astype(F32))
        return jnp.where(hit, 1.0, 0.0).astype(BF16), ye_ref[0, e, pl.ds(src, win), :]

    acc = jnp.zeros((OUT_TILE, D_MODEL), F32)
    for e in range(0, N_EXPERTS, 2):
        (h0, r0), (h1, r1) = window(e, 0), window(e + 1, 0)
        acc = acc + jnp.dot(jnp.concatenate([h0, h1], axis=1), jnp.concatenate([r0, r1], axis=0),
                            preferred_element_type=F32)
    acc_ref[...] = acc
    for e in range(N_EXPERTS):
        first = (slots_before(e, i) >> 4) << 4
        n_win = (slots_before(e, i + 1) - first + win - 1) >> log_win

        def more(w, carry, e=e):
            hit, rows = window(e, w)
            acc_ref[...] += jnp.dot(hit, rows, preferred_element_type=F32)
            return carry

        lax.fori_loop(1, n_win, more, 0)
    acc = acc_ref[...]
    ms = jnp.mean(acc * acc, axis=-1, keepdims=True)
    gt2 = mod_ref[0][:, 5 * D_MODEL:6 * D_MODEL]
    o_ref[0] = x1_ref[0] + gt2 * (acc * lax.rsqrt(ms + NORM_EPS) * gain_ref[...])


def _combine(band, x1, slot_t, ye, mods, gain, cap):
    B, T, D = x1.shape
    E = ye.shape[1]
    grid_spec = pltpu.PrefetchScalarGridSpec(
        num_scalar_prefetch=1,
        grid=(B, T // OUT_TILE),
        in_specs=[pl.BlockSpec((1, OUT_TILE, D), lambda b, i, band: (b, i, 0)),
                  pl.BlockSpec((1, OUT_TILE, LANES), lambda b, i, band: (b, i, 0)),
                  pl.BlockSpec((1, E, cap, D), lambda b, i, band: (b, 0, 0, 0)),
                  pl.BlockSpec((1, 1, mods.shape[2]), lambda b, i, band: (b, 0, 0)),
                  pl.BlockSpec((1, D), lambda b, i, band: (0, 0))],
        out_specs=pl.BlockSpec((1, OUT_TILE, D), lambda b, i, band: (b, i, 0)),
        scratch_shapes=[pltpu.VMEM((OUT_TILE, D), F32)])
    return pl.pallas_call(
        functools.partial(_combine_kernel, cap),
        name="combine",
        grid_spec=grid_spec,
        out_shape=jax.ShapeDtypeStruct((B, T, D), F32),
        compiler_params=_cparams(("arbitrary", "arbitrary")),
    )(band, x1, slot_t, ye, mods, gain)


def _rope_tables(T):
    rows = T // GRID_W
    row = np.repeat(np.arange(rows, dtype=np.float32), GRID_W)
    col = np.tile(np.arange(GRID_W, dtype=np.float32), rows)
    n_freq = HEAD_DIM // 4
    freq = jnp.asarray(ROPE_BASE, F32) ** (-jnp.arange(n_freq, dtype=F32) / n_freq)
    ang = jnp.concatenate([row[:, None] * freq, col[:, None] * freq], axis=-1)
    cos, sin = jnp.cos(ang), jnp.sin(ang)
    reps = LANES // (HEAD_DIM // 2)
    cos_t = jnp.tile(cos, (1, reps))
    sin_t = jnp.tile(jnp.concatenate([-sin, sin], axis=-1), (1, reps // 2))
    return cos_t, sin_t


def kernel(x, c, ctx, c_ctx, w_mod, b_mod, norm_gains, w_in, rwkv_conv, rwkv_w0, rwkv_w2, rwkv_a0, rwkv_a2, rwkv_g2, rwkv_k_k, rwkv_k_a, rwkv_r_k, rwkv_lnx_w, rwkv_lnx_b, w_out, w_router, w_gate, w_up, w_down):
    B, T, D = x.shape
    assert D == D_MODEL and ctx.shape == (B, CTX_LEN, D) and T % OUT_TILE == 0 and w_mod.shape[0] == 1
    W = GROUP_WIDTH
    cap = max(1, EC_CAPACITY * T // N_EXPERTS)
    assert T // BAND_TILE < BAND_COLS

    n_rows = -(-(B + 1) // 8) * 8
    cc = jnp.concatenate([c, c_ctx[None, :], jnp.zeros((n_rows - B - 1, D), F32)], axis=0)
    mods = _modulation(cc, w_mod[0], b_mod).reshape(n_rows, 1, 6 * D)

    split = RET_COLS + 3 * W + 2 * DECAY_LORA + ICLR_LORA
    w_pad = jnp.concatenate([w_in[0][:, :split], jnp.zeros((D, LANES - ICLR_LORA), F32), w_in[0][:, split:]],
                            axis=1).astype(BF16)
    cs = split - RET_COLS
    cw = jnp.concatenate([rwkv_conv[0][:, :cs], jnp.zeros((3, LANES - ICLR_LORA), F32), rwkv_conv[0][:, cs:]], axis=1)
    zl = jnp.zeros((DECAY_LORA, W), F32)
    w2bd = jnp.concatenate([jnp.concatenate([rwkv_w2[0, 0], zl], axis=1),
                            jnp.concatenate([zl, rwkv_w2[0, 1]], axis=1)], axis=0).astype(BF16)
    w0cat = jnp.concatenate([rwkv_w0[0, 0], rwkv_w0[0, 1]])[None, :]
    a2pad = jnp.concatenate([rwkv_a2[0], jnp.zeros((LANES - ICLR_LORA, W), F32)], axis=0).astype(BF16)
    g2f, g2b = rwkv_g2[0, 0].astype(BF16), rwkv_g2[0, 1].astype(BF16)
    row = lambda a: a[0][None, :]
    cos_t, sin_t = _rope_tables(T)

    feature_consts = (cw, w0cat, w2bd, row(rwkv_a0), a2pad, g2f, g2b, row(rwkv_k_k), row(rwkv_k_a), row(rwkv_r_k))
    u_ret, r, k, v, a, b, lwf, lwb, gf, gb, bonus = _projection(
        x, ctx, mods, norm_gains[0, 0][None, :], w_pad, cos_t, sin_t, feature_consts)
    ret_f, ret_b = _retention(u_ret, T)
    rw_f, rw_b = _rwkv_scan(r, k, v, a, b, lwf, lwb, gf, gb, row(rwkv_lnx_w), row(rwkv_lnx_b), T)

    x1, h2, aff = _out_projection(x, ret_f, ret_b, rw_f, rw_b, bonus, w_out[0].astype(BF16), mods,
                                  norm_gains[0], w_router[0].T)
    slot, slot_t, bounds = _select(aff, cap)
    band = bounds[:, :, :BAND_COLS].astype(jnp.int32).reshape(-1)
    ye = _expert_ffn(band, h2, slot, aff, w_gate[0], w_up[0], w_down[0], cap)
    return _combine(band, x1, slot_t, ye, mods, norm_gains[0, 3][None, :], cap)
```

```python
import functools

import numpy as np
import jax
import jax.numpy as jnp
from jax import lax
from jax.experimental import pallas as pl
from jax.experimental.pallas import tpu as pltpu

F32 = jnp.float32
BF16 = jnp.bfloat16
HIGHEST = lax.Precision.HIGHEST

D_MODEL = 1024
CTX_LEN = 256
GRID_W = 64
HEAD_DIM = 64
GROUP_WIDTH = 512
LANES = 128
N_PAIRS = GROUP_WIDTH // LANES
HEADS_PER_GROUP = 4
GROUP_LANES = HEADS_PER_GROUP * HEAD_DIM
N_GROUPS = GROUP_WIDTH // GROUP_LANES
RET_COLS = 5 * GROUP_WIDTH
RWKV_COLS_PAD = 4 * GROUP_WIDTH
DECAY_LORA = 64
ICLR_LORA = 64
GATE_LORA = 128
N_EXPERTS = 16
EC_CAPACITY = 2
ROPE_BASE = 10000.0
NORM_EPS = 1e-6
RWKV_GN_EPS = 64e-5

TOK_TILE = 256
RET_CHUNK = 256
RWKV_CHUNK = 64
RWKV_CHUNKS_PER_STEP = 4
OUT_TILE = 512
BAND_TILE = 256
ROW_PACK = 16
GATHER_WINDOW = 64
SLOT_WINDOW = 128
BAND_COLS = 32
VMEM_LIMIT = 56 * 1024 * 1024


def _cparams(sem):
    return pltpu.CompilerParams(dimension_semantics=sem, vmem_limit_bytes=VMEM_LIMIT)


def _dot(a, b):
    return jnp.dot(a.astype(BF16), b.astype(BF16), preferred_element_type=F32)


def _dot_nt(a, b):
    return lax.dot_general(a.astype(BF16), b.astype(BF16), (((1,), (1,)), ((), ())),
                           preferred_element_type=F32)


def _dot_exact(a, b):
    return jnp.dot(a, b, precision=HIGHEST, preferred_element_type=F32)


def _head_block_mask(n):
    r = lax.broadcasted_iota(jnp.int32, (n, n), 0)
    c = lax.broadcasted_iota(jnp.int32, (n, n), 1)
    return (r // HEAD_DIM) == (c // HEAD_DIM)


def _split_bf16(x, terms):
    out = []
    for _ in range(terms - 1):
        hi = x.astype(BF16)
        out.append(hi)
        x = x - hi.astype(F32)
    out.append(x.astype(BF16))
    return out


def _head_sum(x, scale=1.0):
    cols = []
    for g in range(x.shape[1] // GROUP_LANES):
        m = jnp.where(_head_block_mask(GROUP_LANES), scale, 0.0).astype(BF16)
        parts = _split_bf16(x[:, g * GROUP_LANES:(g + 1) * GROUP_LANES], 2)
        cols.append(jnp.dot(jnp.concatenate(parts, axis=1), jnp.concatenate([m, m], axis=0),
                            preferred_element_type=F32))
    return cols[0] if len(cols) == 1 else jnp.concatenate(cols, axis=1)


def _silu(x):
    return x * jax.nn.sigmoid(x)


def _round_robin(gens):
    results = [None] * len(gens)
    active = list(range(len(gens)))
    while active:
        for i in list(active):
            try:
                next(gens[i])
            except StopIteration as done:
                results[i] = done.value
                active.remove(i)
    return results


def _mod_kernel(c_ref, w_ref, b_ref, o_ref):
    o_ref[...] = _dot_exact(_silu(c_ref[...]), w_ref[...]) + b_ref[...]


def _modulation(cc, w_mod, b_mod):
    rows, d = cc.shape
    n = w_mod.shape[1]
    tn = 1536
    return pl.pallas_call(
        _mod_kernel,
        name="modulation",
        grid=(n // tn,),
        in_specs=[pl.BlockSpec((rows, d), lambda i: (0, 0)),
                  pl.BlockSpec((d, tn), lambda i: (0, i)),
                  pl.BlockSpec((1, tn), lambda i: (0, i))],
        out_specs=pl.BlockSpec((rows, tn), lambda i: (0, i)),
        out_shape=jax.ShapeDtypeStruct((rows, n), F32),
        compiler_params=_cparams(("arbitrary",)),
    )(cc, w_mod, b_mod)


def _proj_kernel(n_tiles, x_ref, ctx_ref, xp_ref, xn_ref, mod_ref, gain_ref, w_ref, cos_ref, sin_ref, *rest):
    feature_consts, ret_ref, feature_outs = rest[:10], rest[10], rest[11:]
    j = pl.program_id(1)
    is_lat = j > 0
    xin = jnp.concatenate([jnp.where(is_lat, x_ref[0], ctx_ref[0]), xp_ref[0], xn_ref[0]], axis=0)
    ms = jnp.mean(xin * xin, axis=-1, keepdims=True)
    y = xin * lax.rsqrt(ms + NORM_EPS) * gain_ref[...]
    m = mod_ref[0]
    h = (y * (1.0 + m[:, D_MODEL:2 * D_MODEL]) + m[:, 0:D_MODEL]).astype(BF16)
    u_rw = jnp.dot(h, w_ref[:, RET_COLS:], preferred_element_type=F32)
    has_prev = (j >= 2).astype(F32)
    has_next = jnp.logical_and(j >= 1, j <= n_tiles - 2).astype(F32)
    features = _rwkv_features(u_rw[0:TOK_TILE], u_rw[TOK_TILE + 7:TOK_TILE + 8] * has_prev,
                              u_rw[TOK_TILE + 8:TOK_TILE + 9] * has_next, *feature_consts, *feature_outs)

    def retention_columns():
        lat = is_lat.astype(F32)
        cos = cos_ref[...] * lat + (1.0 - lat)
        sin = sin_ref[...] * lat
        lane = lax.broadcasted_iota(jnp.int32, (1, LANES), 1)
        first_half = (lane % HEAD_DIM) < (HEAD_DIM // 2)
        W = GROUP_WIDTH
        for c, scale in enumerate((1.0, HEAD_DIM ** -0.5, None, None, None)):
            u = jnp.dot(h[0:TOK_TILE], w_ref[:, c * W:(c + 1) * W], preferred_element_type=F32)
            yield
            if scale is None:
                ret_ref[0, :, c * W:(c + 1) * W] = u
                continue
            for g in range(N_PAIRS):
                t = u[:, g * LANES:(g + 1) * LANES] * scale
                sw = jnp.where(first_half, pltpu.roll(t, LANES - HEAD_DIM // 2, 1), pltpu.roll(t, HEAD_DIM // 2, 1))
                ret_ref[0, :, c * W + g * LANES:c * W + (g + 1) * LANES] = t * cos + sw * sin

    _round_robin([features, retention_columns()])


def _projection(x, ctx, mods, gain, w_pad, cos_t, sin_t, feature_consts):
    B, T, D = x.shape
    nt = (T + CTX_LEN) // TOK_TILE
    S = T + CTX_LEN
    rb = TOK_TILE // 8
    const = lambda a: pl.BlockSpec(a.shape, lambda b, j: (0,) * a.ndim)
    seq_spec = pl.BlockSpec((1, TOK_TILE, GROUP_WIDTH), lambda b, j: (b, j, 0))
    lat_spec = pl.BlockSpec((1, TOK_TILE, GROUP_WIDTH), lambda b, j: (b, jnp.maximum(j - 1, 0), 0))
    seq_shape = jax.ShapeDtypeStruct((B, S, GROUP_WIDTH), F32)
    lat_shape = jax.ShapeDtypeStruct((B, T, GROUP_WIDTH), F32)
    return pl.pallas_call(
        functools.partial(_proj_kernel, nt),
        name="projection",
        grid=(B, nt),
        in_specs=[
            pl.BlockSpec((1, TOK_TILE, D), lambda b, j: (b, jnp.maximum(j - 1, 0), 0)),
            pl.BlockSpec((1, TOK_TILE, D), lambda b, j: (b, 0, 0)),
            pl.BlockSpec((1, 8, D), lambda b, j: (b, jnp.maximum((j - 1) * rb - 1, 0), 0)),
            pl.BlockSpec((1, 8, D), lambda b, j: (b, jnp.clip(j * rb, 0, T // 8 - 1), 0)),
            pl.BlockSpec((1, 1, mods.shape[2]), lambda b, j: (jnp.where(j == 0, B, b), 0, 0)),
            pl.BlockSpec((1, D), lambda b, j: (0, 0)),
            const(w_pad),
            pl.BlockSpec((TOK_TILE, LANES), lambda b, j: (jnp.maximum(j - 1, 0), 0)),
            pl.BlockSpec((TOK_TILE, LANES), lambda b, j: (jnp.maximum(j - 1, 0), 0)),
        ] + [const(a) for a in feature_consts],
        out_specs=[pl.BlockSpec((1, TOK_TILE, RET_COLS), lambda b, j: (b, j, 0))] + [seq_spec] * 7 + [lat_spec] * 3,
        out_shape=[jax.ShapeDtypeStruct((B, S, RET_COLS), F32)] + [seq_shape] * 7 + [lat_shape] * 3,
        compiler_params=_cparams(("arbitrary", "arbitrary")),
    )(x, ctx, x, x, mods, gain, w_pad, cos_t, sin_t, *feature_consts)


def _ret_group(d, gi, q_ref, k_ref, v_ref, g_ref, o_ref, s_ref, dmask_ref, qdec_ref, kdec_ref, cdec_ref):
    C = RET_CHUNK
    lane_head = lax.broadcasted_iota(jnp.int32, (1, GROUP_LANES), 1) // HEAD_DIM
    sl = slice(gi * GROUP_LANES, (gi + 1) * GROUP_LANES)
    q4, k4, v4 = q_ref[0, :, sl], k_ref[0, :, sl], v_ref[0, :, sl]
    state = s_ref[gi]
    q_s = jnp.concatenate([jnp.where(lane_head == hh, q4, 0.0) for hh in range(HEADS_PER_GROUP)], axis=0)
    scores = _dot_nt(q_s, k4) * dmask_ref[d, gi]
    cross = _dot(q4 * qdec_ref[d, :, sl], state)
    yield
    o_s = _dot(scores, v4)
    kd = (k4 * kdec_ref[d, :, sl]).T
    s_ref[gi] = state * cdec_ref[:, sl] + jnp.where(_head_block_mask(GROUP_LANES), _dot(kd, v4), 0.0)
    yield
    y = cross + jnp.where(lane_head == 0, o_s[0:C], 0.0)
    for hh in range(1, HEADS_PER_GROUP):
        y = y + jnp.where(lane_head == hh, o_s[hh * C:(hh + 1) * C], 0.0)
    ms = _head_sum(y * y, 1.0 / HEAD_DIM)
    o_ref[0, :, sl] = _silu(g_ref[0, :, sl]) * (y * lax.rsqrt(ms + NORM_EPS))


def _ret_kernel(qf, kf, vf, gf, qb, kb, vb, gb, dmask, qdec, kdec, cdec, of, ob, sf, sb):
    @pl.when(pl.program_id(1) == 0)
    def _():
        sf[...] = jnp.zeros_like(sf)
        sb[...] = jnp.zeros_like(sb)

    tables = (dmask, qdec, kdec, cdec)
    _round_robin([_ret_group(d, gi, *refs, *tables)
                  for d, refs in ((0, (qf, kf, vf, gf, of, sf)), (1, (qb, kb, vb, gb, ob, sb)))
                  for gi in range(N_GROUPS)])


def _retention_tables():
    n_heads = GROUP_WIDTH // HEAD_DIM
    log_gamma = np.log1p(-np.exp2(-5.0 - np.arange(n_heads, dtype=np.float64)))
    idx = np.arange(RET_CHUNK, dtype=np.float64)
    rel = idx[:, None] - idx[None, :]
    dm = np.where(rel >= 0, np.exp(np.maximum(rel, 0.0) * log_gamma[:, None, None]), 0.0)
    dmask = np.stack([dm, dm.transpose(0, 2, 1)])
    dmask = dmask.reshape(2, N_GROUPS, HEADS_PER_GROUP * RET_CHUNK, RET_CHUNK)
    pos = np.stack([idx, RET_CHUNK - 1.0 - idx])
    lg = np.repeat(log_gamma, HEAD_DIM)
    qdec = np.exp((pos[:, :, None] + 1.0) * lg)
    kdec = np.exp((RET_CHUNK - 1.0 - pos[:, :, None]) * lg)
    cdec = np.exp(RET_CHUNK * lg)[None, :]
    return [jnp.asarray(a, F32) for a in (dmask, qdec, kdec, cdec)]


def _retention(u_ret, T):
    B = u_ret.shape[0]
    n_lat = T // RET_CHUNK
    ns = n_lat + 1
    dmask, qdec, kdec, cdec = _retention_tables()

    def fchunk(s):
        return s

    def bchunk(s):
        return jnp.where(s == 0, 0, ns - s)

    def col(c, chunk):
        return pl.BlockSpec((1, RET_CHUNK, GROUP_WIDTH), lambda b, s: (b, chunk(s), c))

    const = lambda shape: pl.BlockSpec(shape, lambda b, s: (0,) * len(shape))
    in_specs = ([col(c, fchunk) for c in (0, 1, 2, 3)] + [col(c, bchunk) for c in (0, 1, 2, 4)]
                + [const(dmask.shape), const(qdec.shape), const(kdec.shape), const(cdec.shape)])
    out_specs = [pl.BlockSpec((1, RET_CHUNK, GROUP_WIDTH), lambda b, s: (b, jnp.maximum(s - 1, 0), 0)),
                 pl.BlockSpec((1, RET_CHUNK, GROUP_WIDTH), lambda b, s: (b, jnp.where(s == 0, n_lat - 1, n_lat - s), 0))]
    out_shape = [jax.ShapeDtypeStruct((B, T, GROUP_WIDTH), F32)] * 2
    return pl.pallas_call(
        _ret_kernel,
        name="retention",
        grid=(B, ns),
        in_specs=in_specs,
        out_specs=out_specs,
        out_shape=out_shape,
        scratch_shapes=[pltpu.VMEM((N_GROUPS, GROUP_LANES, GROUP_LANES), F32)] * 2,
        compiler_params=_cparams(("arbitrary", "arbitrary")),
    )(*([u_ret] * 8), dmask, qdec, kdec, cdec)


def _rwkv_features(x, prev_row, next_row, cw_ref, w0_ref, w2_ref, a0_ref, a2_ref, g2f_ref, g2b_ref,
                   kk_ref, ka_ref, rk_ref,
                   r_o, k_o, v_o, a_o, b_o, lwf_o, lwb_o, gf_o, gb_o, bon_o):
    rows = lax.broadcasted_iota(jnp.int32, (TOK_TILE, 1), 0)
    xm = jnp.where(rows == 0, prev_row, pltpu.roll(x, 1, 0))
    xp = jnp.where(rows == TOK_TILE - 1, next_row, pltpu.roll(x, TOK_TILE - 1, 0))
    rw = cw_ref[0:1, :] * xm + cw_ref[1:2, :] * x + cw_ref[2:3, :] * xp
    yield

    W = GROUP_WIDTH
    r, k, v, lo = rw[:, 0:W], rw[:, W:2 * W], rw[:, 2 * W:3 * W], rw[:, 3 * W:4 * W]
    z = w0_ref[...] + _dot(jnp.tanh(lo[:, 0:LANES]), w2_ref[...])
    log_decay = -np.exp(-0.5).astype(np.float32) * jax.nn.sigmoid(z)
    iclr = jax.nn.sigmoid(a0_ref[...] + _dot(lo[:, LANES:2 * LANES], a2_ref[...]))
    yield
    g_f = _dot(jax.nn.sigmoid(lo[:, 2 * LANES:3 * LANES]), g2f_ref[...])
    g_b = _dot(jax.nn.sigmoid(lo[:, 3 * LANES:4 * LANES]), g2b_ref[...])
    kk = k * kk_ref[...]
    yield
    kk = kk / jnp.maximum(jnp.sqrt(_head_sum(kk * kk)), 1e-12)
    k_mod = k * (1.0 + (iclr - 1.0) * ka_ref[...])
    yield
    bonus = _head_sum(r * k_mod * rk_ref[...]) * v
    r_o[0] = r
    k_o[0] = k_mod
    v_o[0] = v
    a_o[0] = -kk
    b_o[0] = kk * iclr
    lwf_o[0] = log_decay[:, 0:W]
    lwb_o[0] = log_decay[:, W:2 * W]
    gf_o[0] = g_f
    gb_o[0] = g_b
    bon_o[0] = (g_f + g_b) * bonus


def _dplr_local(r, k, v, a, b, lw, reverse):
    C, H, GL = RWKV_CHUNK, HEADS_PER_GROUP, GROUP_LANES
    CS = H * C
    ri = lax.broadcasted_iota(jnp.int32, (C, C), 0)
    ci = lax.broadcasted_iota(jnp.int32, (C, C), 1)
    tri = jnp.where((ci >= ri) if reverse else (ci <= ri), 1.0, 0.0).astype(BF16)
    lc = jnp.dot(jnp.concatenate([tri, tri, tri], axis=1), jnp.concatenate(_split_bf16(lw, 3), axis=0),
                 preferred_element_type=F32)
    yield
    ltot = jnp.sum(lw, axis=0, keepdims=True)
    a_t = a * jnp.exp(lc - lw)
    r_t = r * jnp.exp(lc)
    inv = jnp.exp(-lc)
    b_t, k_t = b * inv, k * inv
    to_end = jnp.exp(ltot - lc)
    b_e, k_e = b * to_end, k * to_end

    rs = lax.broadcasted_iota(jnp.int32, (CS, CS), 0)
    cs = lax.broadcasted_iota(jnp.int32, (CS, CS), 1)
    bd = (rs // C) == (cs // C)
    stack = lambda x: jnp.where(bd, jnp.concatenate([x] * H, axis=0), 0.0)
    unstack = lambda x: x[0:C] + x[C:2 * C] + x[2 * C:3 * C] + x[3 * C:4 * C]
    a_s, r_s, v_s = stack(a_t), stack(r_t), stack(v)
    g = _dot_nt(jnp.concatenate([a_s, r_s], axis=0), jnp.concatenate([b_t, k_t], axis=0))
    g_swapped = pltpu.roll(g, C, 1)
    t128 = lax.broadcasted_iota(jnp.int32, (C, 2 * C), 0)
    l128 = lax.broadcasted_iota(jnp.int32, (C, 2 * C), 1)
    s128, half128 = l128 % C, l128 // C
    incl128 = (s128 >= t128) if reverse else (s128 <= t128)
    strict128 = (s128 > t128) if reverse else (s128 < t128)
    zeros128 = jnp.zeros((C, 2 * C), F32)

    def block_diagonal(row0, from_k, tri):
        blocks = []
        for hh in range(H):
            half = hh % 2
            src = g if half == int(from_k) else g_swapped
            piece = jnp.where(jnp.logical_and(half128 == half, tri), src[row0 + hh * C:row0 + (hh + 1) * C], 0.0)
            blocks.append(jnp.concatenate([piece if c == hh // 2 else zeros128 for c in range(H // 2)], axis=1))
        return jnp.concatenate(blocks, axis=0)

    low = block_diagonal(0, False, strict128)
    ak = block_diagonal(0, True, strict128)
    rb = block_diagonal(CS, False, incl128)
    rk = block_diagonal(CS, True, incl128)
    yield
    akv = _dot(ak, v_s)
    tm = jnp.where(rs == cs, 1.0, 0.0) + low
    pw = _dot(low, low)
    yield
    n = 2
    while n < C // 2:
        both = _dot(jnp.concatenate([pw, tm], axis=0), pw)
        pw, tm = both[0:CS], tm + both[CS:]
        n *= 2
        yield
    tm = tm + _dot(tm, pw)
    yield
    uw = _dot(tm, jnp.concatenate([akv, a_s], axis=1))
    yield
    y0_s = _dot(jnp.concatenate([rb, rk], axis=1), jnp.concatenate([uw[:, 0:GL], v_s], axis=0))
    qa_s = _dot(rb, uw[:, GL:])
    yield
    u0, w = unstack(uw[:, 0:GL]), unstack(uw[:, GL:])
    y0, q_add = unstack(y0_s), unstack(qa_s)
    zeros = jnp.zeros((C, GL), F32)
    stack_t = jnp.concatenate([w, u0, v, zeros], axis=0).T
    rm = jnp.concatenate([jnp.concatenate([b_e, zeros], axis=1),
                          jnp.concatenate([zeros, b_e], axis=1),
                          jnp.concatenate([zeros, k_e], axis=1),
                          jnp.concatenate([zeros, zeros], axis=1)], axis=0)
    mn = _dot(stack_t, rm)
    return (r_t + q_add, y0, jnp.exp(ltot), jnp.where(bd, mn[:, 0:GL], 0.0), jnp.where(bd, mn[:, GL:], 0.0))


def _scan_kernel(rf, kf, vf, af, bf, lwf, gf, rb, kb, vb, ab, bb, lwb, gb, lnw, lnb, of, ob, sf, sb):
    @pl.when(pl.program_id(1) == 0)
    def _():
        sf[...] = jnp.zeros_like(sf)
        sb[...] = jnp.zeros_like(sb)

    C = RWKV_CHUNK
    fwd = (False, (rf, kf, vf, af, bf, lwf), gf, of, sf)
    bwd = (True, (rb, kb, vb, ab, bb, lwb), gb, ob, sb)
    chains = []
    for reverse, refs, g_ref, o_ref, s_ref in (fwd, bwd):
        subs = range(RWKV_CHUNKS_PER_STEP)
        for gi in range(N_GROUPS):
            sl = slice(gi * GROUP_LANES, (gi + 1) * GROUP_LANES)
            for sub in (reversed(subs) if reverse else subs):
                rows = slice(sub * C, (sub + 1) * C)
                chains.append((gi, sl, rows, g_ref, o_ref, s_ref,
                               _dplr_local(*[ref[0, rows, sl] for ref in refs], reverse)))
    local = _round_robin([c[-1] for c in chains])

    def carry_state(s_ref, gi, parts):
        st = s_ref[gi]
        ys = []
        for q, y0, decay, m_t, n_t in parts:
            ys.append(_dot_nt(q, st) + y0)
            st = st * decay + _dot(st, m_t) + n_t
            yield
        s_ref[gi] = st
        return ys

    per = RWKV_CHUNKS_PER_STEP
    ys = _round_robin([carry_state(chains[i][5], chains[i][0], local[i:i + per]) for i in range(0, len(chains), per)])
    y_all = jnp.concatenate([y for group in ys for y in group], axis=0)
    dlt = y_all - _head_sum(y_all, 1.0 / HEAD_DIM)
    yn = dlt * lax.rsqrt(_head_sum(dlt * dlt, 1.0 / HEAD_DIM) + RWKV_GN_EPS)
    for i, (gi, sl, rows, g_ref, o_ref, s_ref, _) in enumerate(chains):
        o_ref[0, rows, sl] = g_ref[0, rows, sl] * (yn[i * C:(i + 1) * C] * lnw[:, sl] + lnb[:, sl])


def _rwkv_scan(r, k, v, a, b, lwf, lwb, gf, gb, lnw, lnb, T):
    B, S, W = r.shape
    C = RWKV_CHUNK * RWKV_CHUNKS_PER_STEP
    n_ctx = CTX_LEN // C
    n_lat = T // C
    ns = n_ctx + n_lat

    def fchunk(s):
        return s

    def bchunk(s):
        return jnp.where(s < n_ctx, n_ctx - 1 - s, ns + n_ctx - 1 - s)

    def fout(s):
        return jnp.maximum(s - n_ctx, 0)

    def bout(s):
        return jnp.where(s < n_ctx, n_lat - 1, ns - 1 - s)

    def spec(chunk):
        return pl.BlockSpec((1, C, W), lambda bi, s: (bi, chunk(s), 0))

    const = pl.BlockSpec((1, W), lambda bi, s: (0, 0))
    in_specs = [spec(fchunk)] * 6 + [spec(fout)] + [spec(bchunk)] * 6 + [spec(bout)] + [const, const]
    return pl.pallas_call(
        _scan_kernel,
        name="rwkv_scan",
        grid=(B, ns),
        in_specs=in_specs,
        out_specs=[spec(fout), spec(bout)],
        out_shape=[jax.ShapeDtypeStruct((B, T, W), F32)] * 2,
        scratch_shapes=[pltpu.VMEM((N_GROUPS, GROUP_LANES, GROUP_LANES), F32)] * 2,
        compiler_params=_cparams(("arbitrary", "arbitrary")),
    )(r, k, v, a, b, lwf, gf, r, k, v, a, b, lwb, gb, lnw, lnb)


def _out_kernel(x_ref, rf_ref, rb_ref, wf_ref, wb_ref, bon_ref, wo_ref, mod_ref, gains_ref, wr_ref,
                x1_ref, h2_ref, aff_ref):
    W = GROUP_WIDTH
    ret = rf_ref[0] + rb_ref[0]
    rwk = wf_ref[0] + wb_ref[0] + bon_ref[0]
    mix = _dot(ret, wo_ref[0:W, :]) + _dot(rwk, wo_ref[W:2 * W, :])
    m = mod_ref[0]
    D = D_MODEL
    ms = jnp.mean(mix * mix, axis=-1, keepdims=True)
    x1 = x_ref[0] + m[:, 2 * D:3 * D] * (mix * lax.rsqrt(ms + NORM_EPS) * gains_ref[1:2, :])
    x1_ref[0] = x1
    ms2 = jnp.mean(x1 * x1, axis=-1, keepdims=True)
    h2 = (x1 * lax.rsqrt(ms2 + NORM_EPS) * gains_ref[2:3, :]) * (1.0 + m[:, 4 * D:5 * D]) + m[:, 3 * D:4 * D]
    h2_ref[0] = h2.astype(BF16)
    h_hi, h_lo = _split_bf16(h2, 2)
    w_hi, w_lo = _split_bf16(wr_ref[...], 2)
    logits = lax.dot_general(jnp.concatenate([w_hi, w_lo, w_hi], axis=1), jnp.concatenate([h_hi, h_hi, h_lo], axis=1),
                             (((1,), (1,)), ((), ())), preferred_element_type=F32)
    e = jnp.exp(logits - jnp.max(logits, axis=0, keepdims=True))
    aff_ref[0] = e / jnp.sum(e, axis=0, keepdims=True)


def _out_projection(x, ret_f, ret_b, rw_f, rw_b, bonus, w_out, mods, gains, wr_pad):
    B, T, D = x.shape
    nt = T // OUT_TILE
    tok = lambda w: pl.BlockSpec((1, OUT_TILE, w), lambda b, i: (b, i, 0))
    const = lambda a: pl.BlockSpec(a.shape, lambda b, i: (0,) * a.ndim)
    return pl.pallas_call(
        _out_kernel,
        name="out_projection",
        grid=(B, nt),
        in_specs=[tok(D)] + [tok(GROUP_WIDTH)] * 5 + [const(w_out),
                  pl.BlockSpec((1, 1, mods.shape[2]), lambda b, i: (b, 0, 0)), const(gains), const(wr_pad)],
        out_specs=[tok(D), tok(D), pl.BlockSpec((1, N_EXPERTS, OUT_TILE), lambda b, i: (b, 0, i))],
        out_shape=[jax.ShapeDtypeStruct((B, T, D), F32), jax.ShapeDtypeStruct((B, T, D), BF16),
                   jax.ShapeDtypeStruct((B, N_EXPERTS, T), F32)],
        compiler_params=_cparams(("arbitrary", "arbitrary")),
    )(x, ret_f, ret_b, rw_f, rw_b, bonus, w_out, mods, gains, wr_pad)


def _cumsum_lanes(x):
    n = x.shape[1]
    lane = lax.broadcasted_iota(jnp.int32, (1, n), 1)
    sh = 1
    while sh < n:
        x = x + jnp.where(lane >= sh, pltpu.roll(x, sh, 1), 0.0)
        sh *= 2
    return x


def _select_kernel(cap, aff_ref, slot_ref, slot_t_ref, bounds_ref):
    aff = aff_ref[0]
    capf = jnp.float32(cap)

    def body(i, thr):
        cand = thr | (jnp.int32(1) << (30 - i))
        cnt = jnp.sum(jnp.where(aff >= pltpu.bitcast(cand, F32), 1.0, 0.0), axis=1, keepdims=True)
        return jnp.where(cnt >= capf, cand, thr)

    thr = lax.fori_loop(0, 31, body, jnp.zeros((aff.shape[0], 1), jnp.int32))
    min_normal_bits = jnp.int32(0x00800000)
    above = pltpu.bitcast(jnp.maximum(thr + 1, min_normal_bits), F32)
    gt = aff >= above
    eq = jnp.where(jnp.logical_and(aff >= pltpu.bitcast(thr, F32), jnp.logical_not(gt)), 1.0, 0.0)
    need = capf - jnp.sum(jnp.where(gt, 1.0, 0.0), axis=1, keepdims=True)
    eq_before = _cumsum_lanes(eq) - eq
    sel = jnp.where(jnp.logical_or(gt, jnp.logical_and(eq > 0.0, eq_before < need)), 1.0, 0.0)
    count = _cumsum_lanes(sel)
    slot = jnp.where(sel > 0.0, count - sel, -1.0)
    slot_ref[0] = slot
    pad = jnp.full((LANES - slot.shape[0], slot.shape[1]), -1.0, F32)
    slot_t_ref[0] = jnp.concatenate([slot, pad], axis=0).T

    n_experts, n_tok = slot.shape
    tok = lax.broadcasted_iota(jnp.int32, (1, n_tok), 1)
    lane = lax.broadcasted_iota(jnp.int32, (1, LANES), 1)
    bounds = jnp.zeros((n_experts, LANES), F32)
    for i in range(1, n_tok // BAND_TILE + 1):
        before = jnp.sum(jnp.where(tok == i * BAND_TILE - 1, count, 0.0), axis=1, keepdims=True)
        bounds = jnp.where(lane == i, before, bounds)
    bounds_ref[0] = bounds


def _select(aff, cap):
    B, E, T = aff.shape
    return pl.pallas_call(
        functools.partial(_select_kernel, cap),
        name="expert_select",
        grid=(B,),
        in_specs=[pl.BlockSpec((1, E, T), lambda b: (b, 0, 0))],
        out_specs=[pl.BlockSpec((1, E, T), lambda b: (b, 0, 0)), pl.BlockSpec((1, T, LANES), lambda b: (b, 0, 0)),
                   pl.BlockSpec((1, E, LANES), lambda b: (b, 0, 0))],
        out_shape=[jax.ShapeDtypeStruct((B, E, T), F32), jax.ShapeDtypeStruct((B, T, LANES), F32),
                   jax.ShapeDtypeStruct((B, E, LANES), F32)],
        compiler_params=_cparams(("arbitrary",)),
    )(aff)


def _gather_kernel(cap, band_ref, h_ref, slot_ref, xs_ref):
    b, ti = pl.program_id(0), pl.program_id(1)
    n_experts = slot_ref.shape[2]
    win = GATHER_WINDOW
    log_win = win.bit_length() - 1

    @pl.when(ti == 0)
    def _():
        xs_ref[...] = jnp.zeros_like(xs_ref)

    h = h_ref[0]
    jw = lax.broadcasted_iota(jnp.int32, (win, 1), 0)

    def count_before(e, tile):
        return band_ref[(b * n_experts + e) * BAND_COLS + tile]

    def one_hot(e, first):
        return jnp.where(slot_ref[0, 0, e:e + 1, :] == (first + jw).astype(F32), 1.0, 0.0).astype(BF16)

    def add_rows(e, first, rows):
        xs_ref[0, e, pl.ds(pl.multiple_of(first, ROW_PACK), win), :] += rows.astype(BF16)

    firsts = [(count_before(e, ti) >> 4) << 4 for e in range(n_experts)]
    rows = jnp.dot(jnp.concatenate([one_hot(e, firsts[e]) for e in range(n_experts)], axis=0), h,
                   preferred_element_type=F32)
    for e in range(n_experts):
        add_rows(e, firsts[e], rows[e * win:(e + 1) * win])
    for e in range(n_experts):
        n_win = (count_before(e, ti + 1) - firsts[e] + win - 1) >> log_win

        def more(w, carry, e=e):
            first = firsts[e] + w * win
            add_rows(e, first, jnp.dot(one_hot(e, first), h, preferred_element_type=F32))
            return carry

        lax.fori_loop(1, n_win, more, 0)


def _expert_gather(band, h2, slot, cap):
    B, T, D = h2.shape
    E = slot.shape[1]
    nt = T // BAND_TILE
    slot_tiles = slot.reshape(B, E, nt, BAND_TILE).transpose(0, 2, 1, 3)
    rows = cap + GATHER_WINDOW
    grid_spec = pltpu.PrefetchScalarGridSpec(
        num_scalar_prefetch=1,
        grid=(B, nt),
        in_specs=[pl.BlockSpec((1, BAND_TILE, D), lambda b, t, band: (b, t, 0)),
                  pl.BlockSpec((1, 1, E, BAND_TILE), lambda b, t, band: (b, t, 0, 0))],
        out_specs=pl.BlockSpec((1, E, rows, D), lambda b, t, band: (b, 0, 0, 0)))
    return pl.pallas_call(
        functools.partial(_gather_kernel, cap),
        name="expert_gather",
        grid_spec=grid_spec,
        out_shape=jax.ShapeDtypeStruct((B, E, rows, D), BF16),
        compiler_params=_cparams(("arbitrary", "arbitrary")),
    )(band, h2, slot_tiles)


def _expert_kernel(cap, xs_ref, slot_ref, aff_ref, wg_ref, wu_ref, wd_ref, ye_ref):
    jj = lax.broadcasted_iota(jnp.int32, (cap, 1), 0).astype(F32)
    gate = jnp.zeros((cap, 1), F32)
    for ti in range(slot_ref.shape[2]):
        hit = slot_ref[0, 0, ti:ti + 1, :] == jj
        gate = gate + jnp.sum(jnp.where(hit, aff_ref[0, 0, ti:ti + 1, :], 0.0), axis=1, keepdims=True)
    xs = xs_ref[0, 0]
    hid = _silu(_dot(xs, wg_ref[0])) * _dot(xs, wu_ref[0])
    ye = _dot(hid, wd_ref[0]) * gate
    ye_ref[0, 0] = ye.astype(BF16)


def _expert_ffn(xs, slot, aff, wg, wu, wd, cap):
    B, E, _, D = xs.shape
    T = slot.shape[2]
    nt = T // BAND_TILE
    row = pl.BlockSpec((1, 1, nt, BAND_TILE), lambda b, e: (b, e, 0, 0))
    wspec = pl.BlockSpec((1, D, D), lambda b, e: (e, 0, 0))
    return pl.pallas_call(
        functools.partial(_expert_kernel, cap),
        name="expert_ffn",
        grid=(B, E),
        in_specs=[pl.BlockSpec((1, 1, cap, D), lambda b, e: (b, e, 0, 0)), row, row, wspec, wspec, wspec],
        out_specs=pl.BlockSpec((1, 1, cap, D), lambda b, e: (b, e, 0, 0)),
        out_shape=jax.ShapeDtypeStruct((B, E, cap, D), BF16),
        compiler_params=_cparams(("arbitrary", "arbitrary")),
    )(xs, slot.reshape(B, E, nt, BAND_TILE), aff.reshape(B, E, nt, BAND_TILE), wg, wu, wd)


def _combine_kernel(cap, band_ref, x1_ref, slot_t_ref, ye_ref, mod_ref, gain_ref, o_ref, acc_ref):
    win = min(SLOT_WINDOW, cap)
    log_win = win.bit_length() - 1
    b, i = pl.program_id(0), pl.program_id(1)
    tiles_per = OUT_TILE // BAND_TILE
    st = slot_t_ref[0]
    jw = lax.broadcasted_iota(jnp.int32, (1, win), 1)

    def slots_before(e, tile):
        return band_ref[(b * N_EXPERTS + e) * BAND_COLS + tile * tiles_per]

    def window(e, w):
        first = ((slots_before(e, i) >> 4) << 4) + w * win
        src = pl.multiple_of(jnp.minimum(first, cap - win), ROW_PACK)
        col = st[:, e:e + 1]
        hit = jnp.logical_and(col == (src + jw).astype(F32), col >= first.astype(F32))
        return jnp.where(hit, 1.0, 0.0).astype(BF16), ye_ref[0, e, pl.ds(src, win), :]

    acc = jnp.zeros((OUT_TILE, D_MODEL), F32)
    for e in range(0, N_EXPERTS, 2):
        (h0, r0), (h1, r1) = window(e, 0), window(e + 1, 0)
        acc = acc + jnp.dot(jnp.concatenate([h0, h1], axis=1), jnp.concatenate([r0, r1], axis=0),
                            preferred_element_type=F32)
    acc_ref[...] = acc
    for e in range(N_EXPERTS):
        first = (slots_before(e, i) >> 4) << 4
        n_win = (slots_before(e, i + 1) - first + win - 1) >> log_win

        def more(w, carry, e=e):
            hit, rows = window(e, w)
            acc_ref[...] += jnp.dot(hit, rows, preferred_element_type=F32)
            return carry

        lax.fori_loop(1, n_win, more, 0)
    acc = acc_ref[...]
    ms = jnp.mean(acc * acc, axis=-1, keepdims=True)
    gt2 = mod_ref[0][:, 5 * D_MODEL:6 * D_MODEL]
    o_ref[0] = x1_ref[0] + gt2 * (acc * lax.rsqrt(ms + NORM_EPS) * gain_ref[...])


def _combine(band, x1, slot_t, ye, mods, gain, cap):
    B, T, D = x1.shape
    E = ye.shape[1]
    grid_spec = pltpu.PrefetchScalarGridSpec(
        num_scalar_prefetch=1,
        grid=(B, T // OUT_TILE),
        in_specs=[pl.BlockSpec((1, OUT_TILE, D), lambda b, i, band: (b, i, 0)),
                  pl.BlockSpec((1, OUT_TILE, LANES), lambda b, i, band: (b, i, 0)),
                  pl.BlockSpec((1, E, cap, D), lambda b, i, band: (b, 0, 0, 0)),
                  pl.BlockSpec((1, 1, mods.shape[2]), lambda b, i, band: (b, 0, 0)),
                  pl.BlockSpec((1, D), lambda b, i, band: (0, 0))],
        out_specs=pl.BlockSpec((1, OUT_TILE, D), lambda b, i, band: (b, i, 0)),
        scratch_shapes=[pltpu.VMEM((OUT_TILE, D), F32)])
    return pl.pallas_call(
        functools.partial(_combine_kernel, cap),
        name="combine",
        grid_spec=grid_spec,
        out_shape=jax.ShapeDtypeStruct((B, T, D), F32),
        compiler_params=_cparams(("arbitrary", "arbitrary")),
    )(band, x1, slot_t, ye, mods, gain)


def _rope_tables(T):
    rows = T // GRID_W
    row = np.repeat(np.arange(rows, dtype=np.float32), GRID_W)
    col = np.tile(np.arange(GRID_W, dtype=np.float32), rows)
    n_freq = HEAD_DIM // 4
    freq = jnp.asarray(ROPE_BASE, F32) ** (-jnp.arange(n_freq, dtype=F32) / n_freq)
    ang = jnp.concatenate([row[:, None] * freq, col[:, None] * freq], axis=-1)
    cos, sin = jnp.cos(ang), jnp.sin(ang)
    reps = LANES // (HEAD_DIM // 2)
    cos_t = jnp.tile(cos, (1, reps))
    sin_t = jnp.tile(jnp.concatenate([-sin, sin], axis=-1), (1, reps // 2))
    return cos_t, sin_t


def kernel(x, c, ctx, c_ctx, w_mod, b_mod, norm_gains, w_in, rwkv_conv, rwkv_w0, rwkv_w2, rwkv_a0, rwkv_a2, rwkv_g2, rwkv_k_k, rwkv_k_a, rwkv_r_k, rwkv_lnx_w, rwkv_lnx_b, w_out, w_router, w_gate, w_up, w_down):
    B, T, D = x.shape
    assert D == D_MODEL and ctx.shape == (B, CTX_LEN, D) and T % OUT_TILE == 0 and w_mod.shape[0] == 1
    W = GROUP_WIDTH
    cap = max(1, EC_CAPACITY * T // N_EXPERTS)
    assert T // BAND_TILE < BAND_COLS

    n_rows = -(-(B + 1) // 8) * 8
    cc = jnp.concatenate([c, c_ctx[None, :], jnp.zeros((n_rows - B - 1, D), F32)], axis=0)
    mods = _modulation(cc, w_mod[0], b_mod).reshape(n_rows, 1, 6 * D)

    split = RET_COLS + 3 * W + 2 * DECAY_LORA + ICLR_LORA
    w_pad = jnp.concatenate([w_in[0][:, :split], jnp.zeros((D, LANES - ICLR_LORA), F32), w_in[0][:, split:]],
                            axis=1).astype(BF16)
    cs = split - RET_COLS
    cw = jnp.concatenate([rwkv_conv[0][:, :cs], jnp.zeros((3, LANES - ICLR_LORA), F32), rwkv_conv[0][:, cs:]], axis=1)
    zl = jnp.zeros((DECAY_LORA, W), F32)
    w2bd = jnp.concatenate([jnp.concatenate([rwkv_w2[0, 0], zl], axis=1),
                            jnp.concatenate([zl, rwkv_w2[0, 1]], axis=1)], axis=0).astype(BF16)
    w0cat = jnp.concatenate([rwkv_w0[0, 0], rwkv_w0[0, 1]])[None, :]
    a2pad = jnp.concatenate([rwkv_a2[0], jnp.zeros((LANES - ICLR_LORA, W), F32)], axis=0).astype(BF16)
    g2f, g2b = rwkv_g2[0, 0].astype(BF16), rwkv_g2[0, 1].astype(BF16)
    row = lambda a: a[0][None, :]
    cos_t, sin_t = _rope_tables(T)

    feature_consts = (cw, w0cat, w2bd, row(rwkv_a0), a2pad, g2f, g2b, row(rwkv_k_k), row(rwkv_k_a), row(rwkv_r_k))
    u_ret, r, k, v, a, b, lwf, lwb, gf, gb, bonus = _projection(
        x, ctx, mods, norm_gains[0, 0][None, :], w_pad, cos_t, sin_t, feature_consts)
    ret_f, ret_b = _retention(u_ret, T)
    rw_f, rw_b = _rwkv_scan(r, k, v, a, b, lwf, lwb, gf, gb, row(rwkv_lnx_w), row(rwkv_lnx_b), T)

    x1, h2, aff = _out_projection(x, ret_f, ret_b, rw_f, rw_b, bonus, w_out[0].astype(BF16), mods,
                                  norm_gains[0], w_router[0].T)
    slot, slot_t, bounds = _select(aff, cap)
    band = bounds[:, :, :BAND_COLS].astype(jnp.int32).reshape(-1)
    xs = _expert_gather(band, h2, slot, cap)
    ye = _expert_ffn(xs, slot, aff, w_gate[0], w_up[0], w_down[0], cap)
    return _combine(band, x1, slot_t, ye, mods, norm_gains[0, 3][None, :], cap)
```

```python
import functools

import numpy as np
import jax
import jax.numpy as jnp
from jax import lax
from jax.experimental import pallas as pl
from jax.experimental.pallas import tpu as pltpu

F32 = jnp.float32
BF16 = jnp.bfloat16
HIGHEST = lax.Precision.HIGHEST

D_MODEL = 1024
CTX_LEN = 256
GRID_W = 64
HEAD_DIM = 64
GROUP_WIDTH = 512
LANES = 128
N_PAIRS = GROUP_WIDTH // LANES
HEADS_PER_GROUP = 4
GROUP_LANES = HEADS_PER_GROUP * HEAD_DIM
N_GROUPS = GROUP_WIDTH // GROUP_LANES
RET_COLS = 5 * GROUP_WIDTH
RWKV_COLS_PAD = 4 * GROUP_WIDTH
DECAY_LORA = 64
ICLR_LORA = 64
GATE_LORA = 128
N_EXPERTS = 16
EC_CAPACITY = 2
ROPE_BASE = 10000.0
NORM_EPS = 1e-6
RWKV_GN_EPS = 64e-5

TOK_TILE = 256
RET_CHUNK = 256
RWKV_CHUNK = 64
RWKV_CHUNKS_PER_STEP = 4
OUT_TILE = 512
BAND_TILE = 256
ROW_PACK = 16
GATHER_WINDOW = 64
SLOT_WINDOW = 128
BAND_COLS = 32
VMEM_LIMIT = 56 * 1024 * 1024


def _cparams(sem):
    return pltpu.CompilerParams(dimension_semantics=sem, vmem_limit_bytes=VMEM_LIMIT)


def _dot(a, b):
    return jnp.dot(a.astype(BF16), b.astype(BF16), preferred_element_type=F32)


def _dot_nt(a, b):
    return lax.dot_general(a.astype(BF16), b.astype(BF16), (((1,), (1,)), ((), ())),
                           preferred_element_type=F32)


def _dot_exact(a, b):
    return jnp.dot(a, b, precision=HIGHEST, preferred_element_type=F32)


def _head_block_mask(n):
    r = lax.broadcasted_iota(jnp.int32, (n, n), 0)
    c = lax.broadcasted_iota(jnp.int32, (n, n), 1)
    return (r // HEAD_DIM) == (c // HEAD_DIM)


def _split_bf16(x, terms):
    out = []
    for _ in range(terms - 1):
        hi = x.astype(BF16)
        out.append(hi)
        x = x - hi.astype(F32)
    out.append(x.astype(BF16))
    return out


def _head_sum(x, scale=1.0):
    cols = []
    for g in range(x.shape[1] // GROUP_LANES):
        m = jnp.where(_head_block_mask(GROUP_LANES), scale, 0.0).astype(BF16)
        parts = _split_bf16(x[:, g * GROUP_LANES:(g + 1) * GROUP_LANES], 2)
        cols.append(jnp.dot(jnp.concatenate(parts, axis=1), jnp.concatenate([m, m], axis=0),
                            preferred_element_type=F32))
    return cols[0] if len(cols) == 1 else jnp.concatenate(cols, axis=1)


def _silu(x):
    return x * jax.nn.sigmoid(x)


def _round_robin(gens):
    results = [None] * len(gens)
    active = list(range(len(gens)))
    while active:
        for i in list(active):
            try:
                next(gens[i])
            except StopIteration as done:
                results[i] = done.value
                active.remove(i)
    return results


def _mod_kernel(c_ref, w_ref, b_ref, o_ref):
    o_ref[...] = _dot_exact(_silu(c_ref[...]), w_ref[...]) + b_ref[...]


def _modulation(cc, w_mod, b_mod):
    rows, d = cc.shape
    n = w_mod.shape[1]
    tn = 1536
    return pl.pallas_call(
        _mod_kernel,
        name="modulation",
        grid=(n // tn,),
        in_specs=[pl.BlockSpec((rows, d), lambda i: (0, 0)),
                  pl.BlockSpec((d, tn), lambda i: (0, i)),
                  pl.BlockSpec((1, tn), lambda i: (0, i))],
        out_specs=pl.BlockSpec((rows, tn), lambda i: (0, i)),
        out_shape=jax.ShapeDtypeStruct((rows, n), F32),
        compiler_params=_cparams(("arbitrary",)),
    )(cc, w_mod, b_mod)


def _proj_kernel(n_tiles, x_ref, ctx_ref, xp_ref, xn_ref, mod_ref, gain_ref, w_ref, cos_ref, sin_ref, *rest):
    feature_consts, ret_ref, feature_outs = rest[:10], rest[10], rest[11:]
    j = pl.program_id(1)
    is_lat = j > 0
    xin = jnp.concatenate([jnp.where(is_lat, x_ref[0], ctx_ref[0]), xp_ref[0], xn_ref[0]], axis=0)
    ms = jnp.mean(xin * xin, axis=-1, keepdims=True)
    y = xin * lax.rsqrt(ms + NORM_EPS) * gain_ref[...]
    m = mod_ref[0]
    h = (y * (1.0 + m[:, D_MODEL:2 * D_MODEL]) + m[:, 0:D_MODEL]).astype(BF16)
    u_rw = jnp.dot(h, w_ref[:, RET_COLS:], preferred_element_type=F32)
    has_prev = (j >= 2).astype(F32)
    has_next = jnp.logical_and(j >= 1, j <= n_tiles - 2).astype(F32)
    features = _rwkv_features(u_rw[0:TOK_TILE], u_rw[TOK_TILE + 7:TOK_TILE + 8] * has_prev,
                              u_rw[TOK_TILE + 8:TOK_TILE + 9] * has_next, *feature_consts, *feature_outs)

    def retention_columns():
        lat = is_lat.astype(F32)
        cos = cos_ref[...] * lat + (1.0 - lat)
        sin = sin_ref[...] * lat
        lane = lax.broadcasted_iota(jnp.int32, (1, LANES), 1)
        first_half = (lane % HEAD_DIM) < (HEAD_DIM // 2)
        W = GROUP_WIDTH
        for c, scale in enumerate((1.0, HEAD_DIM ** -0.5, None, None, None)):
            u = jnp.dot(h[0:TOK_TILE], w_ref[:, c * W:(c + 1) * W], preferred_element_type=F32)
            yield
            if scale is None:
                ret_ref[0, :, c * W:(c + 1) * W] = u
                continue
            for g in range(N_PAIRS):
                t = u[:, g * LANES:(g + 1) * LANES] * scale
                sw = jnp.where(first_half, pltpu.roll(t, LANES - HEAD_DIM // 2, 1), pltpu.roll(t, HEAD_DIM // 2, 1))
                ret_ref[0, :, c * W + g * LANES:c * W + (g + 1) * LANES] = t * cos + sw * sin

    _round_robin([features, retention_columns()])


def _projection(x, ctx, mods, gain, w_pad, cos_t, sin_t, feature_consts):
    B, T, D = x.shape
    nt = (T + CTX_LEN) // TOK_TILE
    S = T + CTX_LEN
    rb = TOK_TILE // 8
    const = lambda a: pl.BlockSpec(a.shape, lambda b, j: (0,) * a.ndim)
    seq_spec = pl.BlockSpec((1, TOK_TILE, GROUP_WIDTH), lambda b, j: (b, j, 0))
    lat_spec = pl.BlockSpec((1, TOK_TILE, GROUP_WIDTH), lambda b, j: (b, jnp.maximum(j - 1, 0), 0))
    seq_shape = jax.ShapeDtypeStruct((B, S, GROUP_WIDTH), F32)
    lat_shape = jax.ShapeDtypeStruct((B, T, GROUP_WIDTH), BF16)
    return pl.pallas_call(
        functools.partial(_proj_kernel, nt),
        name="projection",
        grid=(B, nt),
        in_specs=[
            pl.BlockSpec((1, TOK_TILE, D), lambda b, j: (b, jnp.maximum(j - 1, 0), 0)),
            pl.BlockSpec((1, TOK_TILE, D), lambda b, j: (b, 0, 0)),
            pl.BlockSpec((1, 8, D), lambda b, j: (b, jnp.maximum((j - 1) * rb - 1, 0), 0)),
            pl.BlockSpec((1, 8, D), lambda b, j: (b, jnp.clip(j * rb, 0, T // 8 - 1), 0)),
            pl.BlockSpec((1, 1, mods.shape[2]), lambda b, j: (jnp.where(j == 0, B, b), 0, 0)),
            pl.BlockSpec((1, D), lambda b, j: (0, 0)),
            const(w_pad),
            pl.BlockSpec((TOK_TILE, LANES), lambda b, j: (jnp.maximum(j - 1, 0), 0)),
            pl.BlockSpec((TOK_TILE, LANES), lambda b, j: (jnp.maximum(j - 1, 0), 0)),
        ] + [const(a) for a in feature_consts],
        out_specs=[pl.BlockSpec((1, TOK_TILE, RET_COLS), lambda b, j: (b, j, 0))] + [seq_spec] * 7 + [lat_spec] * 3,
        out_shape=[jax.ShapeDtypeStruct((B, S, RET_COLS), F32)] + [seq_shape] * 7 + [lat_shape] * 3,
        compiler_params=_cparams(("arbitrary", "arbitrary")),
    )(x, ctx, x, x, mods, gain, w_pad, cos_t, sin_t, *feature_consts)


def _ret_group(d, gi, q_ref, k_ref, v_ref, g_ref, o_ref, s_ref, dmask_ref, qdec_ref, kdec_ref, cdec_ref):
    C = RET_CHUNK
    lane_head = lax.broadcasted_iota(jnp.int32, (1, GROUP_LANES), 1) // HEAD_DIM
    sl = slice(gi * GROUP_LANES, (gi + 1) * GROUP_LANES)
    q4, k4, v4 = q_ref[0, :, sl], k_ref[0, :, sl], v_ref[0, :, sl]
    state = s_ref[gi]
    q_s = jnp.concatenate([jnp.where(lane_head == hh, q4, 0.0) for hh in range(HEADS_PER_GROUP)], axis=0)
    scores = _dot_nt(q_s, k4) * dmask_ref[d, gi]
    cross = _dot(q4 * qdec_ref[d, :, sl], state)
    yield
    o_s = _dot(scores, v4)
    kd = (k4 * kdec_ref[d, :, sl]).T
    s_ref[gi] = state * cdec_ref[:, sl] + jnp.where(_head_block_mask(GROUP_LANES), _dot(kd, v4), 0.0)
    yield
    y = cross + jnp.where(lane_head == 0, o_s[0:C], 0.0)
    for hh in range(1, HEADS_PER_GROUP):
        y = y + jnp.where(lane_head == hh, o_s[hh * C:(hh + 1) * C], 0.0)
    ms = _head_sum(y * y, 1.0 / HEAD_DIM)
    o_ref[0, :, sl] = (_silu(g_ref[0, :, sl]) * (y * lax.rsqrt(ms + NORM_EPS))).astype(o_ref.dtype)


def _ret_kernel(qf, kf, vf, gf, qb, kb, vb, gb, dmask, qdec, kdec, cdec, of, ob, sf, sb):
    @pl.when(pl.program_id(1) == 0)
    def _():
        sf[...] = jnp.zeros_like(sf)
        sb[...] = jnp.zeros_like(sb)

    tables = (dmask, qdec, kdec, cdec)
    _round_robin([_ret_group(d, gi, *refs, *tables)
                  for d, refs in ((0, (qf, kf, vf, gf, of, sf)), (1, (qb, kb, vb, gb, ob, sb)))
                  for gi in range(N_GROUPS)])


def _retention_tables():
    n_heads = GROUP_WIDTH // HEAD_DIM
    log_gamma = np.log1p(-np.exp2(-5.0 - np.arange(n_heads, dtype=np.float64)))
    idx = np.arange(RET_CHUNK, dtype=np.float64)
    rel = idx[:, None] - idx[None, :]
    dm = np.where(rel >= 0, np.exp(np.maximum(rel, 0.0) * log_gamma[:, None, None]), 0.0)
    dmask = np.stack([dm, dm.transpose(0, 2, 1)])
    dmask = dmask.reshape(2, N_GROUPS, HEADS_PER_GROUP * RET_CHUNK, RET_CHUNK)
    pos = np.stack([idx, RET_CHUNK - 1.0 - idx])
    lg = np.repeat(log_gamma, HEAD_DIM)
    qdec = np.exp((pos[:, :, None] + 1.0) * lg)
    kdec = np.exp((RET_CHUNK - 1.0 - pos[:, :, None]) * lg)
    cdec = np.exp(RET_CHUNK * lg)[None, :]
    return [jnp.asarray(a, F32) for a in (dmask, qdec, kdec, cdec)]


def _retention(u_ret, T):
    B = u_ret.shape[0]
    n_lat = T // RET_CHUNK
    ns = n_lat + 1
    dmask, qdec, kdec, cdec = _retention_tables()

    def fchunk(s):
        return s

    def bchunk(s):
        return jnp.where(s == 0, 0, ns - s)

    def col(c, chunk):
        return pl.BlockSpec((1, RET_CHUNK, GROUP_WIDTH), lambda b, s: (b, chunk(s), c))

    const = lambda shape: pl.BlockSpec(shape, lambda b, s: (0,) * len(shape))
    in_specs = ([col(c, fchunk) for c in (0, 1, 2, 3)] + [col(c, bchunk) for c in (0, 1, 2, 4)]
                + [const(dmask.shape), const(qdec.shape), const(kdec.shape), const(cdec.shape)])
    out_specs = [pl.BlockSpec((1, RET_CHUNK, GROUP_WIDTH), lambda b, s: (b, jnp.maximum(s - 1, 0), 0)),
                 pl.BlockSpec((1, RET_CHUNK, GROUP_WIDTH), lambda b, s: (b, jnp.where(s == 0, n_lat - 1, n_lat - s), 0))]
    out_shape = [jax.ShapeDtypeStruct((B, T, GROUP_WIDTH), BF16)] * 2
    return pl.pallas_call(
        _ret_kernel,
        name="retention",
        grid=(B, ns),
        in_specs=in_specs,
        out_specs=out_specs,
        out_shape=out_shape,
        scratch_shapes=[pltpu.VMEM((N_GROUPS, GROUP_LANES, GROUP_LANES), F32)] * 2,
        compiler_params=_cparams(("arbitrary", "arbitrary")),
    )(*([u_ret] * 8), dmask, qdec, kdec, cdec)


def _rwkv_features(x, prev_row, next_row, cw_ref, w0_ref, w2_ref, a0_ref, a2_ref, g2f_ref, g2b_ref,
                   kk_ref, ka_ref, rk_ref,
                   r_o, k_o, v_o, a_o, b_o, lwf_o, lwb_o, gf_o, gb_o, bon_o):
    rows = lax.broadcasted_iota(jnp.int32, (TOK_TILE, 1), 0)
    xm = jnp.where(rows == 0, prev_row, pltpu.roll(x, 1, 0))
    xp = jnp.where(rows == TOK_TILE - 1, next_row, pltpu.roll(x, TOK_TILE - 1, 0))
    rw = cw_ref[0:1, :] * xm + cw_ref[1:2, :] * x + cw_ref[2:3, :] * xp
    yield

    W = GROUP_WIDTH
    r, k, v, lo = rw[:, 0:W], rw[:, W:2 * W], rw[:, 2 * W:3 * W], rw[:, 3 * W:4 * W]
    z = w0_ref[...] + _dot(jnp.tanh(lo[:, 0:LANES]), w2_ref[...])
    log_decay = -np.exp(-0.5).astype(np.float32) * jax.nn.sigmoid(z)
    iclr = jax.nn.sigmoid(a0_ref[...] + _dot(lo[:, LANES:2 * LANES], a2_ref[...]))
    yield
    g_f = _dot(jax.nn.sigmoid(lo[:, 2 * LANES:3 * LANES]), g2f_ref[...])
    g_b = _dot(jax.nn.sigmoid(lo[:, 3 * LANES:4 * LANES]), g2b_ref[...])
    kk = k * kk_ref[...]
    yield
    kk = kk / jnp.maximum(jnp.sqrt(_head_sum(kk * kk)), 1e-12)
    k_mod = k * (1.0 + (iclr - 1.0) * ka_ref[...])
    yield
    bonus = _head_sum(r * k_mod * rk_ref[...]) * v
    r_o[0] = r
    k_o[0] = k_mod
    v_o[0] = v
    a_o[0] = -kk
    b_o[0] = kk * iclr
    lwf_o[0] = log_decay[:, 0:W]
    lwb_o[0] = log_decay[:, W:2 * W]
    gf_o[0] = g_f.astype(gf_o.dtype)
    gb_o[0] = g_b.astype(gb_o.dtype)
    bon_o[0] = ((g_f + g_b) * bonus).astype(bon_o.dtype)


def _dplr_local(r, k, v, a, b, lw, reverse):
    C, H, GL = RWKV_CHUNK, HEADS_PER_GROUP, GROUP_LANES
    CS = H * C
    ri = lax.broadcasted_iota(jnp.int32, (C, C), 0)
    ci = lax.broadcasted_iota(jnp.int32, (C, C), 1)
    tri = jnp.where((ci >= ri) if reverse else (ci <= ri), 1.0, 0.0).astype(BF16)
    lc = jnp.dot(jnp.concatenate([tri, tri, tri], axis=1), jnp.concatenate(_split_bf16(lw, 3), axis=0),
                 preferred_element_type=F32)
    yield
    ltot = jnp.sum(lw, axis=0, keepdims=True)
    a_t = a * jnp.exp(lc - lw)
    r_t = r * jnp.exp(lc)
    inv = jnp.exp(-lc)
    b_t, k_t = b * inv, k * inv
    to_end = jnp.exp(ltot - lc)
    b_e, k_e = b * to_end, k * to_end

    rs = lax.broadcasted_iota(jnp.int32, (CS, CS), 0)
    cs = lax.broadcasted_iota(jnp.int32, (CS, CS), 1)
    bd = (rs // C) == (cs // C)
    stack = lambda x: jnp.where(bd, jnp.concatenate([x] * H, axis=0), 0.0)
    unstack = lambda x: x[0:C] + x[C:2 * C] + x[2 * C:3 * C] + x[3 * C:4 * C]
    a_s, r_s, v_s = stack(a_t), stack(r_t), stack(v)
    g = _dot_nt(jnp.concatenate([a_s, r_s], axis=0), jnp.concatenate([b_t, k_t], axis=0))
    g_swapped = pltpu.roll(g, C, 1)
    t128 = lax.broadcasted_iota(jnp.int32, (C, 2 * C), 0)
    l128 = lax.broadcasted_iota(jnp.int32, (C, 2 * C), 1)
    s128, half128 = l128 % C, l128 // C
    incl128 = (s128 >= t128) if reverse else (s128 <= t128)
    strict128 = (s128 > t128) if reverse else (s128 < t128)
    zeros128 = jnp.zeros((C, 2 * C), F32)

    def block_diagonal(row0, from_k, tri):
        blocks = []
        for hh in range(H):
            half = hh % 2
            src = g if half == int(from_k) else g_swapped
            piece = jnp.where(jnp.logical_and(half128 == half, tri), src[row0 + hh * C:row0 + (hh + 1) * C], 0.0)
            blocks.append(jnp.concatenate([piece if c == hh // 2 else zeros128 for c in range(H // 2)], axis=1))
        return jnp.concatenate(blocks, axis=0)

    low = block_diagonal(0, False, strict128)
    ak = block_diagonal(0, True, strict128)
    rb = block_diagonal(CS, False, incl128)
    rk = block_diagonal(CS, True, incl128)
    yield
    akv = _dot(ak, v_s)
    tm = jnp.where(rs == cs, 1.0, 0.0) + low
    pw = _dot(low, low)
    yield
    n = 2
    while n < C // 2:
        both = _dot(jnp.concatenate([pw, tm], axis=0), pw)
        pw, tm = both[0:CS], tm + both[CS:]
        n *= 2
        yield
    tm = tm + _dot(tm, pw)
    yield
    uw = _dot(tm, jnp.concatenate([akv, a_s], axis=1))
    yield
    y0_s = _dot(jnp.concatenate([rb, rk], axis=1), jnp.concatenate([uw[:, 0:GL], v_s], axis=0))
    qa_s = _dot(rb, uw[:, GL:])
    yield
    u0, w = unstack(uw[:, 0:GL]), unstack(uw[:, GL:])
    y0, q_add = unstack(y0_s), unstack(qa_s)
    zeros = jnp.zeros((C, GL), F32)
    stack_t = jnp.concatenate([w, u0, v, zeros], axis=0).T
    rm = jnp.concatenate([jnp.concatenate([b_e, zeros], axis=1),
                          jnp.concatenate([zeros, b_e], axis=1),
                          jnp.concatenate([zeros, k_e], axis=1),
                          jnp.concatenate([zeros, zeros], axis=1)], axis=0)
    mn = _dot(stack_t, rm)
    return (r_t + q_add, y0, jnp.exp(ltot), jnp.where(bd, mn[:, 0:GL], 0.0), jnp.where(bd, mn[:, GL:], 0.0))


def _scan_kernel(rf, kf, vf, af, bf, lwf, gf, rb, kb, vb, ab, bb, lwb, gb, lnw, lnb, of, ob, sf, sb):
    @pl.when(pl.program_id(1) == 0)
    def _():
        sf[...] = jnp.zeros_like(sf)
        sb[...] = jnp.zeros_like(sb)

    C = RWKV_CHUNK
    fwd = (False, (rf, kf, vf, af, bf, lwf), gf, of, sf)
    bwd = (True, (rb, kb, vb, ab, bb, lwb), gb, ob, sb)
    chains = []
    for reverse, refs, g_ref, o_ref, s_ref in (fwd, bwd):
        subs = range(RWKV_CHUNKS_PER_STEP)
        for gi in range(N_GROUPS):
            sl = slice(gi * GROUP_LANES, (gi + 1) * GROUP_LANES)
            for sub in (reversed(subs) if reverse else subs):
                rows = slice(sub * C, (sub + 1) * C)
                chains.append((gi, sl, rows, g_ref, o_ref, s_ref,
                               _dplr_local(*[ref[0, rows, sl] for ref in refs], reverse)))
    local = _round_robin([c[-1] for c in chains])

    def carry_state(s_ref, gi, parts):
        st = s_ref[gi]
        ys = []
        for q, y0, decay, m_t, n_t in parts:
            ys.append(_dot_nt(q, st) + y0)
            st = st * decay + _dot(st, m_t) + n_t
            yield
        s_ref[gi] = st
        return ys

    per = RWKV_CHUNKS_PER_STEP
    ys = _round_robin([carry_state(chains[i][5], chains[i][0], local[i:i + per]) for i in range(0, len(chains), per)])
    y_all = jnp.concatenate([y for group in ys for y in group], axis=0)
    dlt = y_all - _head_sum(y_all, 1.0 / HEAD_DIM)
    yn = dlt * lax.rsqrt(_head_sum(dlt * dlt, 1.0 / HEAD_DIM) + RWKV_GN_EPS)
    for i, (gi, sl, rows, g_ref, o_ref, s_ref, _) in enumerate(chains):
        out = g_ref[0, rows, sl].astype(F32) * (yn[i * C:(i + 1) * C] * lnw[:, sl] + lnb[:, sl])
        o_ref[0, rows, sl] = out.astype(o_ref.dtype)


def _rwkv_scan(r, k, v, a, b, lwf, lwb, gf, gb, lnw, lnb, T):
    B, S, W = r.shape
    C = RWKV_CHUNK * RWKV_CHUNKS_PER_STEP
    n_ctx = CTX_LEN // C
    n_lat = T // C
    ns = n_ctx + n_lat

    def fchunk(s):
        return s

    def bchunk(s):
        return jnp.where(s < n_ctx, n_ctx - 1 - s, ns + n_ctx - 1 - s)

    def fout(s):
        return jnp.maximum(s - n_ctx, 0)

    def bout(s):
        return jnp.where(s < n_ctx, n_lat - 1, ns - 1 - s)

    def spec(chunk):
        return pl.BlockSpec((1, C, W), lambda bi, s: (bi, chunk(s), 0))

    const = pl.BlockSpec((1, W), lambda bi, s: (0, 0))
    in_specs = [spec(fchunk)] * 6 + [spec(fout)] + [spec(bchunk)] * 6 + [spec(bout)] + [const, const]
    return pl.pallas_call(
        _scan_kernel,
        name="rwkv_scan",
        grid=(B, ns),
        in_specs=in_specs,
        out_specs=[spec(fout), spec(bout)],
        out_shape=[jax.ShapeDtypeStruct((B, T, W), BF16)] * 2,
        scratch_shapes=[pltpu.VMEM((N_GROUPS, GROUP_LANES, GROUP_LANES), F32)] * 2,
        compiler_params=_cparams(("arbitrary", "arbitrary")),
    )(r, k, v, a, b, lwf, gf, r, k, v, a, b, lwb, gb, lnw, lnb)


def _out_kernel(x_ref, rf_ref, rb_ref, wf_ref, wb_ref, bon_ref, wo_ref, mod_ref, gains_ref, wr_ref,
                x1_ref, h2_ref, aff_ref):
    W = GROUP_WIDTH
    ret = rf_ref[0].astype(F32) + rb_ref[0].astype(F32)
    rwk = wf_ref[0].astype(F32) + wb_ref[0].astype(F32) + bon_ref[0].astype(F32)
    mix = _dot(ret, wo_ref[0:W, :]) + _dot(rwk, wo_ref[W:2 * W, :])
    m = mod_ref[0]
    D = D_MODEL
    ms = jnp.mean(mix * mix, axis=-1, keepdims=True)
    x1 = x_ref[0] + m[:, 2 * D:3 * D] * (mix * lax.rsqrt(ms + NORM_EPS) * gains_ref[1:2, :])
    x1_ref[0] = x1
    ms2 = jnp.mean(x1 * x1, axis=-1, keepdims=True)
    h2 = (x1 * lax.rsqrt(ms2 + NORM_EPS) * gains_ref[2:3, :]) * (1.0 + m[:, 4 * D:5 * D]) + m[:, 3 * D:4 * D]
    h2_ref[0] = h2.astype(BF16)
    h_hi, h_lo = _split_bf16(h2, 2)
    w_hi, w_lo = _split_bf16(wr_ref[...], 2)
    logits = lax.dot_general(jnp.concatenate([w_hi, w_lo, w_hi], axis=1), jnp.concatenate([h_hi, h_hi, h_lo], axis=1),
                             (((1,), (1,)), ((), ())), preferred_element_type=F32)
    e = jnp.exp(logits - jnp.max(logits, axis=0, keepdims=True))
    aff_ref[0] = e / jnp.sum(e, axis=0, keepdims=True)


def _out_projection(x, ret_f, ret_b, rw_f, rw_b, bonus, w_out, mods, gains, wr_pad):
    B, T, D = x.shape
    nt = T // OUT_TILE
    tok = lambda w: pl.BlockSpec((1, OUT_TILE, w), lambda b, i: (b, i, 0))
    const = lambda a: pl.BlockSpec(a.shape, lambda b, i: (0,) * a.ndim)
    return pl.pallas_call(
        _out_kernel,
        name="out_projection",
        grid=(B, nt),
        in_specs=[tok(D)] + [tok(GROUP_WIDTH)] * 5 + [const(w_out),
                  pl.BlockSpec((1, 1, mods.shape[2]), lambda b, i: (b, 0, 0)), const(gains), const(wr_pad)],
        out_specs=[tok(D), tok(D), pl.BlockSpec((1, N_EXPERTS, OUT_TILE), lambda b, i: (b, 0, i))],
        out_shape=[jax.ShapeDtypeStruct((B, T, D), F32), jax.ShapeDtypeStruct((B, T, D), BF16),
                   jax.ShapeDtypeStruct((B, N_EXPERTS, T), F32)],
        compiler_params=_cparams(("arbitrary", "arbitrary")),
    )(x, ret_f, ret_b, rw_f, rw_b, bonus, w_out, mods, gains, wr_pad)


def _cumsum_lanes(x):
    n = x.shape[1]
    lane = lax.broadcasted_iota(jnp.int32, (1, n), 1)
    sh = 1
    while sh < n:
        x = x + jnp.where(lane >= sh, pltpu.roll(x, sh, 1), 0.0)
        sh *= 2
    return x


def _select_kernel(cap, aff_ref, slot_ref, slot_t_ref, bounds_ref):
    aff = aff_ref[0]
    capf = jnp.float32(cap)

    def body(i, thr):
        cand = thr | (jnp.int32(1) << (30 - i))
        cnt = jnp.sum(jnp.where(aff >= pltpu.bitcast(cand, F32), 1.0, 0.0), axis=1, keepdims=True)
        return jnp.where(cnt >= capf, cand, thr)

    thr = lax.fori_loop(0, 31, body, jnp.zeros((aff.shape[0], 1), jnp.int32))
    min_normal_bits = jnp.int32(0x00800000)
    above = pltpu.bitcast(jnp.maximum(thr + 1, min_normal_bits), F32)
    gt = aff >= above
    eq = jnp.where(jnp.logical_and(aff >= pltpu.bitcast(thr, F32), jnp.logical_not(gt)), 1.0, 0.0)
    need = capf - jnp.sum(jnp.where(gt, 1.0, 0.0), axis=1, keepdims=True)
    eq_before = _cumsum_lanes(eq) - eq
    sel = jnp.where(jnp.logical_or(gt, jnp.logical_and(eq > 0.0, eq_before < need)), 1.0, 0.0)
    count = _cumsum_lanes(sel)
    slot = jnp.where(sel > 0.0, count - sel, -1.0)
    slot_ref[0] = slot
    pad = jnp.full((LANES - slot.shape[0], slot.shape[1]), -1.0, F32)
    slot_t_ref[0] = jnp.concatenate([slot, pad], axis=0).T

    n_experts, n_tok = slot.shape
    tok = lax.broadcasted_iota(jnp.int32, (1, n_tok), 1)
    lane = lax.broadcasted_iota(jnp.int32, (1, LANES), 1)
    bounds = jnp.zeros((n_experts, LANES), F32)
    for i in range(1, n_tok // BAND_TILE + 1):
        before = jnp.sum(jnp.where(tok == i * BAND_TILE - 1, count, 0.0), axis=1, keepdims=True)
        bounds = jnp.where(lane == i, before, bounds)
    bounds_ref[0] = bounds


def _select(aff, cap):
    B, E, T = aff.shape
    return pl.pallas_call(
        functools.partial(_select_kernel, cap),
        name="expert_select",
        grid=(B,),
        in_specs=[pl.BlockSpec((1, E, T), lambda b: (b, 0, 0))],
        out_specs=[pl.BlockSpec((1, E, T), lambda b: (b, 0, 0)), pl.BlockSpec((1, T, LANES), lambda b: (b, 0, 0)),
                   pl.BlockSpec((1, E, LANES), lambda b: (b, 0, 0))],
        out_shape=[jax.ShapeDtypeStruct((B, E, T), F32), jax.ShapeDtypeStruct((B, T, LANES), F32),
                   jax.ShapeDtypeStruct((B, E, LANES), F32)],
        compiler_params=_cparams(("arbitrary",)),
    )(aff)


def _gather_kernel(cap, band_ref, h_ref, slot_ref, xs_ref):
    b, ti = pl.program_id(0), pl.program_id(1)
    n_experts = slot_ref.shape[2]
    win = GATHER_WINDOW
    log_win = win.bit_length() - 1

    @pl.when(ti == 0)
    def _():
        xs_ref[...] = jnp.zeros_like(xs_ref)

    h = h_ref[0]
    jw = lax.broadcasted_iota(jnp.int32, (win, 1), 0)

    def count_before(e, tile):
        return band_ref[(b * n_experts + e) * BAND_COLS + tile]

    def one_hot(e, first):
        return jnp.where(slot_ref[0, 0, e:e + 1, :] == (first + jw).astype(F32), 1.0, 0.0).astype(BF16)

    def add_rows(e, first, rows):
        xs_ref[0, e, pl.ds(pl.multiple_of(first, ROW_PACK), win), :] += rows.astype(BF16)

    firsts = [(count_before(e, ti) >> 4) << 4 for e in range(n_experts)]
    rows = jnp.dot(jnp.concatenate([one_hot(e, firsts[e]) for e in range(n_experts)], axis=0), h,
                   preferred_element_type=F32)
    for e in range(n_experts):
        add_rows(e, firsts[e], rows[e * win:(e + 1) * win])
    for e in range(n_experts):
        n_win = (count_before(e, ti + 1) - firsts[e] + win - 1) >> log_win

        def more(w, carry, e=e):
            first = firsts[e] + w * win
            add_rows(e, first, jnp.dot(one_hot(e, first), h, preferred_element_type=F32))
            return carry

        lax.fori_loop(1, n_win, more, 0)


def _expert_gather(band, h2, slot, cap):
    B, T, D = h2.shape
    E = slot.shape[1]
    nt = T // BAND_TILE
    slot_tiles = slot.reshape(B, E, nt, BAND_TILE).transpose(0, 2, 1, 3)
    rows = cap + GATHER_WINDOW
    grid_spec = pltpu.PrefetchScalarGridSpec(
        num_scalar_prefetch=1,
        grid=(B, nt),
        in_specs=[pl.BlockSpec((1, BAND_TILE, D), lambda b, t, band: (b, t, 0)),
                  pl.BlockSpec((1, 1, E, BAND_TILE), lambda b, t, band: (b, t, 0, 0))],
        out_specs=pl.BlockSpec((1, E, rows, D), lambda b, t, band: (b, 0, 0, 0)))
    return pl.pallas_call(
        functools.partial(_gather_kernel, cap),
        name="expert_gather",
        grid_spec=grid_spec,
        out_shape=jax.ShapeDtypeStruct((B, E, rows, D), BF16),
        compiler_params=_cparams(("arbitrary", "arbitrary")),
    )(band, h2, slot_tiles)


def _expert_kernel(cap, xs_ref, slot_ref, aff_ref, wg_ref, wu_ref, wd_ref, ye_ref):
    jj = lax.broadcasted_iota(jnp.int32, (cap, 1), 0).astype(F32)
    gate = jnp.zeros((cap, 1), F32)
    for ti in range(slot_ref.shape[2]):
        hit = slot_ref[0, 0, ti:ti + 1, :] == jj
        gate = gate + jnp.sum(jnp.where(hit, aff_ref[0, 0, ti:ti + 1, :], 0.0), axis=1, keepdims=True)
    xs = xs_ref[0, 0]
    hid = _silu(_dot(xs, wg_ref[0])) * _dot(xs, wu_ref[0])
    ye = _dot(hid, wd_ref[0]) * gate
    ye_ref[0, 0] = ye.astype(BF16)


def _expert_ffn(xs, slot, aff, wg, wu, wd, cap):
    B, E, _, D = xs.shape
    T = slot.shape[2]
    nt = T // BAND_TILE
    row = pl.BlockSpec((1, 1, nt, BAND_TILE), lambda e, b: (b, e, 0, 0))
    wspec = pl.BlockSpec((1, D, D), lambda e, b: (e, 0, 0))
    return pl.pallas_call(
        functools.partial(_expert_kernel, cap),
        name="expert_ffn",
        grid=(E, B),
        in_specs=[pl.BlockSpec((1, 1, cap, D), lambda e, b: (b, e, 0, 0)), row, row, wspec, wspec, wspec],
        out_specs=pl.BlockSpec((1, 1, cap, D), lambda e, b: (b, e, 0, 0)),
        out_shape=jax.ShapeDtypeStruct((B, E, cap, D), BF16),
        compiler_params=_cparams(("arbitrary", "arbitrary")),
    )(xs, slot.reshape(B, E, nt, BAND_TILE), aff.reshape(B, E, nt, BAND_TILE), wg, wu, wd)


def _combine_kernel(cap, band_ref, x1_ref, slot_t_ref, ye_ref, mod_ref, gain_ref, o_ref, acc_ref):
    win = min(SLOT_WINDOW, cap)
    log_win = win.bit_length() - 1
    b, i = pl.program_id(0), pl.program_id(1)
    tiles_per = OUT_TILE // BAND_TILE
    st = slot_t_ref[0]
    jw = lax.broadcasted_iota(jnp.int32, (1, win), 1)

    def slots_before(e, tile):
        return band_ref[(b * N_EXPERTS + e) * BAND_COLS + tile * tiles_per]

    def window(e, w):
        first = ((slots_before(e, i) >> 4) << 4) + w * win
        src = pl.multiple_of(jnp.minimum(first, cap - win), ROW_PACK)
        col = st[:, e:e + 1]
        hit = jnp.logical_and(col == (src + jw).astype(F32), col >= first.astype(F32))
        return jnp.where(hit, 1.0, 0.0).astype(BF16), ye_ref[0, e, pl.ds(src, win), :]

    acc = jnp.zeros((OUT_TILE, D_MODEL), F32)
    for e in range(0, N_EXPERTS, 2):
        (h0, r0), (h1, r1) = window(e, 0), window(e + 1, 0)
        acc = acc + jnp.dot(jnp.concatenate([h0, h1], axis=1), jnp.concatenate([r0, r1], axis=0),
                            preferred_element_type=F32)
    acc_ref[...] = acc
    for e in range(N_EXPERTS):
        first = (slots_before(e, i) >> 4) << 4
        n_win = (slots_before(e, i + 1) - first + win - 1) >> log_win

        def more(w, carry, e=e):
            hit, rows = window(e, w)
            acc_ref[...] += jnp.dot(hit, rows, preferred_element_type=F32)
            return carry

        lax.fori_loop(1, n_win, more, 0)
    acc = acc_ref[...]
    ms = jnp.mean(acc * acc, axis=-1, keepdims=True)
    gt2 = mod_ref[0][:, 5 * D_MODEL:6 * D_MODEL]
    o_ref[0] = x1_ref[0] + gt2 * (acc * lax.rsqrt(ms + NORM_EPS) * gain_ref[...])


def _combine(band, x1, slot_t, ye, mods, gain, cap):
    B, T, D = x1.shape
    E = ye.shape[1]
    grid_spec = pltpu.PrefetchScalarGridSpec(
        num_scalar_prefetch=1,
        grid=(B, T // OUT_TILE),
        in_specs=[pl.BlockSpec((1, OUT_TILE, D), lambda b, i, band: (b, i, 0)),
                  pl.BlockSpec((1, OUT_TILE, LANES), lambda b, i, band: (b, i, 0)),
                  pl.BlockSpec((1, E, cap, D), lambda b, i, band: (b, 0, 0, 0)),
                  pl.BlockSpec((1, 1, mods.shape[2]), lambda b, i, band: (b, 0, 0)),
                  pl.BlockSpec((1, D), lambda b, i, band: (0, 0))],
        out_specs=pl.BlockSpec((1, OUT_TILE, D), lambda b, i, band: (b, i, 0)),
        scratch_shapes=[pltpu.VMEM((OUT_TILE, D), F32)])
    return pl.pallas_call(
        functools.partial(_combine_kernel, cap),
        name="combine",
        grid_spec=grid_spec,
        out_shape=jax.ShapeDtypeStruct((B, T, D), F32),
        compiler_params=_cparams(("arbitrary", "arbitrary")),
    )(band, x1, slot_t, ye, mods, gain)


def _rope_tables(T):
    rows = T // GRID_W
    row = np.repeat(np.arange(rows, dtype=np.float32), GRID_W)
    col = np.tile(np.arange(GRID_W, dtype=np.float32), rows)
    n_freq = HEAD_DIM // 4
    freq = jnp.asarray(ROPE_BASE, F32) ** (-jnp.arange(n_freq, dtype=F32) / n_freq)
    ang = jnp.concatenate([row[:, None] * freq, col[:, None] * freq], axis=-1)
    cos, sin = jnp.cos(ang), jnp.sin(ang)
    reps = LANES // (HEAD_DIM // 2)
    cos_t = jnp.tile(cos, (1, reps))
    sin_t = jnp.tile(jnp.concatenate([-sin, sin], axis=-1), (1, reps // 2))
    return cos_t, sin_t


def kernel(x, c, ctx, c_ctx, w_mod, b_mod, norm_gains, w_in, rwkv_conv, rwkv_w0, rwkv_w2, rwkv_a0, rwkv_a2, rwkv_g2, rwkv_k_k, rwkv_k_a, rwkv_r_k, rwkv_lnx_w, rwkv_lnx_b, w_out, w_router, w_gate, w_up, w_down):
    B, T, D = x.shape
    assert D == D_MODEL and ctx.shape == (B, CTX_LEN, D) and T % OUT_TILE == 0 and w_mod.shape[0] == 1
    W = GROUP_WIDTH
    cap = max(1, EC_CAPACITY * T // N_EXPERTS)
    assert T // BAND_TILE < BAND_COLS

    n_rows = -(-(B + 1) // 8) * 8
    cc = jnp.concatenate([c, c_ctx[None, :], jnp.zeros((n_rows - B - 1, D), F32)], axis=0)
    mods = _modulation(cc, w_mod[0], b_mod).reshape(n_rows, 1, 6 * D)

    split = RET_COLS + 3 * W + 2 * DECAY_LORA + ICLR_LORA
    w_pad = jnp.concatenate([w_in[0][:, :split], jnp.zeros((D, LANES - ICLR_LORA), F32), w_in[0][:, split:]],
                            axis=1).astype(BF16)
    cs = split - RET_COLS
    cw = jnp.concatenate([rwkv_conv[0][:, :cs], jnp.zeros((3, LANES - ICLR_LORA), F32), rwkv_conv[0][:, cs:]], axis=1)
    zl = jnp.zeros((DECAY_LORA, W), F32)
    w2bd = jnp.concatenate([jnp.concatenate([rwkv_w2[0, 0], zl], axis=1),
                            jnp.concatenate([zl, rwkv_w2[0, 1]], axis=1)], axis=0).astype(BF16)
    w0cat = jnp.concatenate([rwkv_w0[0, 0], rwkv_w0[0, 1]])[None, :]
    a2pad = jnp.concatenate([rwkv_a2[0], jnp.zeros((LANES - ICLR_LORA, W), F32)], axis=0).astype(BF16)
    g2f, g2b = rwkv_g2[0, 0].astype(BF16), rwkv_g2[0, 1].astype(BF16)
    row = lambda a: a[0][None, :]
    cos_t, sin_t = _rope_tables(T)

    feature_consts = (cw, w0cat, w2bd, row(rwkv_a0), a2pad, g2f, g2b, row(rwkv_k_k), row(rwkv_k_a), row(rwkv_r_k))
    u_ret, r, k, v, a, b, lwf, lwb, gf, gb, bonus = _projection(
        x, ctx, mods, norm_gains[0, 0][None, :], w_pad, cos_t, sin_t, feature_consts)
    ret_f, ret_b = _retention(u_ret, T)
    rw_f, rw_b = _rwkv_scan(r, k, v, a, b, lwf, lwb, gf, gb, row(rwkv_lnx_w), row(rwkv_lnx_b), T)

    x1, h2, aff = _out_projection(x, ret_f, ret_b, rw_f, rw_b, bonus, w_out[0].astype(BF16), mods,
                                  norm_gains[0], w_router[0].T)
    slot, slot_t, bounds = _select(aff, cap)
    band = bounds[:, :, :BAND_COLS].astype(jnp.int32).reshape(-1)
    xs = _expert_gather(band, h2, slot, cap)
    ye = _expert_ffn(xs, slot, aff, w_gate[0], w_up[0], w_down[0], cap)
    return _combine(band, x1, slot_t, ye, mods, norm_gains[0, 3][None, :], cap)
```

```python
import functools

import numpy as np
import jax
import jax.numpy as jnp
from jax import lax
from jax.experimental import pallas as pl
from jax.experimental.pallas import tpu as pltpu

F32 = jnp.float32
BF16 = jnp.bfloat16
HIGHEST = lax.Precision.HIGHEST

D_MODEL = 1024
CTX_LEN = 256
GRID_W = 64
HEAD_DIM = 64
GROUP_WIDTH = 512
LANES = 128
N_PAIRS = GROUP_WIDTH // LANES
HEADS_PER_GROUP = 4
GROUP_LANES = HEADS_PER_GROUP * HEAD_DIM
N_GROUPS = GROUP_WIDTH // GROUP_LANES
RET_COLS = 5 * GROUP_WIDTH
RWKV_COLS_PAD = 4 * GROUP_WIDTH
DECAY_LORA = 64
ICLR_LORA = 64
GATE_LORA = 128
N_EXPERTS = 16
EC_CAPACITY = 2
ROPE_BASE = 10000.0
NORM_EPS = 1e-6
RWKV_GN_EPS = 64e-5

TOK_TILE = 256
RET_CHUNK = 256
RWKV_CHUNK = 64
RWKV_CHUNKS_PER_STEP = 4
OUT_TILE = 512
BAND_TILE = 256
ROW_PACK = 16
GATHER_WINDOW = 64
SLOT_WINDOW = 128
BAND_COLS = 32
VMEM_LIMIT = 56 * 1024 * 1024


def _cparams(sem):
    return pltpu.CompilerParams(dimension_semantics=sem, vmem_limit_bytes=VMEM_LIMIT)


def _dot(a, b):
    return jnp.dot(a.astype(BF16), b.astype(BF16), preferred_element_type=F32)


def _dot_nt(a, b):
    return lax.dot_general(a.astype(BF16), b.astype(BF16), (((1,), (1,)), ((), ())),
                           preferred_element_type=F32)


def _dot_exact(a, b):
    return jnp.dot(a, b, precision=HIGHEST, preferred_element_type=F32)


def _head_block_mask(n):
    r = lax.broadcasted_iota(jnp.int32, (n, n), 0)
    c = lax.broadcasted_iota(jnp.int32, (n, n), 1)
    return (r // HEAD_DIM) == (c // HEAD_DIM)


def _split_bf16(x, terms):
    out = []
    for _ in range(terms - 1):
        hi = x.astype(BF16)
        out.append(hi)
        x = x - hi.astype(F32)
    out.append(x.astype(BF16))
    return out


def _head_sum(x, scale=1.0):
    cols = []
    for g in range(x.shape[1] // GROUP_LANES):
        m = jnp.where(_head_block_mask(GROUP_LANES), scale, 0.0).astype(BF16)
        parts = _split_bf16(x[:, g * GROUP_LANES:(g + 1) * GROUP_LANES], 2)
        cols.append(jnp.dot(jnp.concatenate(parts, axis=1), jnp.concatenate([m, m], axis=0),
                            preferred_element_type=F32))
    return cols[0] if len(cols) == 1 else jnp.concatenate(cols, axis=1)


def _silu(x):
    return x * jax.nn.sigmoid(x)


def _round_robin(gens):
    results = [None] * len(gens)
    active = list(range(len(gens)))
    while active:
        for i in list(active):
            try:
                next(gens[i])
            except StopIteration as done:
                results[i] = done.value
                active.remove(i)
    return results


def _mod_kernel(c_ref, w_ref, b_ref, o_ref):
    o_ref[...] = _dot_exact(_silu(c_ref[...]), w_ref[...]) + b_ref[...]


def _modulation(cc, w_mod, b_mod):
    rows, d = cc.shape
    n = w_mod.shape[1]
    tn = 1536
    return pl.pallas_call(
        _mod_kernel,
        name="modulation",
        grid=(n // tn,),
        in_specs=[pl.BlockSpec((rows, d), lambda i: (0, 0)),
                  pl.BlockSpec((d, tn), lambda i: (0, i)),
                  pl.BlockSpec((1, tn), lambda i: (0, i))],
        out_specs=pl.BlockSpec((rows, tn), lambda i: (0, i)),
        out_shape=jax.ShapeDtypeStruct((rows, n), F32),
        compiler_params=_cparams(("arbitrary",)),
    )(cc, w_mod, b_mod)


def _proj_kernel(n_tiles, x_ref, ctx_ref, xp_ref, xn_ref, mod_ref, gain_ref, w_ref, cos_ref, sin_ref, *rest):
    feature_consts, ret_ref, feature_outs = rest[:10], rest[10], rest[11:]
    j = pl.program_id(1)
    is_lat = j > 0
    xin = jnp.concatenate([jnp.where(is_lat, x_ref[0], ctx_ref[0]), xp_ref[0], xn_ref[0]], axis=0)
    ms = jnp.mean(xin * xin, axis=-1, keepdims=True)
    y = xin * lax.rsqrt(ms + NORM_EPS) * gain_ref[...]
    m = mod_ref[0]
    h = (y * (1.0 + m[:, D_MODEL:2 * D_MODEL]) + m[:, 0:D_MODEL]).astype(BF16)
    u_rw = jnp.dot(h, w_ref[:, RET_COLS:], preferred_element_type=F32)
    has_prev = (j >= 2).astype(F32)
    has_next = jnp.logical_and(j >= 1, j <= n_tiles - 2).astype(F32)
    features = _rwkv_features(u_rw[0:TOK_TILE], u_rw[TOK_TILE + 7:TOK_TILE + 8] * has_prev,
                              u_rw[TOK_TILE + 8:TOK_TILE + 9] * has_next, *feature_consts, *feature_outs)

    def retention_columns():
        lat = is_lat.astype(F32)
        cos = cos_ref[...] * lat + (1.0 - lat)
        sin = sin_ref[...] * lat
        lane = lax.broadcasted_iota(jnp.int32, (1, LANES), 1)
        first_half = (lane % HEAD_DIM) < (HEAD_DIM // 2)
        W = GROUP_WIDTH
        for c, scale in enumerate((1.0, HEAD_DIM ** -0.5, None, None, None)):
            u = jnp.dot(h[0:TOK_TILE], w_ref[:, c * W:(c + 1) * W], preferred_element_type=F32)
            yield
            if scale is None:
                ret_ref[0, :, c * W:(c + 1) * W] = u.astype(ret_ref.dtype)
                continue
            for g in range(N_PAIRS):
                t = u[:, g * LANES:(g + 1) * LANES] * scale
                sw = jnp.where(first_half, pltpu.roll(t, LANES - HEAD_DIM // 2, 1), pltpu.roll(t, HEAD_DIM // 2, 1))
                ret_ref[0, :, c * W + g * LANES:c * W + (g + 1) * LANES] = (t * cos + sw * sin).astype(ret_ref.dtype)

    _round_robin([features, retention_columns()])


def _projection(x, ctx, mods, gain, w_pad, cos_t, sin_t, feature_consts):
    B, T, D = x.shape
    nt = (T + CTX_LEN) // TOK_TILE
    S = T + CTX_LEN
    rb = TOK_TILE // 8
    const = lambda a: pl.BlockSpec(a.shape, lambda b, j: (0,) * a.ndim)
    seq_spec = pl.BlockSpec((1, TOK_TILE, GROUP_WIDTH), lambda b, j: (b, j, 0))
    lat_spec = pl.BlockSpec((1, TOK_TILE, GROUP_WIDTH), lambda b, j: (b, jnp.maximum(j - 1, 0), 0))
    seq_shape = jax.ShapeDtypeStruct((B, S, GROUP_WIDTH), F32)
    lat_shape = jax.ShapeDtypeStruct((B, T, GROUP_WIDTH), BF16)
    return pl.pallas_call(
        functools.partial(_proj_kernel, nt),
        name="projection",
        grid=(B, nt),
        in_specs=[
            pl.BlockSpec((1, TOK_TILE, D), lambda b, j: (b, jnp.maximum(j - 1, 0), 0)),
            pl.BlockSpec((1, TOK_TILE, D), lambda b, j: (b, 0, 0)),
            pl.BlockSpec((1, 8, D), lambda b, j: (b, jnp.maximum((j - 1) * rb - 1, 0), 0)),
            pl.BlockSpec((1, 8, D), lambda b, j: (b, jnp.clip(j * rb, 0, T // 8 - 1), 0)),
            pl.BlockSpec((1, 1, mods.shape[2]), lambda b, j: (jnp.where(j == 0, B, b), 0, 0)),
            pl.BlockSpec((1, D), lambda b, j: (0, 0)),
            const(w_pad),
            pl.BlockSpec((TOK_TILE, LANES), lambda b, j: (jnp.maximum(j - 1, 0), 0)),
            pl.BlockSpec((TOK_TILE, LANES), lambda b, j: (jnp.maximum(j - 1, 0), 0)),
        ] + [const(a) for a in feature_consts],
        out_specs=[pl.BlockSpec((1, TOK_TILE, RET_COLS), lambda b, j: (b, j, 0))] + [seq_spec] * 7 + [lat_spec] * 3,
        out_shape=([jax.ShapeDtypeStruct((B, S, RET_COLS), BF16)]
                   + [jax.ShapeDtypeStruct((B, S, GROUP_WIDTH), BF16)] * 5 + [seq_shape] * 2 + [lat_shape] * 3),
        compiler_params=_cparams(("arbitrary", "arbitrary")),
    )(x, ctx, x, x, mods, gain, w_pad, cos_t, sin_t, *feature_consts)


def _ret_group(d, gi, q_ref, k_ref, v_ref, g_ref, o_ref, s_ref, dmask_ref, qdec_ref, kdec_ref, cdec_ref):
    C = RET_CHUNK
    lane_head = lax.broadcasted_iota(jnp.int32, (1, GROUP_LANES), 1) // HEAD_DIM
    sl = slice(gi * GROUP_LANES, (gi + 1) * GROUP_LANES)
    q4, k4, v4 = q_ref[0, :, sl], k_ref[0, :, sl], v_ref[0, :, sl]
    state = s_ref[gi]
    q_s = jnp.concatenate([jnp.where(lane_head == hh, q4, 0.0) for hh in range(HEADS_PER_GROUP)], axis=0)
    scores = _dot_nt(q_s, k4) * dmask_ref[d, gi]
    cross = _dot(q4 * qdec_ref[d, :, sl], state)
    yield
    o_s = _dot(scores, v4)
    kd = (k4 * kdec_ref[d, :, sl]).T
    s_ref[gi] = state * cdec_ref[:, sl] + jnp.where(_head_block_mask(GROUP_LANES), _dot(kd, v4), 0.0)
    yield
    y = cross + jnp.where(lane_head == 0, o_s[0:C], 0.0)
    for hh in range(1, HEADS_PER_GROUP):
        y = y + jnp.where(lane_head == hh, o_s[hh * C:(hh + 1) * C], 0.0)
    ms = _head_sum(y * y, 1.0 / HEAD_DIM)
    o_ref[0, :, sl] = (_silu(g_ref[0, :, sl].astype(F32)) * (y * lax.rsqrt(ms + NORM_EPS))).astype(o_ref.dtype)


def _ret_kernel(qf, kf, vf, gf, qb, kb, vb, gb, dmask, qdec, kdec, cdec, of, ob, sf, sb):
    @pl.when(pl.program_id(1) == 0)
    def _():
        sf[...] = jnp.zeros_like(sf)
        sb[...] = jnp.zeros_like(sb)

    tables = (dmask, qdec, kdec, cdec)
    _round_robin([_ret_group(d, gi, *refs, *tables)
                  for d, refs in ((0, (qf, kf, vf, gf, of, sf)), (1, (qb, kb, vb, gb, ob, sb)))
                  for gi in range(N_GROUPS)])


def _retention_tables():
    n_heads = GROUP_WIDTH // HEAD_DIM
    log_gamma = np.log1p(-np.exp2(-5.0 - np.arange(n_heads, dtype=np.float64)))
    idx = np.arange(RET_CHUNK, dtype=np.float64)
    rel = idx[:, None] - idx[None, :]
    dm = np.where(rel >= 0, np.exp(np.maximum(rel, 0.0) * log_gamma[:, None, None]), 0.0)
    dmask = np.stack([dm, dm.transpose(0, 2, 1)])
    dmask = dmask.reshape(2, N_GROUPS, HEADS_PER_GROUP * RET_CHUNK, RET_CHUNK)
    pos = np.stack([idx, RET_CHUNK - 1.0 - idx])
    lg = np.repeat(log_gamma, HEAD_DIM)
    qdec = np.exp((pos[:, :, None] + 1.0) * lg)
    kdec = np.exp((RET_CHUNK - 1.0 - pos[:, :, None]) * lg)
    cdec = np.exp(RET_CHUNK * lg)[None, :]
    return [jnp.asarray(a, F32) for a in (dmask, qdec, kdec, cdec)]


def _retention(u_ret, T):
    B = u_ret.shape[0]
    n_lat = T // RET_CHUNK
    ns = n_lat + 1
    dmask, qdec, kdec, cdec = _retention_tables()

    def fchunk(s):
        return s

    def bchunk(s):
        return jnp.where(s == 0, 0, ns - s)

    def col(c, chunk):
        return pl.BlockSpec((1, RET_CHUNK, GROUP_WIDTH), lambda b, s: (b, chunk(s), c))

    const = lambda shape: pl.BlockSpec(shape, lambda b, s: (0,) * len(shape))
    in_specs = ([col(c, fchunk) for c in (0, 1, 2, 3)] + [col(c, bchunk) for c in (0, 1, 2, 4)]
                + [const(dmask.shape), const(qdec.shape), const(kdec.shape), const(cdec.shape)])
    out_specs = [pl.BlockSpec((1, RET_CHUNK, GROUP_WIDTH), lambda b, s: (b, jnp.maximum(s - 1, 0), 0)),
                 pl.BlockSpec((1, RET_CHUNK, GROUP_WIDTH), lambda b, s: (b, jnp.where(s == 0, n_lat - 1, n_lat - s), 0))]
    out_shape = [jax.ShapeDtypeStruct((B, T, GROUP_WIDTH), BF16)] * 2
    return pl.pallas_call(
        _ret_kernel,
        name="retention",
        grid=(B, ns),
        in_specs=in_specs,
        out_specs=out_specs,
        out_shape=out_shape,
        scratch_shapes=[pltpu.VMEM((N_GROUPS, GROUP_LANES, GROUP_LANES), F32)] * 2,
        compiler_params=_cparams(("arbitrary", "arbitrary")),
    )(*([u_ret] * 8), dmask, qdec, kdec, cdec)


def _rwkv_features(x, prev_row, next_row, cw_ref, w0_ref, w2_ref, a0_ref, a2_ref, g2f_ref, g2b_ref,
                   kk_ref, ka_ref, rk_ref,
                   r_o, k_o, v_o, a_o, b_o, lwf_o, lwb_o, gf_o, gb_o, bon_o):
    rows = lax.broadcasted_iota(jnp.int32, (TOK_TILE, 1), 0)
    xm = jnp.where(rows == 0, prev_row, pltpu.roll(x, 1, 0))
    xp = jnp.where(rows == TOK_TILE - 1, next_row, pltpu.roll(x, TOK_TILE - 1, 0))
    rw = cw_ref[0:1, :] * xm + cw_ref[1:2, :] * x + cw_ref[2:3, :] * xp
    yield

    W = GROUP_WIDTH
    r, k, v, lo = rw[:, 0:W], rw[:, W:2 * W], rw[:, 2 * W:3 * W], rw[:, 3 * W:4 * W]
    z = w0_ref[...] + _dot(jnp.tanh(lo[:, 0:LANES]), w2_ref[...])
    log_decay = -np.exp(-0.5).astype(np.float32) * jax.nn.sigmoid(z)
    iclr = jax.nn.sigmoid(a0_ref[...] + _dot(lo[:, LANES:2 * LANES], a2_ref[...]))
    yield
    g_f = _dot(jax.nn.sigmoid(lo[:, 2 * LANES:3 * LANES]), g2f_ref[...])
    g_b = _dot(jax.nn.sigmoid(lo[:, 3 * LANES:4 * LANES]), g2b_ref[...])
    kk = k * kk_ref[...]
    yield
    kk = kk / jnp.maximum(jnp.sqrt(_head_sum(kk * kk)), 1e-12)
    k_mod = k * (1.0 + (iclr - 1.0) * ka_ref[...])
    yield
    bonus = _head_sum(r * k_mod * rk_ref[...]) * v
    r_o[0] = r.astype(r_o.dtype)
    k_o[0] = k_mod.astype(k_o.dtype)
    v_o[0] = v.astype(v_o.dtype)
    a_o[0] = (-kk).astype(a_o.dtype)
    b_o[0] = (kk * iclr).astype(b_o.dtype)
    lwf_o[0] = log_decay[:, 0:W]
    lwb_o[0] = log_decay[:, W:2 * W]
    gf_o[0] = g_f.astype(gf_o.dtype)
    gb_o[0] = g_b.astype(gb_o.dtype)
    bon_o[0] = ((g_f + g_b) * bonus).astype(bon_o.dtype)


def _dplr_local(r, k, v, a, b, lw, reverse):
    C, H, GL = RWKV_CHUNK, HEADS_PER_GROUP, GROUP_LANES
    CS = H * C
    ri = lax.broadcasted_iota(jnp.int32, (C, C), 0)
    ci = lax.broadcasted_iota(jnp.int32, (C, C), 1)
    tri = jnp.where((ci >= ri) if reverse else (ci <= ri), 1.0, 0.0).astype(BF16)
    lc = jnp.dot(jnp.concatenate([tri, tri, tri], axis=1), jnp.concatenate(_split_bf16(lw, 3), axis=0),
                 preferred_element_type=F32)
    yield
    ltot = jnp.sum(lw, axis=0, keepdims=True)
    a_t = a * jnp.exp(lc - lw)
    r_t = r * jnp.exp(lc)
    inv = jnp.exp(-lc)
    b_t, k_t = b * inv, k * inv
    to_end = jnp.exp(ltot - lc)
    b_e, k_e = b * to_end, k * to_end

    rs = lax.broadcasted_iota(jnp.int32, (CS, CS), 0)
    cs = lax.broadcasted_iota(jnp.int32, (CS, CS), 1)
    bd = (rs // C) == (cs // C)
    stack = lambda x: jnp.where(bd, jnp.concatenate([x] * H, axis=0), 0.0)
    unstack = lambda x: x[0:C] + x[C:2 * C] + x[2 * C:3 * C] + x[3 * C:4 * C]
    a_s, r_s, v_s = stack(a_t), stack(r_t), stack(v)
    g = _dot_nt(jnp.concatenate([a_s, r_s], axis=0), jnp.concatenate([b_t, k_t], axis=0))
    g_swapped = pltpu.roll(g, C, 1)
    t128 = lax.broadcasted_iota(jnp.int32, (C, 2 * C), 0)
    l128 = lax.broadcasted_iota(jnp.int32, (C, 2 * C), 1)
    s128, half128 = l128 % C, l128 // C
    incl128 = (s128 >= t128) if reverse else (s128 <= t128)
    strict128 = (s128 > t128) if reverse else (s128 < t128)
    zeros128 = jnp.zeros((C, 2 * C), F32)

    def block_diagonal(row0, from_k, tri):
        blocks = []
        for hh in range(H):
            half = hh % 2
            src = g if half == int(from_k) else g_swapped
            piece = jnp.where(jnp.logical_and(half128 == half, tri), src[row0 + hh * C:row0 + (hh + 1) * C], 0.0)
            blocks.append(jnp.concatenate([piece if c == hh // 2 else zeros128 for c in range(H // 2)], axis=1))
        return jnp.concatenate(blocks, axis=0)

    low = block_diagonal(0, False, strict128)
    ak = block_diagonal(0, True, strict128)
    rb = block_diagonal(CS, False, incl128)
    rk = block_diagonal(CS, True, incl128)
    yield
    akv = _dot(ak, v_s)
    tm = jnp.where(rs == cs, 1.0, 0.0) + low
    pw = _dot(low, low)
    yield
    n = 2
    while n < C // 2:
        both = _dot(jnp.concatenate([pw, tm], axis=0), pw)
        pw, tm = both[0:CS], tm + both[CS:]
        n *= 2
        yield
    tm = tm + _dot(tm, pw)
    yield
    uw = _dot(tm, jnp.concatenate([akv, a_s], axis=1))
    yield
    y0_s = _dot(jnp.concatenate([rb, rk], axis=1), jnp.concatenate([uw[:, 0:GL], v_s], axis=0))
    qa_s = _dot(rb, uw[:, GL:])
    yield
    u0, w = unstack(uw[:, 0:GL]), unstack(uw[:, GL:])
    y0, q_add = unstack(y0_s), unstack(qa_s)
    zeros = jnp.zeros((C, GL), F32)
    stack_t = jnp.concatenate([w, u0, v, zeros], axis=0).T
    rm = jnp.concatenate([jnp.concatenate([b_e, zeros], axis=1),
                          jnp.concatenate([zeros, b_e], axis=1),
                          jnp.concatenate([zeros, k_e], axis=1),
                          jnp.concatenate([zeros, zeros], axis=1)], axis=0)
    mn = _dot(stack_t, rm)
    return (r_t + q_add, y0, jnp.exp(ltot), jnp.where(bd, mn[:, 0:GL], 0.0), jnp.where(bd, mn[:, GL:], 0.0))


def _scan_kernel(rf, kf, vf, af, bf, lwf, gf, rb, kb, vb, ab, bb, lwb, gb, lnw, lnb, of, ob, sf, sb):
    @pl.when(pl.program_id(1) == 0)
    def _():
        sf[...] = jnp.zeros_like(sf)
        sb[...] = jnp.zeros_like(sb)

    C = RWKV_CHUNK
    fwd = (False, (rf, kf, vf, af, bf, lwf), gf, of, sf)
    bwd = (True, (rb, kb, vb, ab, bb, lwb), gb, ob, sb)
    chains = []
    for reverse, refs, g_ref, o_ref, s_ref in (fwd, bwd):
        subs = range(RWKV_CHUNKS_PER_STEP)
        for gi in range(N_GROUPS):
            sl = slice(gi * GROUP_LANES, (gi + 1) * GROUP_LANES)
            for sub in (reversed(subs) if reverse else subs):
                rows = slice(sub * C, (sub + 1) * C)
                chains.append((gi, sl, rows, g_ref, o_ref, s_ref,
                               _dplr_local(*[ref[0, rows, sl] for ref in refs], reverse)))
    local = _round_robin([c[-1] for c in chains])

    def carry_state(s_ref, gi, parts):
        st = s_ref[gi]
        ys = []
        for q, y0, decay, m_t, n_t in parts:
            ys.append(_dot_nt(q, st) + y0)
            st = st * decay + _dot(st, m_t) + n_t
            yield
        s_ref[gi] = st
        return ys

    per = RWKV_CHUNKS_PER_STEP
    ys = _round_robin([carry_state(chains[i][5], chains[i][0], local[i:i + per]) for i in range(0, len(chains), per)])
    y_all = jnp.concatenate([y for group in ys for y in group], axis=0)
    dlt = y_all - _head_sum(y_all, 1.0 / HEAD_DIM)
    yn = dlt * lax.rsqrt(_head_sum(dlt * dlt, 1.0 / HEAD_DIM) + RWKV_GN_EPS)
    for i, (gi, sl, rows, g_ref, o_ref, s_ref, _) in enumerate(chains):
        out = g_ref[0, rows, sl].astype(F32) * (yn[i * C:(i + 1) * C] * lnw[:, sl] + lnb[:, sl])
        o_ref[0, rows, sl] = out.astype(o_ref.dtype)


def _rwkv_scan(r, k, v, a, b, lwf, lwb, gf, gb, lnw, lnb, T):
    B, S, W = r.shape
    C = RWKV_CHUNK * RWKV_CHUNKS_PER_STEP
    n_ctx = CTX_LEN // C
    n_lat = T // C
    ns = n_ctx + n_lat

    def fchunk(s):
        return s

    def bchunk(s):
        return jnp.where(s < n_ctx, n_ctx - 1 - s, ns + n_ctx - 1 - s)

    def fout(s):
        return jnp.maximum(s - n_ctx, 0)

    def bout(s):
        return jnp.where(s < n_ctx, n_lat - 1, ns - 1 - s)

    def spec(chunk):
        return pl.BlockSpec((1, C, W), lambda bi, s: (bi, chunk(s), 0))

    const = pl.BlockSpec((1, W), lambda bi, s: (0, 0))
    in_specs = [spec(fchunk)] * 6 + [spec(fout)] + [spec(bchunk)] * 6 + [spec(bout)] + [const, const]
    return pl.pallas_call(
        _scan_kernel,
        name="rwkv_scan",
        grid=(B, ns),
        in_specs=in_specs,
        out_specs=[spec(fout), spec(bout)],
        out_shape=[jax.ShapeDtypeStruct((B, T, W), BF16)] * 2,
        scratch_shapes=[pltpu.VMEM((N_GROUPS, GROUP_LANES, GROUP_LANES), F32)] * 2,
        compiler_params=_cparams(("arbitrary", "arbitrary")),
    )(r, k, v, a, b, lwf, gf, r, k, v, a, b, lwb, gb, lnw, lnb)


def _out_kernel(x_ref, rf_ref, rb_ref, wf_ref, wb_ref, bon_ref, wo_ref, mod_ref, gains_ref, wr_ref,
                x1_ref, h2_ref, aff_ref):
    W = GROUP_WIDTH
    ret = rf_ref[0].astype(F32) + rb_ref[0].astype(F32)
    rwk = wf_ref[0].astype(F32) + wb_ref[0].astype(F32) + bon_ref[0].astype(F32)
    mix = _dot(ret, wo_ref[0:W, :]) + _dot(rwk, wo_ref[W:2 * W, :])
    m = mod_ref[0]
    D = D_MODEL
    ms = jnp.mean(mix * mix, axis=-1, keepdims=True)
    x1 = x_ref[0] + m[:, 2 * D:3 * D] * (mix * lax.rsqrt(ms + NORM_EPS) * gains_ref[1:2, :])
    x1_ref[0] = x1
    ms2 = jnp.mean(x1 * x1, axis=-1, keepdims=True)
    h2 = (x1 * lax.rsqrt(ms2 + NORM_EPS) * gains_ref[2:3, :]) * (1.0 + m[:, 4 * D:5 * D]) + m[:, 3 * D:4 * D]
    h2_ref[0] = h2.astype(BF16)
    h_hi, h_lo = _split_bf16(h2, 2)
    w_hi, w_lo = _split_bf16(wr_ref[...], 2)
    logits = lax.dot_general(jnp.concatenate([w_hi, w_lo, w_hi], axis=1), jnp.concatenate([h_hi, h_hi, h_lo], axis=1),
                             (((1,), (1,)), ((), ())), preferred_element_type=F32)
    e = jnp.exp(logits - jnp.max(logits, axis=0, keepdims=True))
    aff_ref[0] = e / jnp.sum(e, axis=0, keepdims=True)


def _out_projection(x, ret_f, ret_b, rw_f, rw_b, bonus, w_out, mods, gains, wr_pad):
    B, T, D = x.shape
    nt = T // OUT_TILE
    tok = lambda w: pl.BlockSpec((1, OUT_TILE, w), lambda b, i: (b, i, 0))
    const = lambda a: pl.BlockSpec(a.shape, lambda b, i: (0,) * a.ndim)
    return pl.pallas_call(
        _out_kernel,
        name="out_projection",
        grid=(B, nt),
        in_specs=[tok(D)] + [tok(GROUP_WIDTH)] * 5 + [const(w_out),
                  pl.BlockSpec((1, 1, mods.shape[2]), lambda b, i: (b, 0, 0)), const(gains), const(wr_pad)],
        out_specs=[tok(D), tok(D), pl.BlockSpec((1, N_EXPERTS, OUT_TILE), lambda b, i: (b, 0, i))],
        out_shape=[jax.ShapeDtypeStruct((B, T, D), F32), jax.ShapeDtypeStruct((B, T, D), BF16),
                   jax.ShapeDtypeStruct((B, N_EXPERTS, T), F32)],
        compiler_params=_cparams(("arbitrary", "arbitrary")),
    )(x, ret_f, ret_b, rw_f, rw_b, bonus, w_out, mods, gains, wr_pad)


def _cumsum_lanes(x):
    n = x.shape[1]
    lane = lax.broadcasted_iota(jnp.int32, (1, n), 1)
    sh = 1
    while sh < n:
        x = x + jnp.where(lane >= sh, pltpu.roll(x, sh, 1), 0.0)
        sh *= 2
    return x


def _select_kernel(cap, aff_ref, slot_ref, slot_t_ref, bounds_ref):
    aff = aff_ref[0]
    capf = jnp.float32(cap)

    def body(i, thr):
        cand = thr | (jnp.int32(1) << (30 - i))
        cnt = jnp.sum(jnp.where(aff >= pltpu.bitcast(cand, F32), 1.0, 0.0), axis=1, keepdims=True)
        return jnp.where(cnt >= capf, cand, thr)

    thr = lax.fori_loop(0, 31, body, jnp.zeros((aff.shape[0], 1), jnp.int32))
    min_normal_bits = jnp.int32(0x00800000)
    above = pltpu.bitcast(jnp.maximum(thr + 1, min_normal_bits), F32)
    gt = aff >= above
    eq = jnp.where(jnp.logical_and(aff >= pltpu.bitcast(thr, F32), jnp.logical_not(gt)), 1.0, 0.0)
    need = capf - jnp.sum(jnp.where(gt, 1.0, 0.0), axis=1, keepdims=True)
    eq_before = _cumsum_lanes(eq) - eq
    sel = jnp.where(jnp.logical_or(gt, jnp.logical_and(eq > 0.0, eq_before < need)), 1.0, 0.0)
    count = _cumsum_lanes(sel)
    slot = jnp.where(sel > 0.0, count - sel, -1.0)
    slot_ref[0] = slot
    pad = jnp.full((LANES - slot.shape[0], slot.shape[1]), -1.0, F32)
    slot_t_ref[0] = jnp.concatenate([slot, pad], axis=0).T

    n_experts, n_tok = slot.shape
    tok = lax.broadcasted_iota(jnp.int32, (1, n_tok), 1)
    lane = lax.broadcasted_iota(jnp.int32, (1, LANES), 1)
    bounds = jnp.zeros((n_experts, LANES), F32)
    for i in range(1, n_tok // BAND_TILE + 1):
        before = jnp.sum(jnp.where(tok == i * BAND_TILE - 1, count, 0.0), axis=1, keepdims=True)
        bounds = jnp.where(lane == i, before, bounds)
    bounds_ref[0] = bounds


def _select(aff, cap):
    B, E, T = aff.shape
    return pl.pallas_call(
        functools.partial(_select_kernel, cap),
        name="expert_select",
        grid=(B,),
        in_specs=[pl.BlockSpec((1, E, T), lambda b: (b, 0, 0))],
        out_specs=[pl.BlockSpec((1, E, T), lambda b: (b, 0, 0)), pl.BlockSpec((1, T, LANES), lambda b: (b, 0, 0)),
                   pl.BlockSpec((1, E, LANES), lambda b: (b, 0, 0))],
        out_shape=[jax.ShapeDtypeStruct((B, E, T), F32), jax.ShapeDtypeStruct((B, T, LANES), F32),
                   jax.ShapeDtypeStruct((B, E, LANES), F32)],
        compiler_params=_cparams(("arbitrary",)),
    )(aff)


def _gather_kernel(cap, band_ref, h_ref, slot_ref, xs_ref):
    b, ti = pl.program_id(0), pl.program_id(1)
    n_experts = slot_ref.shape[2]
    win = GATHER_WINDOW
    log_win = win.bit_length() - 1

    @pl.when(ti == 0)
    def _():
        xs_ref[...] = jnp.zeros_like(xs_ref)

    h = h_ref[0]
    jw = lax.broadcasted_iota(jnp.int32, (win, 1), 0)

    def count_before(e, tile):
        return band_ref[(b * n_experts + e) * BAND_COLS + tile]

    def one_hot(e, first):
        return jnp.where(slot_ref[0, 0, e:e + 1, :] == (first + jw).astype(F32), 1.0, 0.0).astype(BF16)

    def add_rows(e, first, rows):
        xs_ref[0, e, pl.ds(pl.multiple_of(first, ROW_PACK), win), :] += rows.astype(BF16)

    firsts = [(count_before(e, ti) >> 4) << 4 for e in range(n_experts)]
    rows = jnp.dot(jnp.concatenate([one_hot(e, firsts[e]) for e in range(n_experts)], axis=0), h,
                   preferred_element_type=F32)
    for e in range(n_experts):
        add_rows(e, firsts[e], rows[e * win:(e + 1) * win])
    for e in range(n_experts):
        n_win = (count_before(e, ti + 1) - firsts[e] + win - 1) >> log_win

        def more(w, carry, e=e):
            first = firsts[e] + w * win
            add_rows(e, first, jnp.dot(one_hot(e, first), h, preferred_element_type=F32))
            return carry

        lax.fori_loop(1, n_win, more, 0)


def _expert_gather(band, h2, slot, cap):
    B, T, D = h2.shape
    E = slot.shape[1]
    nt = T // BAND_TILE
    slot_tiles = slot.reshape(B, E, nt, BAND_TILE).transpose(0, 2, 1, 3)
    rows = cap + GATHER_WINDOW
    grid_spec = pltpu.PrefetchScalarGridSpec(
        num_scalar_prefetch=1,
        grid=(B, nt),
        in_specs=[pl.BlockSpec((1, BAND_TILE, D), lambda b, t, band: (b, t, 0)),
                  pl.BlockSpec((1, 1, E, BAND_TILE), lambda b, t, band: (b, t, 0, 0))],
        out_specs=pl.BlockSpec((1, E, rows, D), lambda b, t, band: (b, 0, 0, 0)))
    return pl.pallas_call(
        functools.partial(_gather_kernel, cap),
        name="expert_gather",
        grid_spec=grid_spec,
        out_shape=jax.ShapeDtypeStruct((B, E, rows, D), BF16),
        compiler_params=_cparams(("arbitrary", "arbitrary")),
    )(band, h2, slot_tiles)


def _expert_kernel(cap, xs_ref, slot_ref, aff_ref, wg_ref, wu_ref, wd_ref, ye_ref):
    jj = lax.broadcasted_iota(jnp.int32, (cap, 1), 0).astype(F32)
    gate = jnp.zeros((cap, 1), F32)
    for ti in range(slot_ref.shape[2]):
        hit = slot_ref[0, 0, ti:ti + 1, :] == jj
        gate = gate + jnp.sum(jnp.where(hit, aff_ref[0, 0, ti:ti + 1, :], 0.0), axis=1, keepdims=True)
    xs = xs_ref[0, 0]
    hid = _silu(_dot(xs, wg_ref[0])) * _dot(xs, wu_ref[0])
    ye = _dot(hid, wd_ref[0]) * gate
    ye_ref[0, 0] = ye.astype(BF16)


def _expert_ffn(xs, slot, aff, wg, wu, wd, cap):
    B, E, _, D = xs.shape
    T = slot.shape[2]
    nt = T // BAND_TILE
    row = pl.BlockSpec((1, 1, nt, BAND_TILE), lambda e, b: (b, e, 0, 0))
    wspec = pl.BlockSpec((1, D, D), lambda e, b: (e, 0, 0))
    return pl.pallas_call(
        functools.partial(_expert_kernel, cap),
        name="expert_ffn",
        grid=(E, B),
        in_specs=[pl.BlockSpec((1, 1, cap, D), lambda e, b: (b, e, 0, 0)), row, row, wspec, wspec, wspec],
        out_specs=pl.BlockSpec((1, 1, cap, D), lambda e, b: (b, e, 0, 0)),
        out_shape=jax.ShapeDtypeStruct((B, E, cap, D), BF16),
        compiler_params=_cparams(("arbitrary", "arbitrary")),
    )(xs, slot.reshape(B, E, nt, BAND_TILE), aff.reshape(B, E, nt, BAND_TILE), wg, wu, wd)


def _combine_kernel(cap, band_ref, x1_ref, slot_t_ref, ye_ref, mod_ref, gain_ref, o_ref, acc_ref):
    win = min(SLOT_WINDOW, cap)
    log_win = win.bit_length() - 1
    b, i = pl.program_id(0), pl.program_id(1)
    tiles_per = OUT_TILE // BAND_TILE
    st = slot_t_ref[0]
    jw = lax.broadcasted_iota(jnp.int32, (1, win), 1)

    def slots_before(e, tile):
        return band_ref[(b * N_EXPERTS + e) * BAND_COLS + tile * tiles_per]

    def window(e, w):
        first = ((slots_before(e, i) >> 4) << 4) + w * win
        src = pl.multiple_of(jnp.minimum(first, cap - win), ROW_PACK)
        col = st[:, e:e + 1]
        hit = jnp.logical_and(col == (src + jw).astype(F32), col >= first.astype(F32))
        return jnp.where(hit, 1.0, 0.0).astype(BF16), ye_ref[0, e, pl.ds(src, win), :]

    acc = jnp.zeros((OUT_TILE, D_MODEL), F32)
    for e in range(0, N_EXPERTS, 2):
        (h0, r0), (h1, r1) = window(e, 0), window(e + 1, 0)
        acc = acc + jnp.dot(jnp.concatenate([h0, h1], axis=1), jnp.concatenate([r0, r1], axis=0),
                            preferred_element_type=F32)
    acc_ref[...] = acc
    for e in range(N_EXPERTS):
        first = (slots_before(e, i) >> 4) << 4
        n_win = (slots_before(e, i + 1) - first + win - 1) >> log_win

        def more(w, carry, e=e):
            hit, rows = window(e, w)
            acc_ref[...] += jnp.dot(hit, rows, preferred_element_type=F32)
            return carry

        lax.fori_loop(1, n_win, more, 0)
    acc = acc_ref[...]
    ms = jnp.mean(acc * acc, axis=-1, keepdims=True)
    gt2 = mod_ref[0][:, 5 * D_MODEL:6 * D_MODEL]
    o_ref[0] = x1_ref[0] + gt2 * (acc * lax.rsqrt(ms + NORM_EPS) * gain_ref[...])


def _combine(band, x1, slot_t, ye, mods, gain, cap):
    B, T, D = x1.shape
    E = ye.shape[1]
    grid_spec = pltpu.PrefetchScalarGridSpec(
        num_scalar_prefetch=1,
        grid=(B, T // OUT_TILE),
        in_specs=[pl.BlockSpec((1, OUT_TILE, D), lambda b, i, band: (b, i, 0)),
                  pl.BlockSpec((1, OUT_TILE, LANES), lambda b, i, band: (b, i, 0)),
                  pl.BlockSpec((1, E, cap, D), lambda b, i, band: (b, 0, 0, 0)),
                  pl.BlockSpec((1, 1, mods.shape[2]), lambda b, i, band: (b, 0, 0)),
                  pl.BlockSpec((1, D), lambda b, i, band: (0, 0))],
        out_specs=pl.BlockSpec((1, OUT_TILE, D), lambda b, i, band: (b, i, 0)),
        scratch_shapes=[pltpu.VMEM((OUT_TILE, D), F32)])
    return pl.pallas_call(
        functools.partial(_combine_kernel, cap),
        name="combine",
        grid_spec=grid_spec,
        out_shape=jax.ShapeDtypeStruct((B, T, D), F32),
        compiler_params=_cparams(("arbitrary", "arbitrary")),
    )(band, x1, slot_t, ye, mods, gain)


def _rope_tables(T):
    rows = T // GRID_W
    row = np.repeat(np.arange(rows, dtype=np.float32), GRID_W)
    col = np.tile(np.arange(GRID_W, dtype=np.float32), rows)
    n_freq = HEAD_DIM // 4
    freq = jnp.asarray(ROPE_BASE, F32) ** (-jnp.arange(n_freq, dtype=F32) / n_freq)
    ang = jnp.concatenate([row[:, None] * freq, col[:, None] * freq], axis=-1)
    cos, sin = jnp.cos(ang), jnp.sin(ang)
    reps = LANES // (HEAD_DIM // 2)
    cos_t = jnp.tile(cos, (1, reps))
    sin_t = jnp.tile(jnp.concatenate([-sin, sin], axis=-1), (1, reps // 2))
    return cos_t, sin_t


def kernel(x, c, ctx, c_ctx, w_mod, b_mod, norm_gains, w_in, rwkv_conv, rwkv_w0, rwkv_w2, rwkv_a0, rwkv_a2, rwkv_g2, rwkv_k_k, rwkv_k_a, rwkv_r_k, rwkv_lnx_w, rwkv_lnx_b, w_out, w_router, w_gate, w_up, w_down):
    B, T, D = x.shape
    assert D == D_MODEL and ctx.shape == (B, CTX_LEN, D) and T % OUT_TILE == 0 and w_mod.shape[0] == 1
    W = GROUP_WIDTH
    cap = max(1, EC_CAPACITY * T // N_EXPERTS)
    assert T // BAND_TILE < BAND_COLS

    n_rows = -(-(B + 1) // 8) * 8
    cc = jnp.concatenate([c, c_ctx[None, :], jnp.zeros((n_rows - B - 1, D), F32)], axis=0)
    mods = _modulation(cc, w_mod[0], b_mod).reshape(n_rows, 1, 6 * D)

    split = RET_COLS + 3 * W + 2 * DECAY_LORA + ICLR_LORA
    w_pad = jnp.concatenate([w_in[0][:, :split], jnp.zeros((D, LANES - ICLR_LORA), F32), w_in[0][:, split:]],
                            axis=1).astype(BF16)
    cs = split - RET_COLS
    cw = jnp.concatenate([rwkv_conv[0][:, :cs], jnp.zeros((3, LANES - ICLR_LORA), F32), rwkv_conv[0][:, cs:]], axis=1)
    zl = jnp.zeros((DECAY_LORA, W), F32)
    w2bd = jnp.concatenate([jnp.concatenate([rwkv_w2[0, 0], zl], axis=1),
                            jnp.concatenate([zl, rwkv_w2[0, 1]], axis=1)], axis=0).astype(BF16)
    w0cat = jnp.concatenate([rwkv_w0[0, 0], rwkv_w0[0, 1]])[None, :]
    a2pad = jnp.concatenate([rwkv_a2[0], jnp.zeros((LANES - ICLR_LORA, W), F32)], axis=0).astype(BF16)
    g2f, g2b = rwkv_g2[0, 0].astype(BF16), rwkv_g2[0, 1].astype(BF16)
    row = lambda a: a[0][None, :]
    cos_t, sin_t = _rope_tables(T)

    feature_consts = (cw, w0cat, w2bd, row(rwkv_a0), a2pad, g2f, g2b, row(rwkv_k_k), row(rwkv_k_a), row(rwkv_r_k))
    u_ret, r, k, v, a, b, lwf, lwb, gf, gb, bonus = _projection(
        x, ctx, mods, norm_gains[0, 0][None, :], w_pad, cos_t, sin_t, feature_consts)
    ret_f, ret_b = _retention(u_ret, T)
    rw_f, rw_b = _rwkv_scan(r, k, v, a, b, lwf, lwb, gf, gb, row(rwkv_lnx_w), row(rwkv_lnx_b), T)

    x1, h2, aff = _out_projection(x, ret_f, ret_b, rw_f, rw_b, bonus, w_out[0].astype(BF16), mods,
                                  norm_gains[0], w_router[0].T)
    slot, slot_t, bounds = _select(aff, cap)
    band = bounds[:, :, :BAND_COLS].astype(jnp.int32).reshape(-1)
    xs = _expert_gather(band, h2, slot, cap)
    ye = _expert_ffn(xs, slot, aff, w_gate[0], w_up[0], w_down[0], cap)
    return _combine(band, x1, slot_t, ye, mods, norm_gains[0, 3][None, :], cap)
```

```python
import functools

import numpy as np
import jax
import jax.numpy as jnp
from jax import lax
from jax.experimental import pallas as pl
from jax.experimental.pallas import tpu as pltpu

F32 = jnp.float32
BF16 = jnp.bfloat16
HIGHEST = lax.Precision.HIGHEST

D_MODEL = 1024
CTX_LEN = 256
GRID_W = 64
HEAD_DIM = 64
GROUP_WIDTH = 512
LANES = 128
N_PAIRS = GROUP_WIDTH // LANES
HEADS_PER_GROUP = 4
GROUP_LANES = HEADS_PER_GROUP * HEAD_DIM
N_GROUPS = GROUP_WIDTH // GROUP_LANES
RET_COLS = 5 * GROUP_WIDTH
RWKV_COLS_PAD = 4 * GROUP_WIDTH
DECAY_LORA = 64
ICLR_LORA = 64
GATE_LORA = 128
N_EXPERTS = 16
EC_CAPACITY = 2
ROPE_BASE = 10000.0
NORM_EPS = 1e-6
RWKV_GN_EPS = 64e-5

TOK_TILE = 256
RET_CHUNK = 256
RWKV_CHUNK = 64
RWKV_CHUNKS_PER_STEP = 4
OUT_TILE = 512
BAND_TILE = 256
ROW_PACK = 16
GATHER_WINDOW = 64
SLOT_WINDOW = 128
BAND_COLS = 32
VMEM_LIMIT = 56 * 1024 * 1024


def _cparams(sem):
    return pltpu.CompilerParams(dimension_semantics=sem, vmem_limit_bytes=VMEM_LIMIT)


def _dot(a, b):
    return jnp.dot(a.astype(BF16), b.astype(BF16), preferred_element_type=F32)


def _dot_nt(a, b):
    return lax.dot_general(a.astype(BF16), b.astype(BF16), (((1,), (1,)), ((), ())),
                           preferred_element_type=F32)


def _dot_exact(a, b):
    return jnp.dot(a, b, precision=HIGHEST, preferred_element_type=F32)


def _head_block_mask(n):
    r = lax.broadcasted_iota(jnp.int32, (n, n), 0)
    c = lax.broadcasted_iota(jnp.int32, (n, n), 1)
    return (r // HEAD_DIM) == (c // HEAD_DIM)


def _split_bf16(x, terms):
    out = []
    for _ in range(terms - 1):
        hi = x.astype(BF16)
        out.append(hi)
        x = x - hi.astype(F32)
    out.append(x.astype(BF16))
    return out


def _head_sum(x, scale=1.0):
    cols = []
    for g in range(x.shape[1] // GROUP_LANES):
        m = jnp.where(_head_block_mask(GROUP_LANES), scale, 0.0).astype(BF16)
        parts = _split_bf16(x[:, g * GROUP_LANES:(g + 1) * GROUP_LANES], 2)
        cols.append(jnp.dot(jnp.concatenate(parts, axis=1), jnp.concatenate([m, m], axis=0),
                            preferred_element_type=F32))
    return cols[0] if len(cols) == 1 else jnp.concatenate(cols, axis=1)


def _silu(x):
    return x * jax.nn.sigmoid(x)


def _round_robin(gens):
    results = [None] * len(gens)
    active = list(range(len(gens)))
    while active:
        for i in list(active):
            try:
                next(gens[i])
            except StopIteration as done:
                results[i] = done.value
                active.remove(i)
    return results


def _mod_kernel(c_ref, w_ref, b_ref, o_ref):
    o_ref[...] = _dot_exact(_silu(c_ref[...]), w_ref[...]) + b_ref[...]


def _modulation(cc, w_mod, b_mod):
    rows, d = cc.shape
    n = w_mod.shape[1]
    tn = 1536
    return pl.pallas_call(
        _mod_kernel,
        name="modulation",
        grid=(n // tn,),
        in_specs=[pl.BlockSpec((rows, d), lambda i: (0, 0)),
                  pl.BlockSpec((d, tn), lambda i: (0, i)),
                  pl.BlockSpec((1, tn), lambda i: (0, i))],
        out_specs=pl.BlockSpec((rows, tn), lambda i: (0, i)),
        out_shape=jax.ShapeDtypeStruct((rows, n), F32),
        compiler_params=_cparams(("arbitrary",)),
    )(cc, w_mod, b_mod)


def _proj_kernel(n_tiles, x_ref, ctx_ref, xp_ref, xn_ref, mod_ref, gain_ref, w_ref, cos_ref, sin_ref, *rest):
    feature_consts, ret_ref, feature_outs = rest[:10], rest[10], rest[11:]
    j = pl.program_id(1)
    is_lat = j > 0
    xin = jnp.concatenate([jnp.where(is_lat, x_ref[0], ctx_ref[0]), xp_ref[0], xn_ref[0]], axis=0)
    ms = jnp.mean(xin * xin, axis=-1, keepdims=True)
    y = xin * lax.rsqrt(ms + NORM_EPS) * gain_ref[...]
    m = mod_ref[0]
    h = (y * (1.0 + m[:, D_MODEL:2 * D_MODEL]) + m[:, 0:D_MODEL]).astype(BF16)
    u_rw = jnp.dot(h, w_ref[:, RET_COLS:], preferred_element_type=F32)
    has_prev = (j >= 2).astype(F32)
    has_next = jnp.logical_and(j >= 1, j <= n_tiles - 2).astype(F32)
    features = _rwkv_features(u_rw[0:TOK_TILE], u_rw[TOK_TILE + 7:TOK_TILE + 8] * has_prev,
                              u_rw[TOK_TILE + 8:TOK_TILE + 9] * has_next, *feature_consts, *feature_outs)

    def retention_columns():
        lat = is_lat.astype(F32)
        cos = cos_ref[...] * lat + (1.0 - lat)
        sin = sin_ref[...] * lat
        lane = lax.broadcasted_iota(jnp.int32, (1, LANES), 1)
        first_half = (lane % HEAD_DIM) < (HEAD_DIM // 2)
        W = GROUP_WIDTH
        for c, scale in enumerate((1.0, HEAD_DIM ** -0.5, None, None, None)):
            u = jnp.dot(h[0:TOK_TILE], w_ref[:, c * W:(c + 1) * W], preferred_element_type=F32)
            yield
            if scale is None:
                ret_ref[0, :, c * W:(c + 1) * W] = u
                continue
            for g in range(N_PAIRS):
                t = u[:, g * LANES:(g + 1) * LANES] * scale
                sw = jnp.where(first_half, pltpu.roll(t, LANES - HEAD_DIM // 2, 1), pltpu.roll(t, HEAD_DIM // 2, 1))
                ret_ref[0, :, c * W + g * LANES:c * W + (g + 1) * LANES] = t * cos + sw * sin

    _round_robin([features, retention_columns()])


def _projection(x, ctx, mods, gain, w_pad, cos_t, sin_t, feature_consts):
    B, T, D = x.shape
    nt = (T + CTX_LEN) // TOK_TILE
    S = T + CTX_LEN
    rb = TOK_TILE // 8
    const = lambda a: pl.BlockSpec(a.shape, lambda b, j: (0,) * a.ndim)
    seq_spec = pl.BlockSpec((1, TOK_TILE, GROUP_WIDTH), lambda b, j: (b, j, 0))
    lat_spec = pl.BlockSpec((1, TOK_TILE, GROUP_WIDTH), lambda b, j: (b, jnp.maximum(j - 1, 0), 0))
    seq_shape = jax.ShapeDtypeStruct((B, S, GROUP_WIDTH), F32)
    lat_shape = jax.ShapeDtypeStruct((B, T, GROUP_WIDTH), BF16)
    return pl.pallas_call(
        functools.partial(_proj_kernel, nt),
        name="projection",
        grid=(B, nt),
        in_specs=[
            pl.BlockSpec((1, TOK_TILE, D), lambda b, j: (b, jnp.maximum(j - 1, 0), 0)),
            pl.BlockSpec((1, TOK_TILE, D), lambda b, j: (b, 0, 0)),
            pl.BlockSpec((1, 8, D), lambda b, j: (b, jnp.maximum((j - 1) * rb - 1, 0), 0)),
            pl.BlockSpec((1, 8, D), lambda b, j: (b, jnp.clip(j * rb, 0, T // 8 - 1), 0)),
            pl.BlockSpec((1, 1, mods.shape[2]), lambda b, j: (jnp.where(j == 0, B, b), 0, 0)),
            pl.BlockSpec((1, D), lambda b, j: (0, 0)),
            const(w_pad),
            pl.BlockSpec((TOK_TILE, LANES), lambda b, j: (jnp.maximum(j - 1, 0), 0)),
            pl.BlockSpec((TOK_TILE, LANES), lambda b, j: (jnp.maximum(j - 1, 0), 0)),
        ] + [const(a) for a in feature_consts],
        out_specs=[pl.BlockSpec((1, TOK_TILE, RET_COLS), lambda b, j: (b, j, 0))] + [seq_spec] * 7 + [lat_spec] * 3,
        out_shape=[jax.ShapeDtypeStruct((B, S, RET_COLS), F32)] + [seq_shape] * 7 + [lat_shape] * 3,
        compiler_params=_cparams(("arbitrary", "arbitrary")),
    )(x, ctx, x, x, mods, gain, w_pad, cos_t, sin_t, *feature_consts)


def _ret_group(d, gi, q_ref, k_ref, v_ref, g_ref, o_ref, s_ref, dmask_ref, qdec_ref, kdec_ref, cdec_ref):
    C = RET_CHUNK
    lane_head = lax.broadcasted_iota(jnp.int32, (1, GROUP_LANES), 1) // HEAD_DIM
    sl = slice(gi * GROUP_LANES, (gi + 1) * GROUP_LANES)
    q4, k4, v4 = q_ref[0, :, sl], k_ref[0, :, sl], v_ref[0, :, sl]
    state = s_ref[gi]
    q_s = jnp.concatenate([jnp.where(lane_head == hh, q4, 0.0) for hh in range(HEADS_PER_GROUP)], axis=0)
    scores = _dot_nt(q_s, k4) * dmask_ref[d, gi]
    cross = _dot(q4 * qdec_ref[d, :, sl], state)
    yield
    o_s = _dot(scores, v4)
    kd = (k4 * kdec_ref[d, :, sl]).T
    s_ref[gi] = state * cdec_ref[:, sl] + jnp.where(_head_block_mask(GROUP_LANES), _dot(kd, v4), 0.0)
    yield
    y = cross + jnp.where(lane_head == 0, o_s[0:C], 0.0)
    for hh in range(1, HEADS_PER_GROUP):
        y = y + jnp.where(lane_head == hh, o_s[hh * C:(hh + 1) * C], 0.0)
    ms = _head_sum(y * y, 1.0 / HEAD_DIM)
    o_ref[0, :, sl] = (_silu(g_ref[0, :, sl]) * (y * lax.rsqrt(ms + NORM_EPS))).astype(o_ref.dtype)


def _ret_kernel(qf, kf, vf, gf, qb, kb, vb, gb, dmask, qdec, kdec, cdec, of, ob, sf, sb):
    @pl.when(pl.program_id(1) == 0)
    def _():
        sf[...] = jnp.zeros_like(sf)
        sb[...] = jnp.zeros_like(sb)

    tables = (dmask, qdec, kdec, cdec)
    _round_robin([_ret_group(d, gi, *refs, *tables)
                  for d, refs in ((0, (qf, kf, vf, gf, of, sf)), (1, (qb, kb, vb, gb, ob, sb)))
                  for gi in range(N_GROUPS)])


def _retention_tables():
    n_heads = GROUP_WIDTH // HEAD_DIM
    log_gamma = np.log1p(-np.exp2(-5.0 - np.arange(n_heads, dtype=np.float64)))
    idx = np.arange(RET_CHUNK, dtype=np.float64)
    rel = idx[:, None] - idx[None, :]
    dm = np.where(rel >= 0, np.exp(np.maximum(rel, 0.0) * log_gamma[:, None, None]), 0.0)
    dmask = np.stack([dm, dm.transpose(0, 2, 1)])
    dmask = dmask.reshape(2, N_GROUPS, HEADS_PER_GROUP * RET_CHUNK, RET_CHUNK)
    pos = np.stack([idx, RET_CHUNK - 1.0 - idx])
    lg = np.repeat(log_gamma, HEAD_DIM)
    qdec = np.exp((pos[:, :, None] + 1.0) * lg)
    kdec = np.exp((RET_CHUNK - 1.0 - pos[:, :, None]) * lg)
    cdec = np.exp(RET_CHUNK * lg)[None, :]
    return [jnp.asarray(a, F32) for a in (dmask, qdec, kdec, cdec)]


def _retention(u_ret, T):
    B = u_ret.shape[0]
    n_lat = T // RET_CHUNK
    ns = n_lat + 1
    dmask, qdec, kdec, cdec = _retention_tables()

    def fchunk(s):
        return s

    def bchunk(s):
        return jnp.where(s == 0, 0, ns - s)

    def col(c, chunk):
        return pl.BlockSpec((1, RET_CHUNK, GROUP_WIDTH), lambda b, s: (b, chunk(s), c))

    const = lambda shape: pl.BlockSpec(shape, lambda b, s: (0,) * len(shape))
    in_specs = ([col(c, fchunk) for c in (0, 1, 2, 3)] + [col(c, bchunk) for c in (0, 1, 2, 4)]
                + [const(dmask.shape), const(qdec.shape), const(kdec.shape), const(cdec.shape)])
    out_specs = [pl.BlockSpec((1, RET_CHUNK, GROUP_WIDTH), lambda b, s: (b, jnp.maximum(s - 1, 0), 0)),
                 pl.BlockSpec((1, RET_CHUNK, GROUP_WIDTH), lambda b, s: (b, jnp.where(s == 0, n_lat - 1, n_lat - s), 0))]
    out_shape = [jax.ShapeDtypeStruct((B, T, GROUP_WIDTH), BF16)] * 2
    return pl.pallas_call(
        _ret_kernel,
        name="retention",
        grid=(B, ns),
        in_specs=in_specs,
        out_specs=out_specs,
        out_shape=out_shape,
        scratch_shapes=[pltpu.VMEM((N_GROUPS, GROUP_LANES, GROUP_LANES), F32)] * 2,
        compiler_params=_cparams(("arbitrary", "arbitrary")),
    )(*([u_ret] * 8), dmask, qdec, kdec, cdec)


def _rwkv_features(x, prev_row, next_row, cw_ref, w0_ref, w2_ref, a0_ref, a2_ref, g2f_ref, g2b_ref,
                   kk_ref, ka_ref, rk_ref,
                   r_o, k_o, v_o, a_o, b_o, lwf_o, lwb_o, gf_o, gb_o, bon_o):
    rows = lax.broadcasted_iota(jnp.int32, (TOK_TILE, 1), 0)
    xm = jnp.where(rows == 0, prev_row, pltpu.roll(x, 1, 0))
    xp = jnp.where(rows == TOK_TILE - 1, next_row, pltpu.roll(x, TOK_TILE - 1, 0))
    rw = cw_ref[0:1, :] * xm + cw_ref[1:2, :] * x + cw_ref[2:3, :] * xp
    yield

    W = GROUP_WIDTH
    r, k, v, lo = rw[:, 0:W], rw[:, W:2 * W], rw[:, 2 * W:3 * W], rw[:, 3 * W:4 * W]
    z = w0_ref[...] + _dot(jnp.tanh(lo[:, 0:LANES]), w2_ref[...])
    log_decay = -np.exp(-0.5).astype(np.float32) * jax.nn.sigmoid(z)
    iclr = jax.nn.sigmoid(a0_ref[...] + _dot(lo[:, LANES:2 * LANES], a2_ref[...]))
    yield
    g_f = _dot(jax.nn.sigmoid(lo[:, 2 * LANES:3 * LANES]), g2f_ref[...])
    g_b = _dot(jax.nn.sigmoid(lo[:, 3 * LANES:4 * LANES]), g2b_ref[...])
    kk = k * kk_ref[...]
    yield
    kk = kk / jnp.maximum(jnp.sqrt(_head_sum(kk * kk)), 1e-12)
    k_mod = k * (1.0 + (iclr - 1.0) * ka_ref[...])
    yield
    bonus = _head_sum(r * k_mod * rk_ref[...]) * v
    r_o[0] = r
    k_o[0] = k_mod
    v_o[0] = v
    a_o[0] = -kk
    b_o[0] = kk * iclr
    lwf_o[0] = log_decay[:, 0:W]
    lwb_o[0] = log_decay[:, W:2 * W]
    gf_o[0] = g_f.astype(gf_o.dtype)
    gb_o[0] = g_b.astype(gb_o.dtype)
    bon_o[0] = ((g_f + g_b) * bonus).astype(bon_o.dtype)


def _dplr_local(r, k, v, a, b, lw, reverse):
    C, H, GL = RWKV_CHUNK, HEADS_PER_GROUP, GROUP_LANES
    CS = H * C
    ri = lax.broadcasted_iota(jnp.int32, (C, C), 0)
    ci = lax.broadcasted_iota(jnp.int32, (C, C), 1)
    tri = jnp.where((ci >= ri) if reverse else (ci <= ri), 1.0, 0.0).astype(BF16)
    lc = jnp.dot(jnp.concatenate([tri, tri, tri], axis=1), jnp.concatenate(_split_bf16(lw, 3), axis=0),
                 preferred_element_type=F32)
    yield
    ltot = jnp.sum(lw, axis=0, keepdims=True)
    a_t = a * jnp.exp(lc - lw)
    r_t = r * jnp.exp(lc)
    inv = jnp.exp(-lc)
    b_t, k_t = b * inv, k * inv
    to_end = jnp.exp(ltot - lc)
    b_e, k_e = b * to_end, k * to_end

    rs = lax.broadcasted_iota(jnp.int32, (CS, CS), 0)
    cs = lax.broadcasted_iota(jnp.int32, (CS, CS), 1)
    bd = (rs // C) == (cs // C)
    stack = lambda x: jnp.where(bd, jnp.concatenate([x] * H, axis=0), 0.0)
    unstack = lambda x: x[0:C] + x[C:2 * C] + x[2 * C:3 * C] + x[3 * C:4 * C]
    a_s, r_s, v_s = stack(a_t), stack(r_t), stack(v)
    g = _dot_nt(jnp.concatenate([a_s, r_s], axis=0), jnp.concatenate([b_t, k_t], axis=0))
    g_swapped = pltpu.roll(g, C, 1)
    t128 = lax.broadcasted_iota(jnp.int32, (C, 2 * C), 0)
    l128 = lax.broadcasted_iota(jnp.int32, (C, 2 * C), 1)
    s128, half128 = l128 % C, l128 // C
    incl128 = (s128 >= t128) if reverse else (s128 <= t128)
    strict128 = (s128 > t128) if reverse else (s128 < t128)
    zeros128 = jnp.zeros((C, 2 * C), F32)

    def block_diagonal(row0, from_k, tri):
        blocks = []
        for hh in range(H):
            half = hh % 2
            src = g if half == int(from_k) else g_swapped
            piece = jnp.where(jnp.logical_and(half128 == half, tri), src[row0 + hh * C:row0 + (hh + 1) * C], 0.0)
            blocks.append(jnp.concatenate([piece if c == hh // 2 else zeros128 for c in range(H // 2)], axis=1))
        return jnp.concatenate(blocks, axis=0)

    low = block_diagonal(0, False, strict128)
    ak = block_diagonal(0, True, strict128)
    rb = block_diagonal(CS, False, incl128)
    rk = block_diagonal(CS, True, incl128)
    yield
    akv = _dot(ak, v_s)
    tm = jnp.where(rs == cs, 1.0, 0.0) + low
    pw = _dot(low, low)
    yield
    n = 2
    while n < C // 2:
        both = _dot(jnp.concatenate([pw, tm], axis=0), pw)
        pw, tm = both[0:CS], tm + both[CS:]
        n *= 2
        yield
    tm = tm + _dot(tm, pw)
    yield
    uw = _dot(tm, jnp.concatenate([akv, a_s], axis=1))
    yield
    y0_s = _dot(jnp.concatenate([rb, rk], axis=1), jnp.concatenate([uw[:, 0:GL], v_s], axis=0))
    qa_s = _dot(rb, uw[:, GL:])
    yield
    u0, w = unstack(uw[:, 0:GL]), unstack(uw[:, GL:])
    y0, q_add = unstack(y0_s), unstack(qa_s)
    zeros = jnp.zeros((C, GL), F32)
    stack_t = jnp.concatenate([w, u0, v, zeros], axis=0).T
    rm = jnp.concatenate([jnp.concatenate([b_e, zeros], axis=1),
                          jnp.concatenate([zeros, b_e], axis=1),
                          jnp.concatenate([zeros, k_e], axis=1),
                          jnp.concatenate([zeros, zeros], axis=1)], axis=0)
    mn = _dot(stack_t, rm)
    return (r_t + q_add, y0, jnp.exp(ltot), jnp.where(bd, mn[:, 0:GL], 0.0), jnp.where(bd, mn[:, GL:], 0.0))


def _scan_kernel(rf, kf, vf, af, bf, lwf, gf, rb, kb, vb, ab, bb, lwb, gb, lnw, lnb, of, ob, sf, sb):
    @pl.when(pl.program_id(1) == 0)
    def _():
        sf[...] = jnp.zeros_like(sf)
        sb[...] = jnp.zeros_like(sb)

    C = RWKV_CHUNK
    fwd = (False, (rf, kf, vf, af, bf, lwf), gf, of, sf)
    bwd = (True, (rb, kb, vb, ab, bb, lwb), gb, ob, sb)
    chains = []
    for reverse, refs, g_ref, o_ref, s_ref in (fwd, bwd):
        subs = range(RWKV_CHUNKS_PER_STEP)
        for gi in range(N_GROUPS):
            sl = slice(gi * GROUP_LANES, (gi + 1) * GROUP_LANES)
            for sub in (reversed(subs) if reverse else subs):
                rows = slice(sub * C, (sub + 1) * C)
                chains.append((gi, sl, rows, g_ref, o_ref, s_ref,
                               _dplr_local(*[ref[0, rows, sl] for ref in refs], reverse)))
    local = _round_robin([c[-1] for c in chains])

    def carry_state(s_ref, gi, parts):
        st = s_ref[gi]
        ys = []
        for q, y0, decay, m_t, n_t in parts:
            ys.append(_dot_nt(q, st) + y0)
            st = st * decay + _dot(st, m_t) + n_t
            yield
        s_ref[gi] = st
        return ys

    per = RWKV_CHUNKS_PER_STEP
    ys = _round_robin([carry_state(chains[i][5], chains[i][0], local[i:i + per]) for i in range(0, len(chains), per)])
    y_all = jnp.concatenate([y for group in ys for y in group], axis=0)
    dlt = y_all - _head_sum(y_all, 1.0 / HEAD_DIM)
    yn = dlt * lax.rsqrt(_head_sum(dlt * dlt, 1.0 / HEAD_DIM) + RWKV_GN_EPS)
    for i, (gi, sl, rows, g_ref, o_ref, s_ref, _) in enumerate(chains):
        out = g_ref[0, rows, sl].astype(F32) * (yn[i * C:(i + 1) * C] * lnw[:, sl] + lnb[:, sl])
        o_ref[0, rows, sl] = out.astype(o_ref.dtype)


def _rwkv_scan(r, k, v, a, b, lwf, lwb, gf, gb, lnw, lnb, T):
    B, S, W = r.shape
    C = RWKV_CHUNK * RWKV_CHUNKS_PER_STEP
    n_ctx = CTX_LEN // C
    n_lat = T // C
    ns = n_ctx + n_lat

    def fchunk(s):
        return s

    def bchunk(s):
        return jnp.where(s < n_ctx, n_ctx - 1 - s, ns + n_ctx - 1 - s)

    def fout(s):
        return jnp.maximum(s - n_ctx, 0)

    def bout(s):
        return jnp.where(s < n_ctx, n_lat - 1, ns - 1 - s)

    def spec(chunk):
        return pl.BlockSpec((1, C, W), lambda bi, s: (bi, chunk(s), 0))

    const = pl.BlockSpec((1, W), lambda bi, s: (0, 0))
    in_specs = [spec(fchunk)] * 6 + [spec(fout)] + [spec(bchunk)] * 6 + [spec(bout)] + [const, const]
    return pl.pallas_call(
        _scan_kernel,
        name="rwkv_scan",
        grid=(B, ns),
        in_specs=in_specs,
        out_specs=[spec(fout), spec(bout)],
        out_shape=[jax.ShapeDtypeStruct((B, T, W), BF16)] * 2,
        scratch_shapes=[pltpu.VMEM((N_GROUPS, GROUP_LANES, GROUP_LANES), F32)] * 2,
        compiler_params=_cparams(("arbitrary", "arbitrary")),
    )(r, k, v, a, b, lwf, gf, r, k, v, a, b, lwb, gb, lnw, lnb)


def _out_kernel(x_ref, rf_ref, rb_ref, wf_ref, wb_ref, bon_ref, wo_ref, mod_ref, gains_ref, wr_ref,
                x1_ref, h2_ref, aff_ref):
    W = GROUP_WIDTH
    ret = rf_ref[0].astype(F32) + rb_ref[0].astype(F32)
    rwk = wf_ref[0].astype(F32) + wb_ref[0].astype(F32) + bon_ref[0].astype(F32)
    mix = _dot(ret, wo_ref[0:W, :]) + _dot(rwk, wo_ref[W:2 * W, :])
    m = mod_ref[0]
    D = D_MODEL
    ms = jnp.mean(mix * mix, axis=-1, keepdims=True)
    x1 = x_ref[0] + m[:, 2 * D:3 * D] * (mix * lax.rsqrt(ms + NORM_EPS) * gains_ref[1:2, :])
    x1_ref[0] = x1
    ms2 = jnp.mean(x1 * x1, axis=-1, keepdims=True)
    h2 = (x1 * lax.rsqrt(ms2 + NORM_EPS) * gains_ref[2:3, :]) * (1.0 + m[:, 4 * D:5 * D]) + m[:, 3 * D:4 * D]
    h2_ref[0] = h2.astype(BF16)
    h_hi, h_lo = _split_bf16(h2, 2)
    w_hi, w_lo = _split_bf16(wr_ref[...], 2)
    logits = lax.dot_general(jnp.concatenate([w_hi, w_lo, w_hi], axis=1), jnp.concatenate([h_hi, h_hi, h_lo], axis=1),
                             (((1,), (1,)), ((), ())), preferred_element_type=F32)
    e = jnp.exp(logits - jnp.max(logits, axis=0, keepdims=True))
    aff_ref[0] = e / jnp.sum(e, axis=0, keepdims=True)


def _out_projection(x, ret_f, ret_b, rw_f, rw_b, bonus, w_out, mods, gains, wr_pad):
    B, T, D = x.shape
    nt = T // OUT_TILE
    tok = lambda w: pl.BlockSpec((1, OUT_TILE, w), lambda b, i: (b, i, 0))
    const = lambda a: pl.BlockSpec(a.shape, lambda b, i: (0,) * a.ndim)
    return pl.pallas_call(
        _out_kernel,
        name="out_projection",
        grid=(B, nt),
        in_specs=[tok(D)] + [tok(GROUP_WIDTH)] * 5 + [const(w_out),
                  pl.BlockSpec((1, 1, mods.shape[2]), lambda b, i: (b, 0, 0)), const(gains), const(wr_pad)],
        out_specs=[tok(D), tok(D), pl.BlockSpec((1, N_EXPERTS, OUT_TILE), lambda b, i: (b, 0, i))],
        out_shape=[jax.ShapeDtypeStruct((B, T, D), F32), jax.ShapeDtypeStruct((B, T, D), BF16),
                   jax.ShapeDtypeStruct((B, N_EXPERTS, T), F32)],
        compiler_params=_cparams(("arbitrary", "arbitrary")),
    )(x, ret_f, ret_b, rw_f, rw_b, bonus, w_out, mods, gains, wr_pad)


def _cumsum_lanes(x):
    n = x.shape[1]
    lane = lax.broadcasted_iota(jnp.int32, (1, n), 1)
    sh = 1
    while sh < n:
        x = x + jnp.where(lane >= sh, pltpu.roll(x, sh, 1), 0.0)
        sh *= 2
    return x


def _select_kernel(cap, aff_ref, slot_ref, bounds_ref):
    aff = aff_ref[0]
    capf = jnp.float32(cap)

    def body(i, thr):
        cand = thr | (jnp.int32(1) << (30 - i))
        cnt = jnp.sum(jnp.where(aff >= pltpu.bitcast(cand, F32), 1.0, 0.0), axis=1, keepdims=True)
        return jnp.where(cnt >= capf, cand, thr)

    thr = lax.fori_loop(0, 31, body, jnp.zeros((aff.shape[0], 1), jnp.int32))
    min_normal_bits = jnp.int32(0x00800000)
    above = pltpu.bitcast(jnp.maximum(thr + 1, min_normal_bits), F32)
    gt = aff >= above
    eq = jnp.where(jnp.logical_and(aff >= pltpu.bitcast(thr, F32), jnp.logical_not(gt)), 1.0, 0.0)
    need = capf - jnp.sum(jnp.where(gt, 1.0, 0.0), axis=1, keepdims=True)
    eq_before = _cumsum_lanes(eq) - eq
    sel = jnp.where(jnp.logical_or(gt, jnp.logical_and(eq > 0.0, eq_before < need)), 1.0, 0.0)
    count = _cumsum_lanes(sel)
    slot = jnp.where(sel > 0.0, count - sel, -1.0)
    slot_ref[0] = slot

    n_experts, n_tok = slot.shape
    tok = lax.broadcasted_iota(jnp.int32, (1, n_tok), 1)
    lane = lax.broadcasted_iota(jnp.int32, (1, LANES), 1)
    bounds = jnp.zeros((n_experts, LANES), F32)
    for i in range(1, n_tok // BAND_TILE + 1):
        before = jnp.sum(jnp.where(tok == i * BAND_TILE - 1, count, 0.0), axis=1, keepdims=True)
        bounds = jnp.where(lane == i, before, bounds)
    bounds_ref[0] = bounds


def _select(aff, cap):
    B, E, T = aff.shape
    return pl.pallas_call(
        functools.partial(_select_kernel, cap),
        name="expert_select",
        grid=(B,),
        in_specs=[pl.BlockSpec((1, E, T), lambda b: (b, 0, 0))],
        out_specs=[pl.BlockSpec((1, E, T), lambda b: (b, 0, 0)), pl.BlockSpec((1, E, LANES), lambda b: (b, 0, 0))],
        out_shape=[jax.ShapeDtypeStruct((B, E, T), F32), jax.ShapeDtypeStruct((B, E, LANES), F32)],
        compiler_params=_cparams(("arbitrary",)),
    )(aff)


def _gather_kernel(cap, band_ref, h_ref, slot_ref, xs_ref):
    b, ti = pl.program_id(0), pl.program_id(1)
    n_experts = slot_ref.shape[2]
    win = GATHER_WINDOW
    log_win = win.bit_length() - 1

    @pl.when(ti == 0)
    def _():
        xs_ref[...] = jnp.zeros_like(xs_ref)

    h = h_ref[0]
    jw = lax.broadcasted_iota(jnp.int32, (win, 1), 0)

    def count_before(e, tile):
        return band_ref[(b * n_experts + e) * BAND_COLS + tile]

    def one_hot(e, first):
        return jnp.where(slot_ref[0, 0, e:e + 1, :] == (first + jw).astype(F32), 1.0, 0.0).astype(BF16)

    def add_rows(e, first, rows):
        xs_ref[0, e, pl.ds(pl.multiple_of(first, ROW_PACK), win), :] += rows.astype(BF16)

    firsts = [(count_before(e, ti) >> 4) << 4 for e in range(n_experts)]
    rows = jnp.dot(jnp.concatenate([one_hot(e, firsts[e]) for e in range(n_experts)], axis=0), h,
                   preferred_element_type=F32)
    for e in range(n_experts):
        add_rows(e, firsts[e], rows[e * win:(e + 1) * win])
    for e in range(n_experts):
        n_win = (count_before(e, ti + 1) - firsts[e] + win - 1) >> log_win

        def more(w, carry, e=e):
            first = firsts[e] + w * win
            add_rows(e, first, jnp.dot(one_hot(e, first), h, preferred_element_type=F32))
            return carry

        lax.fori_loop(1, n_win, more, 0)


def _expert_gather(band, h2, slot, cap):
    B, T, D = h2.shape
    E = slot.shape[1]
    nt = T // BAND_TILE
    slot_tiles = slot.reshape(B, E, nt, BAND_TILE).transpose(0, 2, 1, 3)
    rows = cap + GATHER_WINDOW
    grid_spec = pltpu.PrefetchScalarGridSpec(
        num_scalar_prefetch=1,
        grid=(B, nt),
        in_specs=[pl.BlockSpec((1, BAND_TILE, D), lambda b, t, band: (b, t, 0)),
                  pl.BlockSpec((1, 1, E, BAND_TILE), lambda b, t, band: (b, t, 0, 0))],
        out_specs=pl.BlockSpec((1, E, rows, D), lambda b, t, band: (b, 0, 0, 0)))
    return pl.pallas_call(
        functools.partial(_gather_kernel, cap),
        name="expert_gather",
        grid_spec=grid_spec,
        out_shape=jax.ShapeDtypeStruct((B, E, rows, D), BF16),
        compiler_params=_cparams(("arbitrary", "arbitrary")),
    )(band, h2, slot_tiles)


def _expert_kernel(cap, xs_ref, slot_ref, aff_ref, wg_ref, wu_ref, wd_ref, ye_ref):
    jj = lax.broadcasted_iota(jnp.int32, (cap, 1), 0).astype(F32)
    gate = jnp.zeros((cap, 1), F32)
    for ti in range(slot_ref.shape[2]):
        hit = slot_ref[0, 0, ti:ti + 1, :] == jj
        gate = gate + jnp.sum(jnp.where(hit, aff_ref[0, 0, ti:ti + 1, :], 0.0), axis=1, keepdims=True)
    xs = xs_ref[0, 0]
    hid = _silu(_dot(xs, wg_ref[0])) * _dot(xs, wu_ref[0])
    ye = _dot(hid, wd_ref[0]) * gate
    ye_ref[0, 0] = ye.astype(BF16)


def _expert_ffn(xs, slot, aff, wg, wu, wd, cap):
    B, E, _, D = xs.shape
    T = slot.shape[2]
    nt = T // BAND_TILE
    row = pl.BlockSpec((1, 1, nt, BAND_TILE), lambda e, b: (b, e, 0, 0))
    wspec = pl.BlockSpec((1, D, D), lambda e, b: (e, 0, 0))
    return pl.pallas_call(
        functools.partial(_expert_kernel, cap),
        name="expert_ffn",
        grid=(E, B),
        in_specs=[pl.BlockSpec((1, 1, cap, D), lambda e, b: (b, e, 0, 0)), row, row, wspec, wspec, wspec],
        out_specs=pl.BlockSpec((1, 1, cap, D), lambda e, b: (b, e, 0, 0)),
        out_shape=jax.ShapeDtypeStruct((B, E, cap, D), BF16),
        compiler_params=_cparams(("arbitrary", "arbitrary")),
    )(xs, slot.reshape(B, E, nt, BAND_TILE), aff.reshape(B, E, nt, BAND_TILE), wg, wu, wd)


def _combine_kernel(cap, band_ref, x1_ref, slot_ref, ye_ref, mod_ref, gain_ref, o_ref, acc_ref):
    win = min(SLOT_WINDOW, cap)
    log_win = win.bit_length() - 1
    b, i = pl.program_id(0), pl.program_id(1)
    tiles_per = OUT_TILE // BAND_TILE
    jw = lax.broadcasted_iota(jnp.int32, (win, 1), 0)
    contract_rows = (((0,), (0,)), ((), ()))

    def slots_before(e, tile):
        return band_ref[(b * N_EXPERTS + e) * BAND_COLS + tile * tiles_per]

    def window(e, w):
        first = ((slots_before(e, i) >> 4) << 4) + w * win
        src = pl.multiple_of(jnp.minimum(first, cap - win), ROW_PACK)
        tok = slot_ref[0, e:e + 1, :]
        hit = jnp.logical_and(tok == (src + jw).astype(F32), tok >= first.astype(F32))
        return jnp.where(hit, 1.0, 0.0).astype(BF16), ye_ref[0, e, pl.ds(src, win), :]

    acc = jnp.zeros((OUT_TILE, D_MODEL), F32)
    for e in range(0, N_EXPERTS, 2):
        (h0, r0), (h1, r1) = window(e, 0), window(e + 1, 0)
        acc = acc + lax.dot_general(jnp.concatenate([h0, h1], axis=0), jnp.concatenate([r0, r1], axis=0),
                                    contract_rows, preferred_element_type=F32)
    acc_ref[...] = acc
    for e in range(N_EXPERTS):
        first = (slots_before(e, i) >> 4) << 4
        n_win = (slots_before(e, i + 1) - first + win - 1) >> log_win

        def more(w, carry, e=e):
            hit, rows = window(e, w)
            acc_ref[...] += lax.dot_general(hit, rows, contract_rows, preferred_element_type=F32)
            return carry

        lax.fori_loop(1, n_win, more, 0)
    acc = acc_ref[...]
    ms = jnp.mean(acc * acc, axis=-1, keepdims=True)
    gt2 = mod_ref[0][:, 5 * D_MODEL:6 * D_MODEL]
    o_ref[0] = x1_ref[0] + gt2 * (acc * lax.rsqrt(ms + NORM_EPS) * gain_ref[...])


def _combine(band, x1, slot, ye, mods, gain, cap):
    B, T, D = x1.shape
    E = ye.shape[1]
    grid_spec = pltpu.PrefetchScalarGridSpec(
        num_scalar_prefetch=1,
        grid=(B, T // OUT_TILE),
        in_specs=[pl.BlockSpec((1, OUT_TILE, D), lambda b, i, band: (b, i, 0)),
                  pl.BlockSpec((1, E, OUT_TILE), lambda b, i, band: (b, 0, i)),
                  pl.BlockSpec((1, E, cap, D), lambda b, i, band: (b, 0, 0, 0)),
                  pl.BlockSpec((1, 1, mods.shape[2]), lambda b, i, band: (b, 0, 0)),
                  pl.BlockSpec((1, D), lambda b, i, band: (0, 0))],
        out_specs=pl.BlockSpec((1, OUT_TILE, D), lambda b, i, band: (b, i, 0)),
        scratch_shapes=[pltpu.VMEM((OUT_TILE, D), F32)])
    return pl.pallas_call(
        functools.partial(_combine_kernel, cap),
        name="combine",
        grid_spec=grid_spec,
        out_shape=jax.ShapeDtypeStruct((B, T, D), F32),
        compiler_params=_cparams(("arbitrary", "arbitrary")),
    )(band, x1, slot, ye, mods, gain)


def _rope_tables(T):
    rows = T // GRID_W
    row = np.repeat(np.arange(rows, dtype=np.float32), GRID_W)
    col = np.tile(np.arange(GRID_W, dtype=np.float32), rows)
    n_freq = HEAD_DIM // 4
    freq = jnp.asarray(ROPE_BASE, F32) ** (-jnp.arange(n_freq, dtype=F32) / n_freq)
    ang = jnp.concatenate([row[:, None] * freq, col[:, None] * freq], axis=-1)
    cos, sin = jnp.cos(ang), jnp.sin(ang)
    reps = LANES // (HEAD_DIM // 2)
    cos_t = jnp.tile(cos, (1, reps))
    sin_t = jnp.tile(jnp.concatenate([-sin, sin], axis=-1), (1, reps // 2))
    return cos_t, sin_t


def kernel(x, c, ctx, c_ctx, w_mod, b_mod, norm_gains, w_in, rwkv_conv, rwkv_w0, rwkv_w2, rwkv_a0, rwkv_a2, rwkv_g2, rwkv_k_k, rwkv_k_a, rwkv_r_k, rwkv_lnx_w, rwkv_lnx_b, w_out, w_router, w_gate, w_up, w_down):
    B, T, D = x.shape
    assert D == D_MODEL and ctx.shape == (B, CTX_LEN, D) and T % OUT_TILE == 0 and w_mod.shape[0] == 1
    W = GROUP_WIDTH
    cap = max(1, EC_CAPACITY * T // N_EXPERTS)
    assert T // BAND_TILE < BAND_COLS

    n_rows = -(-(B + 1) // 8) * 8
    cc = jnp.concatenate([c, c_ctx[None, :], jnp.zeros((n_rows - B - 1, D), F32)], axis=0)
    mods = _modulation(cc, w_mod[0], b_mod).reshape(n_rows, 1, 6 * D)

    split = RET_COLS + 3 * W + 2 * DECAY_LORA + ICLR_LORA
    w_pad = jnp.concatenate([w_in[0][:, :split], jnp.zeros((D, LANES - ICLR_LORA), F32), w_in[0][:, split:]],
                            axis=1).astype(BF16)
    cs = split - RET_COLS
    cw = jnp.concatenate([rwkv_conv[0][:, :cs], jnp.zeros((3, LANES - ICLR_LORA), F32), rwkv_conv[0][:, cs:]], axis=1)
    zl = jnp.zeros((DECAY_LORA, W), F32)
    w2bd = jnp.concatenate([jnp.concatenate([rwkv_w2[0, 0], zl], axis=1),
                            jnp.concatenate([zl, rwkv_w2[0, 1]], axis=1)], axis=0).astype(BF16)
    w0cat = jnp.concatenate([rwkv_w0[0, 0], rwkv_w0[0, 1]])[None, :]
    a2pad = jnp.concatenate([rwkv_a2[0], jnp.zeros((LANES - ICLR_LORA, W), F32)], axis=0).astype(BF16)
    g2f, g2b = rwkv_g2[0, 0].astype(BF16), rwkv_g2[0, 1].astype(BF16)
    row = lambda a: a[0][None, :]
    cos_t, sin_t = _rope_tables(T)

    feature_consts = (cw, w0cat, w2bd, row(rwkv_a0), a2pad, g2f, g2b, row(rwkv_k_k), row(rwkv_k_a), row(rwkv_r_k))
    u_ret, r, k, v, a, b, lwf, lwb, gf, gb, bonus = _projection(
        x, ctx, mods, norm_gains[0, 0][None, :], w_pad, cos_t, sin_t, feature_consts)
    ret_f, ret_b = _retention(u_ret, T)
    rw_f, rw_b = _rwkv_scan(r, k, v, a, b, lwf, lwb, gf, gb, row(rwkv_lnx_w), row(rwkv_lnx_b), T)

    x1, h2, aff = _out_projection(x, ret_f, ret_b, rw_f, rw_b, bonus, w_out[0].astype(BF16), mods,
                                  norm_gains[0], w_router[0].T)
    slot, bounds = _select(aff, cap)
    band = bounds[:, :, :BAND_COLS].astype(jnp.int32).reshape(-1)
    xs = _expert_gather(band, h2, slot, cap)
    ye = _expert_ffn(xs, slot, aff, w_gate[0], w_up[0], w_down[0], cap)
    return _combine(band, x1, slot, ye, mods, norm_gains[0, 3][None, :], cap)
```

```python
import functools

import numpy as np
import jax
import jax.numpy as jnp
from jax import lax
from jax.experimental import pallas as pl
from jax.experimental.pallas import tpu as pltpu

F32 = jnp.float32
BF16 = jnp.bfloat16
HIGHEST = lax.Precision.HIGHEST

D_MODEL = 1024
CTX_LEN = 256
GRID_W = 64
HEAD_DIM = 64
GROUP_WIDTH = 512
LANES = 128
N_PAIRS = GROUP_WIDTH // LANES
HEADS_PER_GROUP = 4
GROUP_LANES = HEADS_PER_GROUP * HEAD_DIM
N_GROUPS = GROUP_WIDTH // GROUP_LANES
RET_COLS = 5 * GROUP_WIDTH
RWKV_COLS_PAD = 4 * GROUP_WIDTH
DECAY_LORA = 64
ICLR_LORA = 64
GATE_LORA = 128
N_EXPERTS = 16
EC_CAPACITY = 2
ROPE_BASE = 10000.0
NORM_EPS = 1e-6
RWKV_GN_EPS = 64e-5

TOK_TILE = 256
RET_CHUNK = 256
RWKV_CHUNK = 64
RWKV_CHUNKS_PER_STEP = 4
OUT_TILE = 512
BAND_TILE = 256
MOD_COL_TILE = 1536
ROW_PACK_SHIFT = 4
ROW_PACK = 1 << ROW_PACK_SHIFT
GATHER_WINDOW = 64
SLOT_WINDOW = 128
BAND_COLS = 32
VMEM_LIMIT = 56 * 1024 * 1024


def _cparams(sem):
    return pltpu.CompilerParams(dimension_semantics=sem, vmem_limit_bytes=VMEM_LIMIT)


def _dot(a, b):
    return jnp.dot(a.astype(BF16), b.astype(BF16), preferred_element_type=F32)


def _dot_nt(a, b):
    return lax.dot_general(a.astype(BF16), b.astype(BF16), (((1,), (1,)), ((), ())),
                           preferred_element_type=F32)


def _dot_exact(a, b):
    return jnp.dot(a, b, precision=HIGHEST, preferred_element_type=F32)


def _head_block_mask(n):
    r = lax.broadcasted_iota(jnp.int32, (n, n), 0)
    c = lax.broadcasted_iota(jnp.int32, (n, n), 1)
    return (r // HEAD_DIM) == (c // HEAD_DIM)


def _split_bf16(x, terms):
    out = []
    for _ in range(terms - 1):
        hi = x.astype(BF16)
        out.append(hi)
        x = x - hi.astype(F32)
    out.append(x.astype(BF16))
    return out


def _head_sum(x, scale=1.0):
    cols = []
    for g in range(x.shape[1] // GROUP_LANES):
        m = jnp.where(_head_block_mask(GROUP_LANES), scale, 0.0).astype(BF16)
        parts = _split_bf16(x[:, g * GROUP_LANES:(g + 1) * GROUP_LANES], 2)
        cols.append(jnp.dot(jnp.concatenate(parts, axis=1), jnp.concatenate([m, m], axis=0),
                            preferred_element_type=F32))
    return cols[0] if len(cols) == 1 else jnp.concatenate(cols, axis=1)


def _silu(x):
    return x * jax.nn.sigmoid(x)


def _round_robin(gens):
    results = [None] * len(gens)
    active = list(range(len(gens)))
    while active:
        for i in list(active):
            try:
                next(gens[i])
            except StopIteration as done:
                results[i] = done.value
                active.remove(i)
    return results


def _mod_kernel(c_ref, w_ref, b_ref, o_ref):
    o_ref[...] = _dot_exact(_silu(c_ref[...]), w_ref[...]) + b_ref[...]


def _modulation(cc, w_mod, b_mod):
    rows, d = cc.shape
    n = w_mod.shape[1]
    tn = MOD_COL_TILE
    return pl.pallas_call(
        _mod_kernel,
        name="modulation",
        grid=(n // tn,),
        in_specs=[pl.BlockSpec((rows, d), lambda i: (0, 0)),
                  pl.BlockSpec((d, tn), lambda i: (0, i)),
                  pl.BlockSpec((1, tn), lambda i: (0, i))],
        out_specs=pl.BlockSpec((rows, tn), lambda i: (0, i)),
        out_shape=jax.ShapeDtypeStruct((rows, n), F32),
        compiler_params=_cparams(("arbitrary",)),
    )(cc, w_mod, b_mod)


def _proj_kernel(n_tiles, x_ref, ctx_ref, xp_ref, xn_ref, mod_ref, gain_ref, w_ref, cos_ref, sin_ref, *rest):
    feature_consts, ret_ref, feature_outs = rest[:10], rest[10], rest[11:]
    j = pl.program_id(1)
    is_lat = j > 0
    xin = jnp.concatenate([jnp.where(is_lat, x_ref[0], ctx_ref[0]), xp_ref[0], xn_ref[0]], axis=0)
    ms = jnp.mean(xin * xin, axis=-1, keepdims=True)
    y = xin * lax.rsqrt(ms + NORM_EPS) * gain_ref[...]
    m = mod_ref[0]
    h = (y * (1.0 + m[:, D_MODEL:2 * D_MODEL]) + m[:, 0:D_MODEL]).astype(BF16)
    u_rw = jnp.dot(h, w_ref[:, RET_COLS:], preferred_element_type=F32)
    has_prev = (j >= 2).astype(F32)
    has_next = jnp.logical_and(j >= 1, j <= n_tiles - 2).astype(F32)
    features = _rwkv_features(u_rw[0:TOK_TILE], u_rw[TOK_TILE + 7:TOK_TILE + 8] * has_prev,
                              u_rw[TOK_TILE + 8:TOK_TILE + 9] * has_next, *feature_consts, *feature_outs)

    def retention_columns():
        lat = is_lat.astype(F32)
        cos = cos_ref[...] * lat + (1.0 - lat)
        sin = sin_ref[...] * lat
        lane = lax.broadcasted_iota(jnp.int32, (1, LANES), 1)
        first_half = (lane % HEAD_DIM) < (HEAD_DIM // 2)
        W = GROUP_WIDTH
        for c, scale in enumerate((1.0, HEAD_DIM ** -0.5, None, None, None)):
            u = jnp.dot(h[0:TOK_TILE], w_ref[:, c * W:(c + 1) * W], preferred_element_type=F32)
            yield
            if scale is None:
                ret_ref[0, :, c * W:(c + 1) * W] = u
                continue
            for g in range(N_PAIRS):
                t = u[:, g * LANES:(g + 1) * LANES] * scale
                sw = jnp.where(first_half, pltpu.roll(t, LANES - HEAD_DIM // 2, 1), pltpu.roll(t, HEAD_DIM // 2, 1))
                ret_ref[0, :, c * W + g * LANES:c * W + (g + 1) * LANES] = t * cos + sw * sin

    _round_robin([features, retention_columns()])


def _projection(x, ctx, mods, gain, w_pad, cos_t, sin_t, feature_consts):
    B, T, D = x.shape
    nt = (T + CTX_LEN) // TOK_TILE
    S = T + CTX_LEN
    rb = TOK_TILE // 8
    const = lambda a: pl.BlockSpec(a.shape, lambda b, j: (0,) * a.ndim)
    seq_spec = pl.BlockSpec((1, TOK_TILE, GROUP_WIDTH), lambda b, j: (b, j, 0))
    lat_spec = pl.BlockSpec((1, TOK_TILE, GROUP_WIDTH), lambda b, j: (b, jnp.maximum(j - 1, 0), 0))
    seq_shape = jax.ShapeDtypeStruct((B, S, GROUP_WIDTH), F32)
    lat_shape = jax.ShapeDtypeStruct((B, T, GROUP_WIDTH), BF16)
    return pl.pallas_call(
        functools.partial(_proj_kernel, nt),
        name="projection",
        grid=(B, nt),
        in_specs=[
            pl.BlockSpec((1, TOK_TILE, D), lambda b, j: (b, jnp.maximum(j - 1, 0), 0)),
            pl.BlockSpec((1, TOK_TILE, D), lambda b, j: (b, 0, 0)),
            pl.BlockSpec((1, 8, D), lambda b, j: (b, jnp.maximum((j - 1) * rb - 1, 0), 0)),
            pl.BlockSpec((1, 8, D), lambda b, j: (b, jnp.clip(j * rb, 0, T // 8 - 1), 0)),
            pl.BlockSpec((1, 1, mods.shape[2]), lambda b, j: (jnp.where(j == 0, B, b), 0, 0)),
            pl.BlockSpec((1, D), lambda b, j: (0, 0)),
            const(w_pad),
            pl.BlockSpec((TOK_TILE, LANES), lambda b, j: (jnp.maximum(j - 1, 0), 0)),
            pl.BlockSpec((TOK_TILE, LANES), lambda b, j: (jnp.maximum(j - 1, 0), 0)),
        ] + [const(a) for a in feature_consts],
        out_specs=[pl.BlockSpec((1, TOK_TILE, RET_COLS), lambda b, j: (b, j, 0))] + [seq_spec] * 7 + [lat_spec] * 3,
        out_shape=[jax.ShapeDtypeStruct((B, S, RET_COLS), F32)] + [seq_shape] * 7 + [lat_shape] * 3,
        compiler_params=_cparams(("arbitrary", "arbitrary")),
    )(x, ctx, x, x, mods, gain, w_pad, cos_t, sin_t, *feature_consts)


def _ret_group(d, gi, q_ref, k_ref, v_ref, g_ref, o_ref, s_ref, dmask_ref, qdec_ref, kdec_ref, cdec_ref):
    C = RET_CHUNK
    lane_head = lax.broadcasted_iota(jnp.int32, (1, GROUP_LANES), 1) // HEAD_DIM
    sl = slice(gi * GROUP_LANES, (gi + 1) * GROUP_LANES)
    q4, k4, v4 = q_ref[0, :, sl], k_ref[0, :, sl], v_ref[0, :, sl]
    state = s_ref[gi]
    q_s = jnp.concatenate([jnp.where(lane_head == hh, q4, 0.0) for hh in range(HEADS_PER_GROUP)], axis=0)
    scores = _dot_nt(q_s, k4) * dmask_ref[d, gi]
    cross = _dot(q4 * qdec_ref[d, :, sl], state)
    yield
    o_s = _dot(scores, v4)
    kd = (k4 * kdec_ref[d, :, sl]).T
    s_ref[gi] = state * cdec_ref[:, sl] + jnp.where(_head_block_mask(GROUP_LANES), _dot(kd, v4), 0.0)
    yield
    y = cross + jnp.where(lane_head == 0, o_s[0:C], 0.0)
    for hh in range(1, HEADS_PER_GROUP):
        y = y + jnp.where(lane_head == hh, o_s[hh * C:(hh + 1) * C], 0.0)
    ms = _head_sum(y * y, 1.0 / HEAD_DIM)
    o_ref[0, :, sl] = (_silu(g_ref[0, :, sl]) * (y * lax.rsqrt(ms + NORM_EPS))).astype(o_ref.dtype)


def _ret_kernel(qf, kf, vf, gf, qb, kb, vb, gb, dmask, qdec, kdec, cdec, of, ob, sf, sb):
    @pl.when(pl.program_id(1) == 0)
    def _():
        sf[...] = jnp.zeros_like(sf)
        sb[...] = jnp.zeros_like(sb)

    tables = (dmask, qdec, kdec, cdec)
    _round_robin([_ret_group(d, gi, *refs, *tables)
                  for d, refs in ((0, (qf, kf, vf, gf, of, sf)), (1, (qb, kb, vb, gb, ob, sb)))
                  for gi in range(N_GROUPS)])


def _retention_tables():
    n_heads = GROUP_WIDTH // HEAD_DIM
    log_gamma = np.log1p(-np.exp2(-5.0 - np.arange(n_heads, dtype=np.float64)))
    idx = np.arange(RET_CHUNK, dtype=np.float64)
    rel = idx[:, None] - idx[None, :]
    dm = np.where(rel >= 0, np.exp(np.maximum(rel, 0.0) * log_gamma[:, None, None]), 0.0)
    dmask = np.stack([dm, dm.transpose(0, 2, 1)])
    dmask = dmask.reshape(2, N_GROUPS, HEADS_PER_GROUP * RET_CHUNK, RET_CHUNK)
    pos = np.stack([idx, RET_CHUNK - 1.0 - idx])
    lg = np.repeat(log_gamma, HEAD_DIM)
    qdec = np.exp((pos[:, :, None] + 1.0) * lg)
    kdec = np.exp((RET_CHUNK - 1.0 - pos[:, :, None]) * lg)
    cdec = np.exp(RET_CHUNK * lg)[None, :]
    return [jnp.asarray(a, F32) for a in (dmask, qdec, kdec, cdec)]


def _retention(u_ret, T):
    B = u_ret.shape[0]
    n_lat = T // RET_CHUNK
    ns = n_lat + 1
    dmask, qdec, kdec, cdec = _retention_tables()

    def fchunk(s):
        return s

    def bchunk(s):
        return jnp.where(s == 0, 0, ns - s)

    def col(c, chunk):
        return pl.BlockSpec((1, RET_CHUNK, GROUP_WIDTH), lambda b, s: (b, chunk(s), c))

    const = lambda shape: pl.BlockSpec(shape, lambda b, s: (0,) * len(shape))
    in_specs = ([col(c, fchunk) for c in (0, 1, 2, 3)] + [col(c, bchunk) for c in (0, 1, 2, 4)]
                + [const(dmask.shape), const(qdec.shape), const(kdec.shape), const(cdec.shape)])
    out_specs = [pl.BlockSpec((1, RET_CHUNK, GROUP_WIDTH), lambda b, s: (b, jnp.maximum(s - 1, 0), 0)),
                 pl.BlockSpec((1, RET_CHUNK, GROUP_WIDTH), lambda b, s: (b, jnp.where(s == 0, n_lat - 1, n_lat - s), 0))]
    out_shape = [jax.ShapeDtypeStruct((B, T, GROUP_WIDTH), BF16)] * 2
    return pl.pallas_call(
        _ret_kernel,
        name="retention",
        grid=(B, ns),
        in_specs=in_specs,
        out_specs=out_specs,
        out_shape=out_shape,
        scratch_shapes=[pltpu.VMEM((N_GROUPS, GROUP_LANES, GROUP_LANES), F32)] * 2,
        compiler_params=_cparams(("arbitrary", "arbitrary")),
    )(*([u_ret] * 8), dmask, qdec, kdec, cdec)


def _rwkv_features(x, prev_row, next_row, cw_ref, w0_ref, w2_ref, a0_ref, a2_ref, g2f_ref, g2b_ref,
                   kk_ref, ka_ref, rk_ref,
                   r_o, k_o, v_o, a_o, b_o, lwf_o, lwb_o, gf_o, gb_o, bon_o):
    rows = lax.broadcasted_iota(jnp.int32, (TOK_TILE, 1), 0)
    xm = jnp.where(rows == 0, prev_row, pltpu.roll(x, 1, 0))
    xp = jnp.where(rows == TOK_TILE - 1, next_row, pltpu.roll(x, TOK_TILE - 1, 0))
    rw = cw_ref[0:1, :] * xm + cw_ref[1:2, :] * x + cw_ref[2:3, :] * xp
    yield

    W = GROUP_WIDTH
    r, k, v, lo = rw[:, 0:W], rw[:, W:2 * W], rw[:, 2 * W:3 * W], rw[:, 3 * W:4 * W]
    z = w0_ref[...] + _dot(jnp.tanh(lo[:, 0:LANES]), w2_ref[...])
    log_decay = -np.exp(-0.5).astype(np.float32) * jax.nn.sigmoid(z)
    iclr = jax.nn.sigmoid(a0_ref[...] + _dot(lo[:, LANES:2 * LANES], a2_ref[...]))
    yield
    g_f = _dot(jax.nn.sigmoid(lo[:, 2 * LANES:3 * LANES]), g2f_ref[...])
    g_b = _dot(jax.nn.sigmoid(lo[:, 3 * LANES:4 * LANES]), g2b_ref[...])
    kk = k * kk_ref[...]
    yield
    kk = kk / jnp.maximum(jnp.sqrt(_head_sum(kk * kk)), 1e-12)
    k_mod = k * (1.0 + (iclr - 1.0) * ka_ref[...])
    yield
    bonus = _head_sum(r * k_mod * rk_ref[...]) * v
    r_o[0] = r
    k_o[0] = k_mod
    v_o[0] = v
    a_o[0] = -kk
    b_o[0] = kk * iclr
    lwf_o[0] = log_decay[:, 0:W]
    lwb_o[0] = log_decay[:, W:2 * W]
    gf_o[0] = g_f.astype(gf_o.dtype)
    gb_o[0] = g_b.astype(gb_o.dtype)
    bon_o[0] = ((g_f + g_b) * bonus).astype(bon_o.dtype)


def _dplr_local(r, k, v, a, b, lw, reverse):
    C, H, GL = RWKV_CHUNK, HEADS_PER_GROUP, GROUP_LANES
    CS = H * C
    ri = lax.broadcasted_iota(jnp.int32, (C, C), 0)
    ci = lax.broadcasted_iota(jnp.int32, (C, C), 1)
    tri = jnp.where((ci >= ri) if reverse else (ci <= ri), 1.0, 0.0).astype(BF16)
    lc = jnp.dot(jnp.concatenate([tri, tri, tri], axis=1), jnp.concatenate(_split_bf16(lw, 3), axis=0),
                 preferred_element_type=F32)
    yield
    ltot = jnp.sum(lw, axis=0, keepdims=True)
    a_t = a * jnp.exp(lc - lw)
    r_t = r * jnp.exp(lc)
    inv = jnp.exp(-lc)
    b_t, k_t = b * inv, k * inv
    to_end = jnp.exp(ltot - lc)
    b_e, k_e = b * to_end, k * to_end

    rs = lax.broadcasted_iota(jnp.int32, (CS, CS), 0)
    cs = lax.broadcasted_iota(jnp.int32, (CS, CS), 1)
    bd = (rs // C) == (cs // C)
    stack = lambda x: jnp.where(bd, jnp.concatenate([x] * H, axis=0), 0.0)
    unstack = lambda x: x[0:C] + x[C:2 * C] + x[2 * C:3 * C] + x[3 * C:4 * C]
    a_s, r_s, v_s = stack(a_t), stack(r_t), stack(v)
    g = _dot_nt(jnp.concatenate([a_s, r_s], axis=0), jnp.concatenate([b_t, k_t], axis=0))
    g_swapped = pltpu.roll(g, C, 1)
    t128 = lax.broadcasted_iota(jnp.int32, (C, 2 * C), 0)
    l128 = lax.broadcasted_iota(jnp.int32, (C, 2 * C), 1)
    s128, half128 = l128 % C, l128 // C
    incl128 = (s128 >= t128) if reverse else (s128 <= t128)
    strict128 = (s128 > t128) if reverse else (s128 < t128)
    zeros128 = jnp.zeros((C, 2 * C), F32)

    def block_diagonal(row0, from_k, tri):
        blocks = []
        for hh in range(H):
            half = hh % 2
            src = g if half == int(from_k) else g_swapped
            piece = jnp.where(jnp.logical_and(half128 == half, tri), src[row0 + hh * C:row0 + (hh + 1) * C], 0.0)
            blocks.append(jnp.concatenate([piece if c == hh // 2 else zeros128 for c in range(H // 2)], axis=1))
        return jnp.concatenate(blocks, axis=0)

    low = block_diagonal(0, False, strict128)
    ak = block_diagonal(0, True, strict128)
    rb = block_diagonal(CS, False, incl128)
    rk = block_diagonal(CS, True, incl128)
    yield
    akv = _dot(ak, v_s)
    tm = jnp.where(rs == cs, 1.0, 0.0) + low
    pw = _dot(low, low)
    yield
    n = 2
    while n < C // 2:
        both = _dot(jnp.concatenate([pw, tm], axis=0), pw)
        pw, tm = both[0:CS], tm + both[CS:]
        n *= 2
        yield
    tm = tm + _dot(tm, pw)
    yield
    uw = _dot(tm, jnp.concatenate([akv, a_s], axis=1))
    yield
    y0_s = _dot(jnp.concatenate([rb, rk], axis=1), jnp.concatenate([uw[:, 0:GL], v_s], axis=0))
    qa_s = _dot(rb, uw[:, GL:])
    yield
    u0, w = unstack(uw[:, 0:GL]), unstack(uw[:, GL:])
    y0, q_add = unstack(y0_s), unstack(qa_s)
    zeros = jnp.zeros((C, GL), F32)
    stack_t = jnp.concatenate([w, u0, v, zeros], axis=0).T
    rm = jnp.concatenate([jnp.concatenate([b_e, zeros], axis=1),
                          jnp.concatenate([zeros, b_e], axis=1),
                          jnp.concatenate([zeros, k_e], axis=1),
                          jnp.concatenate([zeros, zeros], axis=1)], axis=0)
    mn = _dot(stack_t, rm)
    return (r_t + q_add, y0, jnp.exp(ltot), jnp.where(bd, mn[:, 0:GL], 0.0), jnp.where(bd, mn[:, GL:], 0.0))


def _scan_kernel(rf, kf, vf, af, bf, lwf, gf, rb, kb, vb, ab, bb, lwb, gb, lnw, lnb, of, ob, sf, sb):
    @pl.when(pl.program_id(1) == 0)
    def _():
        sf[...] = jnp.zeros_like(sf)
        sb[...] = jnp.zeros_like(sb)

    C = RWKV_CHUNK
    fwd = (False, (rf, kf, vf, af, bf, lwf), gf, of, sf)
    bwd = (True, (rb, kb, vb, ab, bb, lwb), gb, ob, sb)
    chains = []
    for reverse, refs, g_ref, o_ref, s_ref in (fwd, bwd):
        subs = range(RWKV_CHUNKS_PER_STEP)
        for gi in range(N_GROUPS):
            sl = slice(gi * GROUP_LANES, (gi + 1) * GROUP_LANES)
            for sub in (reversed(subs) if reverse else subs):
                rows = slice(sub * C, (sub + 1) * C)
                chains.append((gi, sl, rows, g_ref, o_ref, s_ref,
                               _dplr_local(*[ref[0, rows, sl] for ref in refs], reverse)))
    local = _round_robin([c[-1] for c in chains])

    def carry_state(s_ref, gi, parts):
        st = s_ref[gi]
        ys = []
        for q, y0, decay, m_t, n_t in parts:
            ys.append(_dot_nt(q, st) + y0)
            st = st * decay + _dot(st, m_t) + n_t
            yield
        s_ref[gi] = st
        return ys

    per = RWKV_CHUNKS_PER_STEP
    ys = _round_robin([carry_state(chains[i][5], chains[i][0], local[i:i + per]) for i in range(0, len(chains), per)])
    y_all = jnp.concatenate([y for group in ys for y in group], axis=0)
    dlt = y_all - _head_sum(y_all, 1.0 / HEAD_DIM)
    yn = dlt * lax.rsqrt(_head_sum(dlt * dlt, 1.0 / HEAD_DIM) + RWKV_GN_EPS)
    for i, (gi, sl, rows, g_ref, o_ref, s_ref, _) in enumerate(chains):
        out = g_ref[0, rows, sl].astype(F32) * (yn[i * C:(i + 1) * C] * lnw[:, sl] + lnb[:, sl])
        o_ref[0, rows, sl] = out.astype(o_ref.dtype)


def _rwkv_scan(r, k, v, a, b, lwf, lwb, gf, gb, lnw, lnb, T):
    B, S, W = r.shape
    C = RWKV_CHUNK * RWKV_CHUNKS_PER_STEP
    n_ctx = CTX_LEN // C
    n_lat = T // C
    ns = n_ctx + n_lat

    def fchunk(s):
        return s

    def bchunk(s):
        return jnp.where(s < n_ctx, n_ctx - 1 - s, ns + n_ctx - 1 - s)

    def fout(s):
        return jnp.maximum(s - n_ctx, 0)

    def bout(s):
        return jnp.where(s < n_ctx, n_lat - 1, ns - 1 - s)

    def spec(chunk):
        return pl.BlockSpec((1, C, W), lambda bi, s: (bi, chunk(s), 0))

    const = pl.BlockSpec((1, W), lambda bi, s: (0, 0))
    in_specs = [spec(fchunk)] * 6 + [spec(fout)] + [spec(bchunk)] * 6 + [spec(bout)] + [const, const]
    return pl.pallas_call(
        _scan_kernel,
        name="rwkv_scan",
        grid=(B, ns),
        in_specs=in_specs,
        out_specs=[spec(fout), spec(bout)],
        out_shape=[jax.ShapeDtypeStruct((B, T, W), BF16)] * 2,
        scratch_shapes=[pltpu.VMEM((N_GROUPS, GROUP_LANES, GROUP_LANES), F32)] * 2,
        compiler_params=_cparams(("arbitrary", "arbitrary")),
    )(r, k, v, a, b, lwf, gf, r, k, v, a, b, lwb, gb, lnw, lnb)


def _out_kernel(x_ref, rf_ref, rb_ref, wf_ref, wb_ref, bon_ref, wo_ref, mod_ref, gains_ref, wr_ref,
                x1_ref, h2_ref, aff_ref):
    W = GROUP_WIDTH
    ret = rf_ref[0].astype(F32) + rb_ref[0].astype(F32)
    rwk = wf_ref[0].astype(F32) + wb_ref[0].astype(F32) + bon_ref[0].astype(F32)
    mix = _dot(ret, wo_ref[0:W, :]) + _dot(rwk, wo_ref[W:2 * W, :])
    m = mod_ref[0]
    D = D_MODEL
    ms = jnp.mean(mix * mix, axis=-1, keepdims=True)
    x1 = x_ref[0] + m[:, 2 * D:3 * D] * (mix * lax.rsqrt(ms + NORM_EPS) * gains_ref[1:2, :])
    x1_ref[0] = x1
    ms2 = jnp.mean(x1 * x1, axis=-1, keepdims=True)
    h2 = (x1 * lax.rsqrt(ms2 + NORM_EPS) * gains_ref[2:3, :]) * (1.0 + m[:, 4 * D:5 * D]) + m[:, 3 * D:4 * D]
    h2_ref[0] = h2.astype(BF16)
    h_hi, h_lo = _split_bf16(h2, 2)
    w_hi, w_lo = _split_bf16(wr_ref[...], 2)
    logits = lax.dot_general(jnp.concatenate([w_hi, w_lo, w_hi], axis=1), jnp.concatenate([h_hi, h_hi, h_lo], axis=1),
                             (((1,), (1,)), ((), ())), preferred_element_type=F32)
    e = jnp.exp(logits - jnp.max(logits, axis=0, keepdims=True))
    aff_ref[0] = e / jnp.sum(e, axis=0, keepdims=True)


def _out_projection(x, ret_f, ret_b, rw_f, rw_b, bonus, w_out, mods, gains, wr_pad):
    B, T, D = x.shape
    nt = T // OUT_TILE
    tok = lambda w: pl.BlockSpec((1, OUT_TILE, w), lambda b, i: (b, i, 0))
    const = lambda a: pl.BlockSpec(a.shape, lambda b, i: (0,) * a.ndim)
    return pl.pallas_call(
        _out_kernel,
        name="out_projection",
        grid=(B, nt),
        in_specs=[tok(D)] + [tok(GROUP_WIDTH)] * 5 + [const(w_out),
                  pl.BlockSpec((1, 1, mods.shape[2]), lambda b, i: (b, 0, 0)), const(gains), const(wr_pad)],
        out_specs=[tok(D), tok(D), pl.BlockSpec((1, N_EXPERTS, OUT_TILE), lambda b, i: (b, 0, i))],
        out_shape=[jax.ShapeDtypeStruct((B, T, D), F32), jax.ShapeDtypeStruct((B, T, D), BF16),
                   jax.ShapeDtypeStruct((B, N_EXPERTS, T), F32)],
        compiler_params=_cparams(("arbitrary", "arbitrary")),
    )(x, ret_f, ret_b, rw_f, rw_b, bonus, w_out, mods, gains, wr_pad)


def _cumsum_lanes(x):
    n = x.shape[1]
    lane = lax.broadcasted_iota(jnp.int32, (1, n), 1)
    sh = 1
    while sh < n:
        x = x + jnp.where(lane >= sh, pltpu.roll(x, sh, 1), 0.0)
        sh *= 2
    return x


def _select_kernel(cap, aff_ref, slot_ref, bounds_ref):
    aff = aff_ref[0]
    capf = jnp.float32(cap)

    def body(i, thr):
        cand = thr | (jnp.int32(1) << (30 - i))
        cnt = jnp.sum(jnp.where(aff >= pltpu.bitcast(cand, F32), 1.0, 0.0), axis=1, keepdims=True)
        return jnp.where(cnt >= capf, cand, thr)

    thr = lax.fori_loop(0, 31, body, jnp.zeros((aff.shape[0], 1), jnp.int32))
    min_normal_bits = jnp.int32(0x00800000)
    above = pltpu.bitcast(jnp.maximum(thr + 1, min_normal_bits), F32)
    gt = aff >= above
    eq = jnp.where(jnp.logical_and(aff >= pltpu.bitcast(thr, F32), jnp.logical_not(gt)), 1.0, 0.0)
    need = capf - jnp.sum(jnp.where(gt, 1.0, 0.0), axis=1, keepdims=True)
    eq_before = _cumsum_lanes(eq) - eq
    sel = jnp.where(jnp.logical_or(gt, jnp.logical_and(eq > 0.0, eq_before < need)), 1.0, 0.0)
    count = _cumsum_lanes(sel)
    slot = jnp.where(sel > 0.0, count - sel, -1.0)
    slot_ref[0] = slot

    n_experts, n_tok = slot.shape
    tok = lax.broadcasted_iota(jnp.int32, (1, n_tok), 1)
    lane = lax.broadcasted_iota(jnp.int32, (1, LANES), 1)
    bounds = jnp.zeros((n_experts, LANES), F32)
    for i in range(1, n_tok // BAND_TILE + 1):
        before = jnp.sum(jnp.where(tok == i * BAND_TILE - 1, count, 0.0), axis=1, keepdims=True)
        bounds = jnp.where(lane == i, before, bounds)
    bounds_ref[0] = bounds


def _select(aff, cap):
    B, E, T = aff.shape
    return pl.pallas_call(
        functools.partial(_select_kernel, cap),
        name="expert_select",
        grid=(B,),
        in_specs=[pl.BlockSpec((1, E, T), lambda b: (b, 0, 0))],
        out_specs=[pl.BlockSpec((1, E, T), lambda b: (b, 0, 0)), pl.BlockSpec((1, E, LANES), lambda b: (b, 0, 0))],
        out_shape=[jax.ShapeDtypeStruct((B, E, T), F32), jax.ShapeDtypeStruct((B, E, LANES), F32)],
        compiler_params=_cparams(("arbitrary",)),
    )(aff)


def _gather_kernel(cap, band_ref, h_ref, slot_ref, xs_ref):
    b, ti = pl.program_id(0), pl.program_id(1)
    n_experts = slot_ref.shape[2]
    win = GATHER_WINDOW
    log_win = win.bit_length() - 1

    @pl.when(ti == 0)
    def _():
        xs_ref[...] = jnp.zeros_like(xs_ref)

    h = h_ref[0]
    jw = lax.broadcasted_iota(jnp.int32, (win, 1), 0)

    def count_before(e, tile):
        return band_ref[(b * n_experts + e) * BAND_COLS + tile]

    def one_hot(e, first):
        return jnp.where(slot_ref[0, 0, e:e + 1, :] == (first + jw).astype(F32), 1.0, 0.0).astype(BF16)

    def add_rows(e, first, rows):
        xs_ref[0, e, pl.ds(pl.multiple_of(first, ROW_PACK), win), :] += rows.astype(BF16)

    firsts = [(count_before(e, ti) >> ROW_PACK_SHIFT) << ROW_PACK_SHIFT for e in range(n_experts)]
    rows = jnp.dot(jnp.concatenate([one_hot(e, firsts[e]) for e in range(n_experts)], axis=0), h,
                   preferred_element_type=F32)
    for e in range(n_experts):
        add_rows(e, firsts[e], rows[e * win:(e + 1) * win])
    for e in range(n_experts):
        n_win = (count_before(e, ti + 1) - firsts[e] + win - 1) >> log_win

        def more(w, carry, e=e):
            first = firsts[e] + w * win
            add_rows(e, first, jnp.dot(one_hot(e, first), h, preferred_element_type=F32))
            return carry

        lax.fori_loop(1, n_win, more, 0)


def _expert_gather(band, h2, slot, cap):
    B, T, D = h2.shape
    E = slot.shape[1]
    nt = T // BAND_TILE
    slot_tiles = slot.reshape(B, E, nt, BAND_TILE).transpose(0, 2, 1, 3)
    rows = cap + GATHER_WINDOW
    grid_spec = pltpu.PrefetchScalarGridSpec(
        num_scalar_prefetch=1,
        grid=(B, nt),
        in_specs=[pl.BlockSpec((1, BAND_TILE, D), lambda b, t, band: (b, t, 0)),
                  pl.BlockSpec((1, 1, E, BAND_TILE), lambda b, t, band: (b, t, 0, 0))],
        out_specs=pl.BlockSpec((1, E, rows, D), lambda b, t, band: (b, 0, 0, 0)))
    return pl.pallas_call(
        functools.partial(_gather_kernel, cap),
        name="expert_gather",
        grid_spec=grid_spec,
        out_shape=jax.ShapeDtypeStruct((B, E, rows, D), BF16),
        compiler_params=_cparams(("arbitrary", "arbitrary")),
    )(band, h2, slot_tiles)


def _expert_kernel(cap, xs_ref, slot_ref, aff_ref, wg_ref, wu_ref, wd_ref, ye_ref):
    jj = lax.broadcasted_iota(jnp.int32, (cap, 1), 0).astype(F32)
    gate = jnp.zeros((cap, 1), F32)
    for ti in range(slot_ref.shape[2]):
        hit = slot_ref[0, 0, ti:ti + 1, :] == jj
        gate = gate + jnp.sum(jnp.where(hit, aff_ref[0, 0, ti:ti + 1, :], 0.0), axis=1, keepdims=True)
    xs = xs_ref[0, 0]
    hid = _silu(_dot(xs, wg_ref[0])) * _dot(xs, wu_ref[0])
    ye = _dot(hid, wd_ref[0]) * gate
    ye_ref[0, 0] = ye.astype(BF16)


def _expert_ffn(xs, slot, aff, wg, wu, wd, cap):
    B, E, _, D = xs.shape
    T = slot.shape[2]
    nt = T // BAND_TILE
    row = pl.BlockSpec((1, 1, nt, BAND_TILE), lambda e, b: (b, e, 0, 0))
    wspec = pl.BlockSpec((1, D, D), lambda e, b: (e, 0, 0))
    return pl.pallas_call(
        functools.partial(_expert_kernel, cap),
        name="expert_ffn",
        grid=(E, B),
        in_specs=[pl.BlockSpec((1, 1, cap, D), lambda e, b: (b, e, 0, 0)), row, row, wspec, wspec, wspec],
        out_specs=pl.BlockSpec((1, 1, cap, D), lambda e, b: (b, e, 0, 0)),
        out_shape=jax.ShapeDtypeStruct((B, E, cap, D), BF16),
        compiler_params=_cparams(("arbitrary", "arbitrary")),
    )(xs, slot.reshape(B, E, nt, BAND_TILE), aff.reshape(B, E, nt, BAND_TILE), wg, wu, wd)


def _combine_kernel(cap, band_ref, x1_ref, slot_ref, ye_ref, mod_ref, gain_ref, o_ref, acc_ref):
    win = min(SLOT_WINDOW, cap)
    log_win = win.bit_length() - 1
    b, i = pl.program_id(0), pl.program_id(1)
    tiles_per = OUT_TILE // BAND_TILE
    jw = lax.broadcasted_iota(jnp.int32, (win, 1), 0)
    contract_rows = (((0,), (0,)), ((), ()))

    def slots_before(e, tile):
        return band_ref[(b * N_EXPERTS + e) * BAND_COLS + tile * tiles_per]

    def window(e, w):
        first = ((slots_before(e, i) >> ROW_PACK_SHIFT) << ROW_PACK_SHIFT) + w * win
        src = pl.multiple_of(jnp.minimum(first, cap - win), ROW_PACK)
        tok = slot_ref[0, e:e + 1, :]
        hit = jnp.logical_and(tok == (src + jw).astype(F32), tok >= first.astype(F32))
        return jnp.where(hit, 1.0, 0.0).astype(BF16), ye_ref[0, e, pl.ds(src, win), :]

    acc = jnp.zeros((OUT_TILE, D_MODEL), F32)
    for e in range(0, N_EXPERTS, 2):
        (h0, r0), (h1, r1) = window(e, 0), window(e + 1, 0)
        acc = acc + lax.dot_general(jnp.concatenate([h0, h1], axis=0), jnp.concatenate([r0, r1], axis=0),
                                    contract_rows, preferred_element_type=F32)
    acc_ref[...] = acc
    for e in range(N_EXPERTS):
        first = (slots_before(e, i) >> ROW_PACK_SHIFT) << ROW_PACK_SHIFT
        n_win = (slots_before(e, i + 1) - first + win - 1) >> log_win

        def more(w, carry, e=e):
            hit, rows = window(e, w)
            acc_ref[...] += lax.dot_general(hit, rows, contract_rows, preferred_element_type=F32)
            return carry

        lax.fori_loop(1, n_win, more, 0)
    acc = acc_ref[...]
    ms = jnp.mean(acc * acc, axis=-1, keepdims=True)
    gt2 = mod_ref[0][:, 5 * D_MODEL:6 * D_MODEL]
    o_ref[0] = x1_ref[0] + gt2 * (acc * lax.rsqrt(ms + NORM_EPS) * gain_ref[...])


def _combine(band, x1, slot, ye, mods, gain, cap):
    B, T, D = x1.shape
    E = ye.shape[1]
    grid_spec = pltpu.PrefetchScalarGridSpec(
        num_scalar_prefetch=1,
        grid=(B, T // OUT_TILE),
        in_specs=[pl.BlockSpec((1, OUT_TILE, D), lambda b, i, band: (b, i, 0)),
                  pl.BlockSpec((1, E, OUT_TILE), lambda b, i, band: (b, 0, i)),
                  pl.BlockSpec((1, E, cap, D), lambda b, i, band: (b, 0, 0, 0)),
                  pl.BlockSpec((1, 1, mods.shape[2]), lambda b, i, band: (b, 0, 0)),
                  pl.BlockSpec((1, D), lambda b, i, band: (0, 0))],
        out_specs=pl.BlockSpec((1, OUT_TILE, D), lambda b, i, band: (b, i, 0)),
        scratch_shapes=[pltpu.VMEM((OUT_TILE, D), F32)])
    return pl.pallas_call(
        functools.partial(_combine_kernel, cap),
        name="combine",
        grid_spec=grid_spec,
        out_shape=jax.ShapeDtypeStruct((B, T, D), F32),
        compiler_params=_cparams(("arbitrary", "arbitrary")),
    )(band, x1, slot, ye, mods, gain)


def _rope_tables(T):
    rows = T // GRID_W
    row = np.repeat(np.arange(rows, dtype=np.float32), GRID_W)
    col = np.tile(np.arange(GRID_W, dtype=np.float32), rows)
    n_freq = HEAD_DIM // 4
    freq = jnp.asarray(ROPE_BASE, F32) ** (-jnp.arange(n_freq, dtype=F32) / n_freq)
    ang = jnp.concatenate([row[:, None] * freq, col[:, None] * freq], axis=-1)
    cos, sin = jnp.cos(ang), jnp.sin(ang)
    reps = LANES // (HEAD_DIM // 2)
    cos_t = jnp.tile(cos, (1, reps))
    sin_t = jnp.tile(jnp.concatenate([-sin, sin], axis=-1), (1, reps // 2))
    return cos_t, sin_t


def kernel(x, c, ctx, c_ctx, w_mod, b_mod, norm_gains, w_in, rwkv_conv, rwkv_w0, rwkv_w2, rwkv_a0, rwkv_a2, rwkv_g2, rwkv_k_k, rwkv_k_a, rwkv_r_k, rwkv_lnx_w, rwkv_lnx_b, w_out, w_router, w_gate, w_up, w_down):
    B, T, D = x.shape
    assert D == D_MODEL and ctx.shape == (B, CTX_LEN, D) and T % OUT_TILE == 0 and w_mod.shape[0] == 1
    W = GROUP_WIDTH
    cap = max(1, EC_CAPACITY * T // N_EXPERTS)
    assert T // BAND_TILE < BAND_COLS

    n_rows = -(-(B + 1) // 8) * 8
    cc = jnp.concatenate([c, c_ctx[None, :], jnp.zeros((n_rows - B - 1, D), F32)], axis=0)
    mods = _modulation(cc, w_mod[0], b_mod).reshape(n_rows, 1, 6 * D)

    split = RET_COLS + 3 * W + 2 * DECAY_LORA + ICLR_LORA
    w_pad = jnp.concatenate([w_in[0][:, :split], jnp.zeros((D, LANES - ICLR_LORA), F32), w_in[0][:, split:]],
                            axis=1).astype(BF16)
    cs = split - RET_COLS
    cw = jnp.concatenate([rwkv_conv[0][:, :cs], jnp.zeros((3, LANES - ICLR_LORA), F32), rwkv_conv[0][:, cs:]], axis=1)
    zl = jnp.zeros((DECAY_LORA, W), F32)
    w2bd = jnp.concatenate([jnp.concatenate([rwkv_w2[0, 0], zl], axis=1),
                            jnp.concatenate([zl, rwkv_w2[0, 1]], axis=1)], axis=0).astype(BF16)
    w0cat = jnp.concatenate([rwkv_w0[0, 0], rwkv_w0[0, 1]])[None, :]
    a2pad = jnp.concatenate([rwkv_a2[0], jnp.zeros((LANES - ICLR_LORA, W), F32)], axis=0).astype(BF16)
    g2f, g2b = rwkv_g2[0, 0].astype(BF16), rwkv_g2[0, 1].astype(BF16)
    row = lambda a: a[0][None, :]
    cos_t, sin_t = _rope_tables(T)

    feature_consts = (cw, w0cat, w2bd, row(rwkv_a0), a2pad, g2f, g2b, row(rwkv_k_k), row(rwkv_k_a), row(rwkv_r_k))
    u_ret, r, k, v, a, b, lwf, lwb, gf, gb, bonus = _projection(
        x, ctx, mods, norm_gains[0, 0][None, :], w_pad, cos_t, sin_t, feature_consts)
    ret_f, ret_b = _retention(u_ret, T)
    rw_f, rw_b = _rwkv_scan(r, k, v, a, b, lwf, lwb, gf, gb, row(rwkv_lnx_w), row(rwkv_lnx_b), T)

    x1, h2, aff = _out_projection(x, ret_f, ret_b, rw_f, rw_b, bonus, w_out[0].astype(BF16), mods,
                                  norm_gains[0], w_router[0].T)
    slot, bounds = _select(aff, cap)
    band = bounds[:, :, :BAND_COLS].astype(jnp.int32).reshape(-1)
    xs = _expert_gather(band, h2, slot, cap)
    ye = _expert_ffn(xs, slot, aff, w_gate[0], w_up[0], w_down[0], cap)
    return _combine(band, x1, slot, ye, mods, norm_gains[0, 3][None, :], cap)
```
